```python
import math
import jax, jax.numpy as jnp
from jax import lax
import numpy as np

D_MODEL = 2048
BATCH = 2
SEQ = 8192
DEPTH = 4

HEAD_DIM = 128
N_HEADS = D_MODEL // HEAD_DIM
N_MOBA_HEADS = N_HEADS // 2
N_SB_HEADS = N_HEADS - N_MOBA_HEADS
N_FOX_HEADS = N_HEADS
MOBA_BLOCK = 256
MOBA_TOPK = 3
MOBA_Q_CHUNK = 32
ATTN_Q_BLOCK = 128
N_EXPERTS = 32
TOP_K = 4
D_EXPERT = (3 * D_MODEL) // 8
SWIGLU_LIMIT = 7.0
SWIGLU_ALPHA = 1.702
RMS_EPS = 1e-5
N_MOD = 6
N_EVEN = (DEPTH + 1) // 2
N_ODD = DEPTH // 2

kernel_name = 'hybrid_moba_stickbreak_fox_moe_adaln'


def alibi_slopes(n_heads):
    return jnp.asarray(np.array([2.0 ** (-8.0 * (i + 1) / n_heads) for i in range(n_heads)], dtype=np.float32))


def rms_norm(x, gain):
    xf = x.astype(jnp.float32)
    y = xf * lax.rsqrt(jnp.mean(xf * xf, axis=-1, keepdims=True) + RMS_EPS)
    return y.astype(x.dtype) * gain


def split_heads(t, n_heads):
    b, s, _ = t.shape
    return t.reshape(b, s, n_heads, HEAD_DIM).transpose(0, 2, 1, 3)


def merge_heads(t):
    b, h, s, dh = t.shape
    return t.transpose(0, 2, 1, 3).reshape(b, s, h * dh)


def moba_attention(q, k, v, slopes):
    B, H, S, Dh = q.shape
    nb = -(-S // MOBA_BLOCK)
    pad = nb * MOBA_BLOCK - S
    kp = jnp.pad(k, ((0, 0), (0, 0), (0, pad), (0, 0)))
    vp = jnp.pad(v, ((0, 0), (0, 0), (0, pad), (0, 0)))
    k_blocks = kp.reshape(B, H, nb, MOBA_BLOCK, Dh)
    v_blocks = vp.reshape(B, H, nb, MOBA_BLOCK, Dh)
    k_mean = jnp.mean(k_blocks.astype(jnp.float32), axis=3).astype(k.dtype)
    topk = min(MOBA_TOPK, nb)
    scale = Dh ** -0.5
    n_chunks = S // MOBA_Q_CHUNK
    q_chunks = q.reshape(B, H, n_chunks, MOBA_Q_CHUNK, Dh).transpose(2, 0, 1, 3, 4)
    b_idx = jnp.arange(B)[:, None, None, None]
    h_idx = jnp.arange(H)[None, :, None, None]
    n_sel = topk * MOBA_BLOCK

    def chunk(args):
        q_c, ci = args
        t = ci * MOBA_Q_CHUNK + jnp.arange(MOBA_Q_CHUNK)
        own = (ci * MOBA_Q_CHUNK) // MOBA_BLOCK
        gate = jnp.einsum('bhqd,bhnd->bhqn', q_c, k_mean).astype(jnp.float32)
        gate = jnp.where(jnp.arange(nb) < own, gate, -jnp.inf)
        _, sel = lax.top_k(gate, topk)
        slot_ok = jnp.arange(topk) < own
        k_sel = k_blocks[b_idx, h_idx, sel]
        v_sel = v_blocks[b_idx, h_idx, sel]
        s_sel = jnp.einsum('bhqd,bhqjpd->bhqjp', q_c, k_sel).astype(jnp.float32) * scale
        pos_sel = sel[..., None] * MOBA_BLOCK + jnp.arange(MOBA_BLOCK)
        s_sel = s_sel - slopes[:, None, None, None] * (t[:, None, None] - pos_sel).astype(jnp.float32)
        s_sel = jnp.where(slot_ok[:, None], s_sel, -jnp.inf).reshape(B, H, MOBA_Q_CHUNK, n_sel)
        k_own = lax.dynamic_slice_in_dim(kp, own * MOBA_BLOCK, MOBA_BLOCK, axis=2)
        v_own = lax.dynamic_slice_in_dim(vp, own * MOBA_BLOCK, MOBA_BLOCK, axis=2)
        pos_own = own * MOBA_BLOCK + jnp.arange(MOBA_BLOCK)
        s_own = jnp.einsum('bhqd,bhpd->bhqp', q_c, k_own).astype(jnp.float32) * scale
        s_own = s_own - slopes[:, None, None] * (t[:, None] - pos_own[None, :]).astype(jnp.float32)
        s_own = jnp.where(pos_own[None, :] <= t[:, None], s_own, -jnp.inf)
        p = jax.nn.softmax(jnp.concatenate([s_sel, s_own], axis=-1), axis=-1)
        p_sel = p[..., :n_sel].reshape(B, H, MOBA_Q_CHUNK, topk, MOBA_BLOCK).astype(v.dtype)
        p_own = p[..., n_sel:].astype(v.dtype)
        return (jnp.einsum('bhqjp,bhqjpd->bhqd', p_sel, v_sel)
                + jnp.einsum('bhqp,bhpd->bhqd', p_own, v_own))

    out = lax.map(chunk, (q_chunks, jnp.arange(n_chunks)))
    return out.transpose(1, 2, 0, 3, 4).reshape(B, H, S, Dh)


def stick_breaking_attention(q, k, v):
    B, H, S, Dh = q.shape
    nq = S // ATTN_Q_BLOCK
    scale = Dh ** -0.5
    q_blocks = q.reshape(B, H, nq, ATTN_Q_BLOCK, Dh).transpose(2, 0, 1, 3, 4)
    s_pos = jnp.arange(S)

    def block(args):
        q_b, bi = args
        t = bi * ATTN_Q_BLOCK + jnp.arange(ATTN_Q_BLOCK)
        z = jnp.einsum('bhqd,bhsd->bhqs', q_b, k).astype(jnp.float32) * scale
        causal = s_pos[None, :] < t[:, None]
        log_keep = jnp.where(causal, jax.nn.log_sigmoid(-z), 0.0)
        tail = lax.cumsum(log_keep, axis=3, reverse=True) - log_keep
        a = jnp.where(causal, jnp.exp(jax.nn.log_sigmoid(z) + tail), 0.0)
        return jnp.einsum('bhqs,bhsd->bhqd', a.astype(v.dtype), v)

    out = lax.map(block, (q_blocks, jnp.arange(nq)))
    return out.transpose(1, 2, 0, 3, 4).reshape(B, H, S, Dh)


def forgetting_attention(q, k, v, log_f):
    B, H, S, Dh = q.shape
    nq = S // ATTN_Q_BLOCK
    scale = Dh ** -0.5
    cum_f = lax.cumsum(log_f, axis=2)
    q_blocks = q.reshape(B, H, nq, ATTN_Q_BLOCK, Dh).transpose(2, 0, 1, 3, 4)
    f_blocks = cum_f.reshape(B, H, nq, ATTN_Q_BLOCK).transpose(2, 0, 1, 3)
    s_pos = jnp.arange(S)

    def block(args):
        q_b, f_q, bi = args
        t = bi * ATTN_Q_BLOCK + jnp.arange(ATTN_Q_BLOCK)
        logits = (jnp.einsum('bhqd,bhsd->bhqs', q_b, k).astype(jnp.float32) * scale
                  + f_q[..., :, None] - cum_f[..., None, :])
        logits = jnp.where(s_pos[None, :] <= t[:, None], logits, -jnp.inf)
        p = jax.nn.softmax(logits, axis=-1)
        return jnp.einsum('bhqs,bhsd->bhqd', p.astype(v.dtype), v)

    out = lax.map(block, (q_blocks, f_blocks, jnp.arange(nq)))
    return out.transpose(1, 2, 0, 3, 4).reshape(B, H, S, Dh)


def moba_sb_mixer(h, w_in, w_out, q_gain, k_gain, slopes):
    wa = N_MOBA_HEADS * HEAD_DIM
    wb = N_SB_HEADS * HEAD_DIM
    proj = h @ w_in
    q_a, k_a, v_a, q_b, k_b, v_b = jnp.split(proj, [wa, 2 * wa, 3 * wa, 3 * wa + wb, 3 * wa + 2 * wb], axis=-1)
    q_a = rms_norm(split_heads(q_a, N_MOBA_HEADS), q_gain)
    k_a = rms_norm(split_heads(k_a, N_MOBA_HEADS), k_gain)
    o_a = moba_attention(q_a, k_a, split_heads(v_a, N_MOBA_HEADS), slopes)
    o_b = stick_breaking_attention(split_heads(q_b, N_SB_HEADS), split_heads(k_b, N_SB_HEADS),
                                   split_heads(v_b, N_SB_HEADS))
    return jnp.concatenate([merge_heads(o_a), merge_heads(o_b)], axis=-1) @ w_out


def fox_mixer(h, w_in, b_f, w_out, q_gain, k_gain):
    w = N_FOX_HEADS * HEAD_DIM
    proj = h @ w_in
    q, k, v, f_logit = jnp.split(proj, [w, 2 * w, 3 * w], axis=-1)
    q = rms_norm(split_heads(q, N_FOX_HEADS), q_gain)
    k = rms_norm(split_heads(k, N_FOX_HEADS), k_gain)
    log_f = jax.nn.log_sigmoid((f_logit + b_f).astype(jnp.float32)).transpose(0, 2, 1)
    o = forgetting_attention(q, k, split_heads(v, N_FOX_HEADS), log_f)
    return merge_heads(o) @ w_out


def moe_ffn(h, w_router, b_router, w_gate, b_gate, w_up, b_up, w_down, b_down):
    B, S, D = h.shape
    tok = h.reshape(B * S, D)
    logits = (tok @ w_router + b_router).astype(jnp.float32)
    top_val, top_idx = lax.top_k(logits, TOP_K)
    top_w = jax.nn.softmax(top_val, axis=-1)
    combine = jnp.sum(jax.nn.one_hot(top_idx, N_EXPERTS, dtype=jnp.float32) * top_w[..., None], axis=1)
    combine = combine.astype(tok.dtype).T

    def expert(acc, p):
        wg, bg, wu, bu, wd, bd, ce = p
        g = jnp.minimum(tok @ wg + bg, SWIGLU_LIMIT)
        u = jnp.clip(tok @ wu + bu, -SWIGLU_LIMIT, SWIGLU_LIMIT)
        y = ((u + 1.0) * (g * jax.nn.sigmoid(SWIGLU_ALPHA * g))) @ wd + bd
        return acc + ce[:, None] * y, None

    out, _ = lax.scan(expert, jnp.zeros_like(tok), (w_gate, b_gate, w_up, b_up, w_down, b_down, combine))
    return out.reshape(B, S, D)


def setup_inputs(seed: int = 0) -> dict:
    key = jax.random.key(seed)
    ks = jax.random.split(key, 24)
    d = D_MODEL
    f32 = jnp.float32

    def nrm(k, shape, scale):
        return jax.random.normal(k, shape, f32) * scale

    w_ab = 3 * (N_MOBA_HEADS + N_SB_HEADS) * HEAD_DIM
    w_ab_out = (N_MOBA_HEADS + N_SB_HEADS) * HEAD_DIM
    w_fox = 3 * N_FOX_HEADS * HEAD_DIM + N_FOX_HEADS
    w_fox_out = N_FOX_HEADS * HEAD_DIM
    return {
        'x': nrm(ks[0], (BATCH, SEQ, d), 1.0),
        'c': nrm(ks[1], (BATCH, d), 1.0),
        'mod_w': nrm(ks[2], (DEPTH, d, N_MOD * d), 0.3 * d ** -0.5),
        'mod_b': nrm(ks[3], (DEPTH, N_MOD * d), 0.02),
        'mix_norm_g': 1.0 + nrm(ks[4], (DEPTH, d), 0.02),
        'ffn_norm_g': 1.0 + nrm(ks[5], (DEPTH, d), 0.02),
        'ab_w_in': nrm(ks[6], (N_EVEN, d, w_ab), d ** -0.5),
        'ab_w_out': nrm(ks[7], (N_EVEN, w_ab_out, d), w_ab_out ** -0.5),
        'moba_q_gain': 1.0 + nrm(ks[8], (N_EVEN, HEAD_DIM), 0.02),
        'moba_k_gain': 1.0 + nrm(ks[9], (N_EVEN, HEAD_DIM), 0.02),
        'fox_w_in': nrm(ks[10], (N_ODD, d, w_fox), d ** -0.5),
        'fox_b_f': 3.0 + nrm(ks[11], (N_ODD, N_FOX_HEADS), 0.5),
        'fox_w_out': nrm(ks[12], (N_ODD, w_fox_out, d), w_fox_out ** -0.5),
        'fox_q_gain': 1.0 + nrm(ks[13], (N_ODD, HEAD_DIM), 0.02),
        'fox_k_gain': 1.0 + nrm(ks[14], (N_ODD, HEAD_DIM), 0.02),
        'router_w': nrm(ks[15], (DEPTH, d, N_EXPERTS), d ** -0.5),
        'router_b': nrm(ks[16], (DEPTH, N_EXPERTS), 0.01),
        'exp_w_gate': nrm(ks[17], (DEPTH, N_EXPERTS, d, D_EXPERT), d ** -0.5),
        'exp_b_gate': nrm(ks[18], (DEPTH, N_EXPERTS, D_EXPERT), 0.02),
        'exp_w_up': nrm(ks[19], (DEPTH, N_EXPERTS, d, D_EXPERT), d ** -0.5),
        'exp_b_up': nrm(ks[20], (DEPTH, N_EXPERTS, D_EXPERT), 0.02),
        'exp_w_down': nrm(ks[21], (DEPTH, N_EXPERTS, D_EXPERT, d), D_EXPERT ** -0.5),
        'exp_b_down': nrm(ks[22], (DEPTH, N_EXPERTS, d), 0.02),
    }


def reference(x, c, mod_w, mod_b, mix_norm_g, ffn_norm_g, ab_w_in, ab_w_out, moba_q_gain, moba_k_gain,
              fox_w_in, fox_b_f, fox_w_out, fox_q_gain, fox_k_gain, router_w, router_b,
              exp_w_gate, exp_b_gate, exp_w_up, exp_b_up, exp_w_down, exp_b_down):
    slopes = alibi_slopes(N_MOBA_HEADS)
    c_act = jax.nn.silu(c)
    for layer in range(DEPTH):
        mod = (c_act @ mod_w[layer] + mod_b[layer])[:, None, :]
        sh1, sc1, g1, sh2, sc2, g2 = jnp.split(mod, N_MOD, axis=-1)
        h = rms_norm(x, mix_norm_g[layer]) * (1.0 + sc1) + sh1
        j = layer // 2
        if layer % 2 == 0:
            y = moba_sb_mixer(h, ab_w_in[j], ab_w_out[j], moba_q_gain[j], moba_k_gain[j], slopes)
        else:
            y = fox_mixer(h, fox_w_in[j], fox_b_f[j], fox_w_out[j], fox_q_gain[j], fox_k_gain[j])
        x = x + g1 * y
        h = rms_norm(x, ffn_norm_g[layer]) * (1.0 + sc2) + sh2
        x = x + g2 * moe_ffn(h, router_w[layer], router_b[layer], exp_w_gate[layer], exp_b_gate[layer],
                             exp_w_up[layer], exp_b_up[layer], exp_w_down[layer], exp_b_down[layer])
    return x
```

```python
import functools

import jax
import jax.numpy as jnp
from jax import lax
from jax.experimental import pallas as pl
from jax.experimental.pallas import tpu as pltpu

F32 = jnp.float32
BF16 = jnp.bfloat16

HEAD_DIM = 128
MOBA_BLOCK = 256
MOBA_TOPK = 3
TOP_K = 4
SWIGLU_LIMIT = 7.0
SWIGLU_ALPHA = 1.702
RMS_EPS = 1e-5
N_MOD = 6
LANES = 128
SUBLANES = 8
MASKED = -1e30
VMEM_LIMIT_BYTES = 56 * 1024 * 1024


def _params(*semantics):
    return pltpu.CompilerParams(dimension_semantics=semantics, vmem_limit_bytes=VMEM_LIMIT_BYTES)


def _dot(a, b):
    return jnp.dot(a, b, preferred_element_type=F32)


def _dot_nt(a, b):
    return lax.dot_general(a, b, (((1,), (1,)), ((), ())), preferred_element_type=F32)


def _split_bf16(x):
    hi = x.astype(BF16)
    lo = (x - hi.astype(F32)).astype(BF16)
    return hi, lo


def _mod_kernel(c_ref, w_ref, b_ref, o_ref):
    c = c_ref[...]
    c_act = c / (1.0 + jnp.exp(-c))
    o_ref[...] = _dot(c_act, w_ref[...]) + b_ref[...]


def _adaln_mod(c, mod_w, mod_b):
    depth, d, n = mod_w.shape
    b = c.shape[0]
    bp = -(-b // SUBLANES) * SUBLANES
    c_pad = jnp.zeros((bp, d), F32).at[:b].set(c)
    tn = min(n, 1024)
    out = pl.pallas_call(
        _mod_kernel,
        grid=(depth, n // tn),
        in_specs=[pl.BlockSpec((bp, d), lambda l, j: (0, 0)),
                  pl.BlockSpec((None, d, tn), lambda l, j: (l, 0, j)),
                  pl.BlockSpec((None, 1, tn), lambda l, j: (l, 0, j))],
        out_specs=pl.BlockSpec((None, bp, tn), lambda l, j: (l, 0, j)),
        out_shape=jax.ShapeDtypeStruct((depth, bp, n), F32),
        compiler_params=_params("parallel", "parallel"),
        name="adaln_mod",
    )(c_pad, mod_w, mod_b.reshape(depth, 1, n))
    return out[:, :b]


def _norm_kernel(x_ref, g_ref, sc_ref, sh_ref, o_ref):
    x = x_ref[...]
    ms = jnp.mean(x * x, axis=-1, keepdims=True)
    y = x * lax.rsqrt(ms + RMS_EPS)
    o_ref[...] = ((y * g_ref[...]) * (1.0 + sc_ref[...]) + sh_ref[...]).astype(o_ref.dtype)


def _norm_mod(x, gain, scale, shift):
    b, s, d = x.shape
    ts = min(s, 512)
    return pl.pallas_call(
        _norm_kernel,
        grid=(b, s // ts),
        in_specs=[pl.BlockSpec((None, ts, d), lambda i, j: (i, j, 0)),
                  pl.BlockSpec((1, d), lambda i, j: (0, 0)),
                  pl.BlockSpec((None, 1, d), lambda i, j: (i, 0, 0)),
                  pl.BlockSpec((None, 1, d), lambda i, j: (i, 0, 0))],
        out_specs=pl.BlockSpec((None, ts, d), lambda i, j: (i, j, 0)),
        out_shape=jax.ShapeDtypeStruct((b, s, d), BF16),
        compiler_params=_params("parallel", "parallel"),
        name="norm_mod",
    )(x, gain.reshape(1, d), scale.reshape(b, 1, d), shift.reshape(b, 1, d))


def _inproj_kernel(h_ref, w_ref, gain_ref, o_ref, *, n_norm_tiles):
    acc = _dot(h_ref[...], w_ref[...])
    j = pl.program_id(0)

    @pl.when(j < n_norm_tiles)
    def _():
        for g in range(acc.shape[1] // HEAD_DIM):
            sl = slice(g * HEAD_DIM, (g + 1) * HEAD_DIM)
            blk = acc[:, sl]
            ms = jnp.mean(blk * blk, axis=-1, keepdims=True)
            o_ref[:, sl] = (blk * lax.rsqrt(ms + RMS_EPS) * gain_ref[:, sl]).astype(o_ref.dtype)

    @pl.when(j >= n_norm_tiles)
    def _():
        o_ref[...] = acc.astype(o_ref.dtype)


def _in_proj(h, w, gain_cols, n_norm_cols):
    t, d = h.shape
    n = w.shape[1]
    tm = min(t, 1024)
    tn = min(n, 1024)
    assert n_norm_cols % tn == 0 and n % tn == 0 and t % tm == 0
    return pl.pallas_call(
        functools.partial(_inproj_kernel, n_norm_tiles=n_norm_cols // tn),
        grid=(n // tn, t // tm),
        in_specs=[pl.BlockSpec((tm, d), lambda j, i: (i, 0)),
                  pl.BlockSpec((d, tn), lambda j, i: (0, j)),
                  pl.BlockSpec((1, tn), lambda j, i: (0, j))],
        out_specs=pl.BlockSpec((tm, tn), lambda j, i: (i, j)),
        out_shape=jax.ShapeDtypeStruct((t, n), BF16),
        compiler_params=_params("parallel", "parallel"),
        name="in_proj",
    )(h, w, gain_cols)


def _fgate_kernel(h_ref, wf_ref, bf_ref, o_ref, carry_ref):
    @pl.when(pl.program_id(1) == 0)
    def _():
        carry_ref[...] = jnp.zeros_like(carry_ref)

    ts = h_ref.shape[0]
    logit = _dot_nt(wf_ref[...], h_ref[...]) + bf_ref[...]
    log_f = jnp.minimum(logit, 0.0) - jnp.log(1.0 + jnp.exp(-jnp.abs(logit)))
    row = lax.broadcasted_iota(jnp.int32, (ts, ts), 0)
    col = lax.broadcasted_iota(jnp.int32, (ts, ts), 1)
    upper = jnp.where(row <= col, 1.0, 0.0).astype(BF16)
    hi, lo = _split_bf16(log_f)
    cum = _dot(hi, upper) + _dot(lo, upper) + carry_ref[:, 0:1]
    o_ref[...] = cum
    carry_ref[...] = jnp.broadcast_to(cum[:, ts - 1:ts], carry_ref.shape)


def _forget_cumsum(h, w_f, b_f):
    b, s, d = h.shape
    nh = w_f.shape[1]
    ts = min(s, 512)
    return pl.pallas_call(
        _fgate_kernel,
        grid=(b, s // ts),
        in_specs=[pl.BlockSpec((None, ts, d), lambda i, j: (i, j, 0)),
                  pl.BlockSpec((nh, d), lambda i, j: (0, 0)),
                  pl.BlockSpec((nh, 1), lambda i, j: (0, 0))],
        out_specs=pl.BlockSpec((None, nh, ts), lambda i, j: (i, 0, j)),
        out_shape=jax.ShapeDtypeStruct((b, nh, s), F32),
        scratch_shapes=[pltpu.VMEM((nh, LANES), F32)],
        compiler_params=_params("parallel", "arbitrary"),
        name="forget_cumsum",
    )(h, w_f.T.astype(BF16), b_f.reshape(nh, 1).astype(F32))


def _softmax_init(s, v, m_ref, l_ref, acc_ref):
    m = jnp.max(s, axis=-1, keepdims=True)
    p = jnp.exp(s - m)
    m_ref[...] = m
    l_ref[...] = jnp.sum(p, axis=-1, keepdims=True)
    acc_ref[...] = _dot(p.astype(BF16), v)


def _softmax_update(s, v, m_ref, l_ref, acc_ref):
    m_old = m_ref[...]
    m_new = jnp.maximum(m_old, jnp.max(s, axis=-1, keepdims=True))
    alpha = jnp.exp(m_old - m_new)
    p = jnp.exp(s - m_new)
    m_ref[...] = m_new
    l_ref[...] = alpha * l_ref[...] + jnp.sum(p, axis=-1, keepdims=True)
    acc_ref[...] = alpha * acc_ref[...] + _dot(p.astype(BF16), v)


def _kv_block(ref, j, tk):
    return ref[pl.ds(pl.multiple_of(j * tk, tk), tk), :]


def _fox_kernel(q_ref, k_ref, v_ref, f_ref, o_ref, m_ref, l_ref, acc_ref, *, tile, scale):
    qi = pl.program_id(2)
    q = q_ref[...]

    def scores(j):
        return _dot_nt(q, _kv_block(k_ref, j, tile)) * scale - f_ref[j]

    row = lax.broadcasted_iota(jnp.int32, (tile, tile), 0)
    col = lax.broadcasted_iota(jnp.int32, (tile, tile), 1)
    s_diag = jnp.where(col <= row, scores(qi), MASKED)
    _softmax_init(s_diag, _kv_block(v_ref, qi, tile), m_ref, l_ref, acc_ref)

    def body(j, carry):
        _softmax_update(scores(j), _kv_block(v_ref, j, tile), m_ref, l_ref, acc_ref)
        return carry

    lax.fori_loop(0, qi, body, 0)
    o_ref[...] = (acc_ref[...] / l_ref[...]).astype(o_ref.dtype)


def _fox_attention(proj, cum_f, n_heads):
    b, s, _ = proj.shape
    tile = min(s, 512)
    nkb = s // tile
    f_blocks = cum_f.reshape(b, n_heads, nkb, 1, tile)
    return pl.pallas_call(
        functools.partial(_fox_kernel, tile=tile, scale=HEAD_DIM ** -0.5),
        grid=(b, n_heads, s // tile),
        in_specs=[pl.BlockSpec((None, tile, HEAD_DIM), lambda i, h, t: (i, t, h)),
                  pl.BlockSpec((None, s, HEAD_DIM), lambda i, h, t: (i, 0, n_heads + h)),
                  pl.BlockSpec((None, s, HEAD_DIM), lambda i, h, t: (i, 0, 2 * n_heads + h)),
                  pl.BlockSpec((None, None, nkb, 1, tile), lambda i, h, t: (i, h, 0, 0, 0))],
        out_specs=pl.BlockSpec((None, tile, HEAD_DIM), lambda i, h, t: (i, t, h)),
        out_shape=jax.ShapeDtypeStruct((b, s, n_heads * HEAD_DIM), BF16),
        scratch_shapes=[pltpu.VMEM((tile, 1), F32), pltpu.VMEM((tile, 1), F32),
                        pltpu.VMEM((tile, HEAD_DIM), F32)],
        compiler_params=_params("parallel", "parallel", "arbitrary"),
        name="fox_attention",
    )(proj, proj, proj, f_blocks)


def _moba_kernel(q_ref, k_ref, v_ref, slope_ref, o_ref, kmean_ref, sel_ref, m_ref, l_ref, acc_ref,
                 *, n_blocks, scale):
    blk = MOBA_BLOCK
    qi = pl.program_id(2)

    @pl.when(qi == 0)
    def _():
        kmean_ref[...] = jnp.zeros_like(kmean_ref)

        def mean_body(n, carry):
            kb = _kv_block(k_ref, n, blk).astype(F32)
            kmean_ref[pl.ds(n, 1), :] = jnp.mean(kb, axis=0, keepdims=True)
            return carry

        lax.fori_loop(0, n_blocks, mean_body, 0)

    q = q_ref[...]
    lane = lax.broadcasted_iota(jnp.int32, (blk, LANES), 1).astype(F32)
    own = qi.astype(F32)
    gate = _dot_nt(q, kmean_ref[...].astype(BF16))
    gate = jnp.where(lane < own, gate, -jnp.inf)
    sel = jnp.zeros((blk, LANES), F32)
    for slot in range(MOBA_TOPK):
        best = jnp.max(gate, axis=-1, keepdims=True)
        idx = jnp.min(jnp.where(gate == best, lane, float(LANES)), axis=-1, keepdims=True)
        hit = lane == idx
        slot_ok = jnp.where(slot < qi, 1.0, 0.0)
        sel = jnp.maximum(sel, jnp.where(hit, slot_ok, 0.0))
        gate = jnp.where(hit, -jnp.inf, gate)
    sel_ref[...] = sel

    row = lax.broadcasted_iota(jnp.int32, (blk, blk), 0)
    col = lax.broadcasted_iota(jnp.int32, (blk, blk), 1)
    slope = slope_ref[...]
    alibi = slope * (col - row).astype(F32)

    def scores(j):
        dist = ((qi - j) * blk).astype(F32)
        return _dot_nt(q, _kv_block(k_ref, j, blk)) * scale + (alibi - slope * dist)

    s_own = jnp.where(col <= row, scores(qi), MASKED)
    _softmax_init(s_own, _kv_block(v_ref, qi, blk), m_ref, l_ref, acc_ref)

    def body(j, carry):
        picked = jnp.sum(jnp.where(lane == j.astype(F32), sel_ref[...], 0.0), axis=-1, keepdims=True)
        s = jnp.where(picked > 0.5, scores(j), MASKED)
        _softmax_update(s, _kv_block(v_ref, j, blk), m_ref, l_ref, acc_ref)
        return carry

    lax.fori_loop(0, qi, body, 0)
    o_ref[...] = (acc_ref[...] / l_ref[...]).astype(o_ref.dtype)


def _moba_attention(proj, n_heads, col_block):
    b, s, _ = proj.shape
    blk = MOBA_BLOCK
    assert s % blk == 0 and s // blk <= LANES
    slopes = jnp.asarray([2.0 ** (-8.0 * (i + 1) / n_heads) for i in range(n_heads)], F32)
    slope_rows = jnp.broadcast_to(slopes[:, None, None], (n_heads, 1, blk))
    return pl.pallas_call(
        functools.partial(_moba_kernel, n_blocks=s // blk, scale=HEAD_DIM ** -0.5),
        grid=(b, n_heads, s // blk),
        in_specs=[pl.BlockSpec((None, blk, HEAD_DIM), lambda i, h, t: (i, t, col_block + h)),
                  pl.BlockSpec((None, s, HEAD_DIM), lambda i, h, t: (i, 0, col_block + n_heads + h)),
                  pl.BlockSpec((None, s, HEAD_DIM), lambda i, h, t: (i, 0, col_block + 2 * n_heads + h)),
                  pl.BlockSpec((None, 1, blk), lambda i, h, t: (h, 0, 0))],
        out_specs=pl.BlockSpec((None, blk, HEAD_DIM), lambda i, h, t: (i, t, h)),
        out_shape=jax.ShapeDtypeStruct((b, s, n_heads * HEAD_DIM), BF16),
        scratch_shapes=[pltpu.VMEM((LANES, HEAD_DIM), F32), pltpu.VMEM((blk, LANES), F32),
                        pltpu.VMEM((blk, 1), F32), pltpu.VMEM((blk, 1), F32),
                        pltpu.VMEM((blk, HEAD_DIM), F32)],
        compiler_params=_params("parallel", "parallel", "arbitrary"),
        name="moba_attention",
    )(proj, proj, proj, slope_rows)


def _sb_kernel(q_ref, k_ref, v_ref, o_ref, tail_ref, acc_ref, *, tile, scale):
    qi = pl.program_id(2)
    q = q_ref[...]
    row = lax.broadcasted_iota(jnp.int32, (tile, tile), 0)
    col = lax.broadcasted_iota(jnp.int32, (tile, tile), 1)
    after = jnp.where(row > col, 1.0, 0.0).astype(BF16)

    def block(j, causal):
        z = _dot_nt(q, _kv_block(k_ref, j, tile)) * scale
        log_keep = -(jnp.maximum(z, 0.0) + jnp.log(1.0 + jnp.exp(-jnp.abs(z))))
        if causal is not None:
            log_keep = jnp.where(causal, log_keep, 0.0)
        hi, lo = _split_bf16(log_keep)
        tail = _dot(hi, after) + _dot(lo, after) + tail_ref[...]
        a = jnp.exp(z + log_keep + tail)
        if causal is not None:
            a = jnp.where(causal, a, 0.0)
        acc_ref[...] += _dot(a.astype(BF16), _kv_block(v_ref, j, tile))
        tail_ref[...] += jnp.sum(log_keep, axis=-1, keepdims=True)

    tail_ref[...] = jnp.zeros_like(tail_ref)
    acc_ref[...] = jnp.zeros_like(acc_ref)
    block(qi, col < row)

    def body(jj, carry):
        block(qi - 1 - jj, None)
        return carry

    lax.fori_loop(0, qi, body, 0)
    o_ref[...] = acc_ref[...].astype(o_ref.dtype)


def _sb_attention(proj, n_heads, col_block):
    b, s, _ = proj.shape
    tile = min(s, 256)
    return pl.pallas_call(
        functools.partial(_sb_kernel, tile=tile, scale=HEAD_DIM ** -0.5),
        grid=(b, n_heads, s // tile),
        in_specs=[pl.BlockSpec((None, tile, HEAD_DIM), lambda i, h, t: (i, t, col_block + h)),
                  pl.BlockSpec((None, s, HEAD_DIM), lambda i, h, t: (i, 0, col_block + n_heads + h)),
                  pl.BlockSpec((None, s, HEAD_DIM), lambda i, h, t: (i, 0, col_block + 2 * n_heads + h))],
        out_specs=pl.BlockSpec((None, tile, HEAD_DIM), lambda i, h, t: (i, t, h)),
        out_shape=jax.ShapeDtypeStruct((b, s, n_heads * HEAD_DIM), BF16),
        scratch_shapes=[pltpu.VMEM((tile, 1), F32), pltpu.VMEM((tile, HEAD_DIM), F32)],
        compiler_params=_params("parallel", "parallel", "arbitrary"),
        name="sb_attention",
    )(proj, proj, proj)


def _outproj_kernel(*refs, n_parts):
    x_ref, gate_ref = refs[0], refs[1]
    parts = refs[2:2 + 2 * n_parts]
    o_ref = refs[2 + 2 * n_parts]
    y = _dot(parts[0][...], parts[1][...])
    for p in range(1, n_parts):
        y += _dot(parts[2 * p][...], parts[2 * p + 1][...])
    o_ref[...] = x_ref[...] + gate_ref[...] * y


def _out_proj_residual(x, gate, parts):
    b, s, d = x.shape
    ts = min(s, 512)
    in_specs = [pl.BlockSpec((None, ts, d), lambda i, j: (i, j, 0)),
                pl.BlockSpec((None, 1, d), lambda i, j: (i, 0, 0))]
    args = [x, gate.reshape(b, 1, d)]
    for a, w in parts:
        kp = a.shape[-1]
        in_specs += [pl.BlockSpec((None, ts, kp), lambda i, j: (i, j, 0)),
                     pl.BlockSpec((kp, d), lambda i, j: (0, 0))]
        args += [a, w]
    return pl.pallas_call(
        functools.partial(_outproj_kernel, n_parts=len(parts)),
        grid=(b, s // ts),
        in_specs=in_specs,
        out_specs=pl.BlockSpec((None, ts, d), lambda i, j: (i, j, 0)),
        out_shape=jax.ShapeDtypeStruct((b, s, d), F32),
        compiler_params=_params("parallel", "parallel"),
        name="out_proj",
    )(*args)


def _router_kernel(h_ref, w_ref, b_ref, idx_ref, wgt_ref, rank_ref, cnt_ref, carry_ref):
    @pl.when(pl.program_id(0) == 0)
    def _():
        carry_ref[...] = jnp.zeros_like(carry_ref)

    tm = h_ref.shape[0]
    logits = _dot(h_ref[...], w_ref[...]) + b_ref[...]
    lane = lax.broadcasted_iota(jnp.int32, (tm, LANES), 1).astype(F32)
    vals, idxs = [], []
    onehot = jnp.zeros((tm, LANES), F32)
    for _ in range(TOP_K):
        best = jnp.max(logits, axis=-1, keepdims=True)
        idx = jnp.min(jnp.where(logits == best, lane, float(LANES)), axis=-1, keepdims=True)
        hit = lane == idx
        onehot = jnp.where(hit, 1.0, onehot)
        logits = jnp.where(hit, -jnp.inf, logits)
        vals.append(best)
        idxs.append(idx)
    exps = [jnp.exp(v - vals[0]) for v in vals]
    denom = exps[0]
    for e in exps[1:]:
        denom = denom + e

    row = lax.broadcasted_iota(jnp.int32, (tm, tm), 0)
    col = lax.broadcasted_iota(jnp.int32, (tm, tm), 1)
    before = jnp.where(col < row, 1.0, 0.0).astype(BF16)
    prior = _dot(before, onehot.astype(BF16)) + carry_ref[0:1, :]

    idx_out = jnp.zeros((tm, LANES), F32)
    wgt_out = jnp.zeros((tm, LANES), F32)
    rank_out = jnp.zeros((tm, LANES), F32)
    for k in range(TOP_K):
        rank_k = jnp.sum(jnp.where(lane == idxs[k], prior, 0.0), axis=-1, keepdims=True)
        slot = lane == float(k)
        idx_out = jnp.where(slot, idxs[k], idx_out)
        wgt_out = jnp.where(slot, exps[k] / denom, wgt_out)
        rank_out = jnp.where(slot, rank_k, rank_out)
    idx_ref[...] = idx_out.astype(jnp.int32)
    wgt_ref[...] = wgt_out
    rank_ref[...] = rank_out.astype(jnp.int32)
    counts = carry_ref[...] + jnp.sum(onehot, axis=0, keepdims=True)
    carry_ref[...] = counts
    cnt_ref[...] = counts.astype(jnp.int32)


def _route(h, w_router, b_router):
    t, d = h.shape
    e = w_router.shape[1]
    assert e <= LANES
    tm = min(t, 512)
    w_pad = jnp.zeros((d, LANES), BF16).at[:, :e].set(w_router.astype(BF16))
    b_pad = jnp.full((1, LANES), MASKED, F32).at[0, :e].set(b_router)
    tok_spec = pl.BlockSpec((tm, LANES), lambda i: (i, 0))
    return pl.pallas_call(
        _router_kernel,
        grid=(t // tm,),
        in_specs=[pl.BlockSpec((tm, d), lambda i: (i, 0)),
                  pl.BlockSpec((d, LANES), lambda i: (0, 0)),
                  pl.BlockSpec((1, LANES), lambda i: (0, 0))],
        out_specs=[tok_spec, tok_spec, tok_spec, pl.BlockSpec((SUBLANES, LANES), lambda i: (0, 0))],
        out_shape=[jax.ShapeDtypeStruct((t, LANES), jnp.int32), jax.ShapeDtypeStruct((t, LANES), F32),
                   jax.ShapeDtypeStruct((t, LANES), jnp.int32),
                   jax.ShapeDtypeStruct((SUBLANES, LANES), jnp.int32)],
        scratch_shapes=[pltpu.VMEM((SUBLANES, LANES), F32)],
        compiler_params=_params("arbitrary"),
        name="moe_router",
    )(h, w_pad, b_pad)


def _moe_up_kernel(te_ref, first_ref, valid_ref, x_ref, wg_ref, bg_ref, wu_ref, bu_ref, o_ref,
                   wg_bf, wu_bf):
    i = pl.program_id(0)

    @pl.when(first_ref[i] == 1)
    def _():
        wg_bf[...] = wg_ref[...].astype(BF16)
        wu_bf[...] = wu_ref[...].astype(BF16)

    @pl.when(valid_ref[i] == 1)
    def _():
        x = x_ref[...]
        g = jnp.minimum(_dot(x, wg_bf[...]) + bg_ref[...], SWIGLU_LIMIT)
        u = jnp.clip(_dot(x, wu_bf[...]) + bu_ref[...], -SWIGLU_LIMIT, SWIGLU_LIMIT)
        act = (u + 1.0) * (g / (1.0 + jnp.exp(-SWIGLU_ALPHA * g)))
        o_ref[...] = act.astype(o_ref.dtype)

    @pl.when(valid_ref[i] == 0)
    def _():
        o_ref[...] = jnp.zeros_like(o_ref)


def _moe_down_kernel(te_ref, first_ref, valid_ref, a_ref, wd_ref, bd_ref, o_ref, wd_bf):
    i = pl.program_id(0)

    @pl.when(first_ref[i] == 1)
    def _():
        wd_bf[...] = wd_ref[...].astype(BF16)

    @pl.when(valid_ref[i] == 1)
    def _():
        o_ref[...] = (_dot(a_ref[...], wd_bf[...]) + bd_ref[...]).astype(o_ref.dtype)

    @pl.when(valid_ref[i] == 0)
    def _():
        o_ref[...] = jnp.zeros_like(o_ref)


def _expert_spec(shape):
    return pl.BlockSpec((None,) + shape, lambda i, te, first, valid: (te[i], 0, 0))


def _moe_experts(xs, tile_expert, tile_first, tile_valid, w_gate, b_gate, w_up, b_up, w_down, b_down, tm):
    p, d = xs.shape
    e, _, f = w_gate.shape
    n_tiles = p // tm
    row_spec = lambda width: pl.BlockSpec((tm, width), lambda i, te, first, valid: (i, 0))
    act = pl.pallas_call(
        _moe_up_kernel,
        grid_spec=pltpu.PrefetchScalarGridSpec(
            num_scalar_prefetch=3, grid=(n_tiles,),
            in_specs=[row_spec(d), _expert_spec((d, f)), _expert_spec((1, f)),
                      _expert_spec((d, f)), _expert_spec((1, f))],
            out_specs=row_spec(f),
            scratch_shapes=[pltpu.VMEM((d, f), BF16), pltpu.VMEM((d, f), BF16)]),
        out_shape=jax.ShapeDtypeStruct((p, f), BF16),
        compiler_params=_params("arbitrary"),
        name="moe_up",
    )(tile_expert, tile_first, tile_valid, xs, w_gate, b_gate.reshape(e, 1, f), w_up, b_up.reshape(e, 1, f))
    return pl.pallas_call(
        _moe_down_kernel,
        grid_spec=pltpu.PrefetchScalarGridSpec(
            num_scalar_prefetch=3, grid=(n_tiles,),
            in_specs=[row_spec(f), _expert_spec((f, d)), _expert_spec((1, d))],
            out_specs=row_spec(d),
            scratch_shapes=[pltpu.VMEM((f, d), BF16)]),
        out_shape=jax.ShapeDtypeStruct((p, d), BF16),
        compiler_params=_params("arbitrary"),
        name="moe_down",
    )(tile_expert, tile_first, tile_valid, act, w_down, b_down.reshape(e, 1, d))


def _combine_kernel(x_ref, gate_ref, y_ref, w_ref, o_ref):
    w = w_ref[...]
    y = w[:, 0:1] * y_ref[0].astype(F32)
    for k in range(1, TOP_K):
        y += w[:, k:k + 1] * y_ref[k].astype(F32)
    o_ref[...] = x_ref[...] + gate_ref[...] * y


def _moe_combine(x, gate, y_sel, weights):
    b, s, d = x.shape
    ts = min(s, 256)
    per_b = s // ts
    return pl.pallas_call(
        _combine_kernel,
        grid=(b, per_b),
        in_specs=[pl.BlockSpec((None, ts, d), lambda i, j: (i, j, 0)),
                  pl.BlockSpec((None, 1, d), lambda i, j: (i, 0, 0)),
                  pl.BlockSpec((TOP_K, ts, d), lambda i, j: (0, i * per_b + j, 0)),
                  pl.BlockSpec((ts, LANES), lambda i, j: (i * per_b + j, 0))],
        out_specs=pl.BlockSpec((None, ts, d), lambda i, j: (i, j, 0)),
        out_shape=jax.ShapeDtypeStruct((b, s, d), F32),
        compiler_params=_params("parallel", "parallel"),
        name="moe_combine",
    )(x, gate.reshape(b, 1, d), y_sel, weights)


def _moe_ffn(x, h, gate, w_router, b_router, w_gate, b_gate, w_up, b_up, w_down, b_down):
    b, s, d = x.shape
    t = b * s
    e = w_router.shape[1]
    tm = 256
    h2 = h.reshape(t, d)
    idx_pad, wgt_pad, rank_pad, counts_pad = _route(h2, w_router, b_router)
    idx = idx_pad[:, :TOP_K]
    counts = counts_pad[0, :e]

    padded = ((counts + tm - 1) // tm) * tm
    ends = jnp.cumsum(padded)
    starts = ends - padded
    pos = starts[idx] + rank_pad[:, :TOP_K]
    n_slots = t * TOP_K + e * tm
    n_tiles = n_slots // tm
    tile_start = jnp.arange(n_tiles, dtype=jnp.int32) * tm
    tile_expert = jnp.minimum(jnp.searchsorted(ends, tile_start, side="right"), e - 1).astype(jnp.int32)
    tile_valid = (tile_start < ends[-1]).astype(jnp.int32)
    tile_first = jnp.concatenate([jnp.ones((1,), jnp.int32),
                                  (tile_expert[1:] != tile_expert[:-1]).astype(jnp.int32)])
    token_ids = jnp.repeat(jnp.arange(t, dtype=jnp.int32), TOP_K)
    slot_token = jnp.zeros((n_slots,), jnp.int32).at[pos.reshape(-1)].set(token_ids)

    xs = jnp.take(h2, slot_token, axis=0)
    ys = _moe_experts(xs, tile_expert, tile_first, tile_valid,
                      w_gate, b_gate, w_up, b_up, w_down, b_down, tm)
    y_sel = jnp.take(ys, pos.T, axis=0)
    return _moe_combine(x, gate, y_sel, wgt_pad)


def kernel(x, c, mod_w, mod_b, mix_norm_g, ffn_norm_g, ab_w_in, ab_w_out, moba_q_gain, moba_k_gain,
           fox_w_in, fox_b_f, fox_w_out, fox_q_gain, fox_k_gain, router_w, router_b,
           exp_w_gate, exp_b_gate, exp_w_up, exp_b_up, exp_w_down, exp_b_down):
    b, s, d = x.shape
    depth = mod_w.shape[0]
    n_heads = d // HEAD_DIM
    n_moba = n_heads // 2
    n_sb = n_heads - n_moba
    mod = _adaln_mod(c, mod_w, mod_b)

    for layer in range(depth):
        sh1, sc1, g1, sh2, sc2, g2 = [mod[layer, :, i * d:(i + 1) * d] for i in range(N_MOD)]
        j = layer // 2
        h = _norm_mod(x, mix_norm_g[layer], sc1, sh1)
        if layer % 2 == 0:
            wa = n_moba * HEAD_DIM
            n_cols = ab_w_in.shape[2]
            gains = jnp.concatenate([jnp.tile(moba_q_gain[j], n_moba), jnp.tile(moba_k_gain[j], n_moba),
                                     jnp.ones((n_cols - 2 * wa,), F32)]).reshape(1, n_cols)
            proj = _in_proj(h.reshape(b * s, d), ab_w_in[j].astype(BF16), gains, 2 * wa)
            proj = proj.reshape(b, s, n_cols)
            o_a = _moba_attention(proj, n_moba, 0)
            o_b = _sb_attention(proj, n_sb, 3 * n_moba)
            w_out = ab_w_out[j].astype(BF16)
            x = _out_proj_residual(x, g1, [(o_a, w_out[:wa]), (o_b, w_out[wa:])])
        else:
            w = n_heads * HEAD_DIM
            gains = jnp.concatenate([jnp.tile(fox_q_gain[j], n_heads), jnp.tile(fox_k_gain[j], n_heads),
                                     jnp.ones((w,), F32)]).reshape(1, 3 * w)
            proj = _in_proj(h.reshape(b * s, d), fox_w_in[j, :, :3 * w].astype(BF16), gains, 2 * w)
            cum_f = _forget_cumsum(h, fox_w_in[j, :, 3 * w:], fox_b_f[j])
            o = _fox_attention(proj.reshape(b, s, 3 * w), cum_f, n_heads)
            x = _out_proj_residual(x, g1, [(o, fox_w_out[j].astype(BF16))])
        h = _norm_mod(x, ffn_norm_g[layer], sc2, sh2)
        x = _moe_ffn(x, h, g2, router_w[layer], router_b[layer], exp_w_gate[layer], exp_b_gate[layer],
                     exp_w_up[layer], exp_b_up[layer], exp_w_down[layer], exp_b_down[layer])
    return x
```

```python
import functools
import math

import jax
import jax.numpy as jnp
from jax import lax
from jax.experimental import pallas as pl
from jax.experimental.pallas import tpu as pltpu

F32 = jnp.float32
BF16 = jnp.bfloat16

HEAD_DIM = 128
MOBA_BLOCK = 256
MOBA_TOPK = 3
TOP_K = 4
SWIGLU_LIMIT = 7.0
SWIGLU_ALPHA = 1.702
RMS_EPS = 1e-5
N_MOD = 6
LOG2E = math.log2(math.e)
Q_PRESCALE = HEAD_DIM ** -0.5 * LOG2E
LANES = 128
SUBLANES = 8
BF16_ROWS = 16
MASKED = -1e30
SB_EXIT_BITS = 160.0
VMEM_LIMIT_BYTES = 56 * 1024 * 1024


def _params(*semantics):
    return pltpu.CompilerParams(dimension_semantics=semantics, vmem_limit_bytes=VMEM_LIMIT_BYTES)


def _dot(a, b):
    return jnp.dot(a, b, preferred_element_type=F32)


def _dot_nt(a, b):
    return lax.dot_general(a, b, (((1,), (1,)), ((), ())), preferred_element_type=F32)


def _split_bf16(x):
    hi = x.astype(BF16)
    lo = (x - hi.astype(F32)).astype(BF16)
    return hi, lo


def _kv_block(ref, j, tk):
    return ref[pl.ds(pl.multiple_of(j * tk, tk), tk), :]


def _mod_kernel(c_ref, w_ref, b_ref, o_ref):
    c = c_ref[...]
    c_act = c / (1.0 + jnp.exp(-c))
    o_ref[...] = _dot(c_act, w_ref[...]) + b_ref[...]


def _adaln_mod(c, mod_w, mod_b):
    depth, d, n = mod_w.shape
    b = c.shape[0]
    bp = -(-b // SUBLANES) * SUBLANES
    c_pad = jnp.zeros((bp, d), F32).at[:b].set(c)
    tn = min(n, 1024)
    out = pl.pallas_call(
        _mod_kernel,
        grid=(depth, n // tn),
        in_specs=[pl.BlockSpec((bp, d), lambda l, j: (0, 0)),
                  pl.BlockSpec((None, d, tn), lambda l, j: (l, 0, j)),
                  pl.BlockSpec((None, 1, tn), lambda l, j: (l, 0, j))],
        out_specs=pl.BlockSpec((None, bp, tn), lambda l, j: (l, 0, j)),
        out_shape=jax.ShapeDtypeStruct((depth, bp, n), F32),
        compiler_params=_params("parallel", "parallel"),
        name="adaln_mod",
    )(c_pad, mod_w, mod_b.reshape(depth, 1, n))
    return out[:, :b]


def _norm_kernel(x_ref, g_ref, sc_ref, sh_ref, o_ref):
    x = x_ref[...]
    ms = jnp.mean(x * x, axis=-1, keepdims=True)
    y = x * lax.rsqrt(ms + RMS_EPS)
    o_ref[...] = ((y * g_ref[...]) * (1.0 + sc_ref[...]) + sh_ref[...]).astype(o_ref.dtype)


def _norm_mod(x, gain, scale, shift):
    b, s, d = x.shape
    ts = min(s, 512)
    return pl.pallas_call(
        _norm_kernel,
        grid=(b, s // ts),
        in_specs=[pl.BlockSpec((None, ts, d), lambda i, j: (i, j, 0)),
                  pl.BlockSpec((1, d), lambda i, j: (0, 0)),
                  pl.BlockSpec((None, 1, d), lambda i, j: (i, 0, 0)),
                  pl.BlockSpec((None, 1, d), lambda i, j: (i, 0, 0))],
        out_specs=pl.BlockSpec((None, ts, d), lambda i, j: (i, j, 0)),
        out_shape=jax.ShapeDtypeStruct((b, s, d), BF16),
        compiler_params=_params("parallel", "parallel"),
        name="norm_mod",
    )(x, gain.reshape(1, d), scale.reshape(b, 1, d), shift.reshape(b, 1, d))


def _inproj_kernel(h_ref, w_ref, gain_ref, o_ref, *, n_norm_tiles):
    acc = _dot(h_ref[...], w_ref[...])
    j = pl.program_id(0)

    @pl.when(j < n_norm_tiles)
    def _():
        for g in range(acc.shape[1] // HEAD_DIM):
            sl = slice(g * HEAD_DIM, (g + 1) * HEAD_DIM)
            blk = acc[:, sl]
            ms = jnp.mean(blk * blk, axis=-1, keepdims=True)
            o_ref[:, sl] = (blk * lax.rsqrt(ms + RMS_EPS) * gain_ref[:, sl]).astype(o_ref.dtype)

    @pl.when(j >= n_norm_tiles)
    def _():
        o_ref[...] = (acc * gain_ref[...]).astype(o_ref.dtype)


def _in_proj(h, w, col_scale, n_norm_cols):
    t, d = h.shape
    n = w.shape[1]
    tm = min(t, 1024)
    tn = min(n, 1024)
    assert n_norm_cols % tn == 0 and n % tn == 0 and t % tm == 0
    return pl.pallas_call(
        functools.partial(_inproj_kernel, n_norm_tiles=n_norm_cols // tn),
        grid=(n // tn, t // tm),
        in_specs=[pl.BlockSpec((tm, d), lambda j, i: (i, 0)),
                  pl.BlockSpec((d, tn), lambda j, i: (0, j)),
                  pl.BlockSpec((1, tn), lambda j, i: (0, j))],
        out_specs=pl.BlockSpec((tm, tn), lambda j, i: (i, j)),
        out_shape=jax.ShapeDtypeStruct((t, n), BF16),
        compiler_params=_params("parallel", "parallel"),
        name="in_proj",
    )(h, w, col_scale)


def _fgate_kernel(h_ref, wf_ref, bf_ref, o_ref, carry_ref):
    @pl.when(pl.program_id(1) == 0)
    def _():
        carry_ref[...] = jnp.zeros_like(carry_ref)

    ts = h_ref.shape[0]
    logit = _dot(h_ref[...], wf_ref[...]) + bf_ref[...]
    log_f = jnp.minimum(logit, 0.0) - jnp.log(1.0 + jnp.exp(-jnp.abs(logit)))
    row = lax.broadcasted_iota(jnp.int32, (ts, ts), 0)
    col = lax.broadcasted_iota(jnp.int32, (ts, ts), 1)
    lower = jnp.where(col <= row, 1.0, 0.0).astype(BF16)
    hi, lo = _split_bf16(log_f)
    cum = _dot(lower, hi) + _dot(lower, lo) + carry_ref[0:1, :]
    o_ref[...] = cum
    carry_ref[...] = jnp.broadcast_to(cum[ts - 1:ts, :], carry_ref.shape)


def _forget_cumsum(h, w_f, b_f):
    b, s, d = h.shape
    nh = w_f.shape[1]
    assert nh <= LANES
    ts = min(s, 512)
    w_pad = jnp.zeros((d, LANES), BF16).at[:, :nh].set(w_f.astype(BF16))
    b_pad = jnp.zeros((1, LANES), F32).at[0, :nh].set(b_f)
    return pl.pallas_call(
        _fgate_kernel,
        grid=(b, s // ts),
        in_specs=[pl.BlockSpec((None, ts, d), lambda i, j: (i, j, 0)),
                  pl.BlockSpec((d, LANES), lambda i, j: (0, 0)),
                  pl.BlockSpec((1, LANES), lambda i, j: (0, 0))],
        out_specs=pl.BlockSpec((None, ts, LANES), lambda i, j: (i, j, 0)),
        out_shape=jax.ShapeDtypeStruct((b, s, LANES), F32),
        scratch_shapes=[pltpu.VMEM((SUBLANES, LANES), F32)],
        compiler_params=_params("parallel", "arbitrary"),
        name="forget_cumsum",
    )(h, w_pad, b_pad)


VT_ROWS = HEAD_DIM + BF16_ROWS


def _store_v_transposed(v_ref, vt_ref, c, tile):
    vb = _kv_block(v_ref, c, tile).astype(F32)
    vt_ref[c, 0:HEAD_DIM, :] = vb.T.astype(BF16)
    vt_ref[c, HEAD_DIM:VT_ROWS, :] = jnp.ones((BF16_ROWS, tile), BF16)


def _softmax_accumulate(s, vt_blk, m_ref, acc_ref):
    m_old = m_ref[...]
    m_new = jnp.maximum(m_old, jnp.max(s, axis=0, keepdims=True))
    alpha = jnp.exp2(m_old - m_new)
    p = jnp.exp2((s - m_new).astype(BF16))
    m_ref[...] = m_new
    acc_ref[...] = alpha * acc_ref[...] + _dot(vt_blk, p)


def _softmax_finish(acc_ref, o_ref):
    acc = acc_ref[...]
    out_t = acc[0:HEAD_DIM, :] / acc[HEAD_DIM:HEAD_DIM + 1, :]
    o_ref[...] = out_t.T.astype(o_ref.dtype)


def _flash_sweep(qi, produce, s0_ref, s1_ref, vt_ref, m_ref, acc_ref, tile):
    def block_at(pos):
        return jnp.where(pos == 0, qi, pos - 1)

    m_ref[...] = jnp.full(m_ref.shape, MASKED, F32)
    acc_ref[...] = jnp.zeros_like(acc_ref)
    key = lax.broadcasted_iota(jnp.int32, (tile, tile), 0)
    qry = lax.broadcasted_iota(jnp.int32, (tile, tile), 1)
    s0_ref[...] = jnp.where(key <= qry, produce(qi), MASKED)
    n_pos = qi + 1
    last_past = jnp.maximum(qi - 1, 0)

    def pair(i, carry):
        pos = 2 * i
        s1_ref[...] = produce(pos)
        _softmax_accumulate(s0_ref[...], vt_ref[block_at(pos)], m_ref, acc_ref)
        s0_ref[...] = produce(jnp.minimum(pos + 1, last_past))
        _softmax_accumulate(s1_ref[...], vt_ref[pos], m_ref, acc_ref)
        return carry

    lax.fori_loop(0, n_pos // 2, pair, 0)

    @pl.when(lax.rem(n_pos, 2) == 1)
    def _():
        _softmax_accumulate(s0_ref[...], vt_ref[block_at(n_pos - 1)], m_ref, acc_ref)


def _fox_kernel(q_ref, k_ref, v_ref, f_ref, o_ref, vt_ref, frep_ref, s0_ref, s1_ref, m_ref, acc_ref,
                *, tile, n_kv):
    head = pl.program_id(1)
    qi = pl.program_id(2)

    @pl.when(qi == 0)
    def _():
        lane = lax.broadcasted_iota(jnp.int32, (tile, LANES), 1)

        def prep(c, carry):
            _store_v_transposed(v_ref, vt_ref, c, tile)
            f_blk = _kv_block(f_ref, c, tile)
            f_col = jnp.sum(jnp.where(lane == head, f_blk, 0.0), axis=-1, keepdims=True)
            frep_ref[pl.ds(pl.multiple_of(c * tile, tile), tile), :] = jnp.broadcast_to(
                f_col * LOG2E, (tile, LANES))
            return carry

        lax.fori_loop(0, n_kv, prep, 0)

    q = q_ref[...]

    def produce(j):
        f_rep = _kv_block(frep_ref, j, tile)
        return _dot_nt(_kv_block(k_ref, j, tile), q) - jnp.concatenate([f_rep] * (tile // LANES), axis=1)

    _flash_sweep(qi, produce, s0_ref, s1_ref, vt_ref, m_ref, acc_ref, tile)
    _softmax_finish(acc_ref, o_ref)


def _fox_attention(proj, cum_f, n_heads):
    b, s, _ = proj.shape
    tile = min(s, 512)
    n_kv = s // tile
    return pl.pallas_call(
        functools.partial(_fox_kernel, tile=tile, n_kv=n_kv),
        grid=(b, n_heads, s // tile),
        in_specs=[pl.BlockSpec((None, tile, HEAD_DIM), lambda i, h, t: (i, t, h)),
                  pl.BlockSpec((None, s, HEAD_DIM), lambda i, h, t: (i, 0, n_heads + h)),
                  pl.BlockSpec((None, s, HEAD_DIM), lambda i, h, t: (i, 0, 2 * n_heads + h)),
                  pl.BlockSpec((None, s, LANES), lambda i, h, t: (i, 0, 0))],
        out_specs=pl.BlockSpec((None, tile, HEAD_DIM), lambda i, h, t: (i, t, h)),
        out_shape=jax.ShapeDtypeStruct((b, s, n_heads * HEAD_DIM), BF16),
        scratch_shapes=[pltpu.VMEM((n_kv, VT_ROWS, tile), BF16), pltpu.VMEM((s, LANES), F32),
                        pltpu.VMEM((tile, tile), F32), pltpu.VMEM((tile, tile), F32),
                        pltpu.VMEM((1, tile), F32),
                        pltpu.VMEM((VT_ROWS, tile), F32)],
        compiler_params=_params("parallel", "arbitrary", "arbitrary"),
        name="fox_attention",
    )(proj, proj, proj, cum_f)


def _moba_kernel(q_ref, k_ref, v_ref, slope_ref, o_ref, kmean_ref, vt_ref, sel_ref, s0_ref, s1_ref, m_ref,
                 acc_ref, *, tile, n_blocks):
    blk = MOBA_BLOCK
    per_tile = tile // blk
    qi = pl.program_id(2)

    @pl.when(qi == 0)
    def _():
        kmean_ref[...] = jnp.zeros_like(kmean_ref)

        def block_mean(n, carry):
            kb = _kv_block(k_ref, n, blk).astype(F32)
            kmean_ref[pl.ds(n, 1), :] = jnp.mean(kb, axis=0, keepdims=True)
            return carry

        def transpose_v(c, carry):
            _store_v_transposed(v_ref, vt_ref, c, tile)
            return carry

        lax.fori_loop(0, n_blocks, block_mean, 0)
        lax.fori_loop(0, n_blocks // per_tile, transpose_v, 0)

    q = q_ref[...]
    block_id = lax.broadcasted_iota(jnp.int32, (LANES, tile), 0).astype(F32)
    qry = lax.broadcasted_iota(jnp.int32, (LANES, tile), 1)
    own = (qi * per_tile).astype(F32) + jnp.floor(qry.astype(F32) * (1.0 / blk))
    gate = _dot_nt(kmean_ref[...].astype(BF16), q)
    gate = jnp.where(block_id < own, gate, -jnp.inf)
    sel = jnp.where(block_id == own, 1.0, 0.0)
    for slot in range(MOBA_TOPK):
        best = jnp.max(gate, axis=0, keepdims=True)
        idx = jnp.min(jnp.where(gate == best, block_id, float(LANES)), axis=0, keepdims=True)
        hit = block_id == idx
        slot_ok = jnp.where(float(slot) < own, 1.0, 0.0)
        sel = jnp.maximum(sel, jnp.where(hit, slot_ok, 0.0))
        gate = jnp.where(hit, -jnp.inf, gate)
    sel_ref[...] = sel

    slope = slope_ref[...] * LOG2E
    key_in_tile = lax.broadcasted_iota(jnp.int32, (tile, tile), 0).astype(F32)
    alibi = slope * key_in_tile

    def produce(j):
        shift = slope * ((j - qi) * tile).astype(F32)
        s = _dot_nt(_kv_block(k_ref, j, tile), q) + (alibi + shift)
        picked = jnp.concatenate(
            [jnp.broadcast_to(sel_ref[pl.ds(j * per_tile + r, 1), :], (blk, tile)) for r in range(per_tile)],
            axis=0)
        return jnp.where(picked > 0.5, s, MASKED)

    _flash_sweep(qi, produce, s0_ref, s1_ref, vt_ref, m_ref, acc_ref, tile)
    _softmax_finish(acc_ref, o_ref)


def _moba_attention(proj, n_heads, col_block):
    b, s, _ = proj.shape
    blk = MOBA_BLOCK
    n_blocks = s // blk
    tile = min(s, 2 * blk)
    assert s % tile == 0 and tile % blk == 0 and n_blocks <= LANES
    slopes = jnp.asarray([2.0 ** (-8.0 * (i + 1) / n_heads) for i in range(n_heads)], F32)
    slope_rows = jnp.broadcast_to(slopes[:, None, None], (n_heads, 1, tile))
    return pl.pallas_call(
        functools.partial(_moba_kernel, tile=tile, n_blocks=n_blocks),
        grid=(b, n_heads, s // tile),
        in_specs=[pl.BlockSpec((None, tile, HEAD_DIM), lambda i, h, t: (i, t, col_block + h)),
                  pl.BlockSpec((None, s, HEAD_DIM), lambda i, h, t: (i, 0, col_block + n_heads + h)),
                  pl.BlockSpec((None, s, HEAD_DIM), lambda i, h, t: (i, 0, col_block + 2 * n_heads + h)),
                  pl.BlockSpec((None, 1, tile), lambda i, h, t: (h, 0, 0))],
        out_specs=pl.BlockSpec((None, tile, HEAD_DIM), lambda i, h, t: (i, t, h)),
        out_shape=jax.ShapeDtypeStruct((b, s, n_heads * HEAD_DIM), BF16),
        scratch_shapes=[pltpu.VMEM((LANES, HEAD_DIM), F32), pltpu.VMEM((s // tile, VT_ROWS, tile), BF16),
                        pltpu.VMEM((LANES, tile), F32), pltpu.VMEM((tile, tile), F32),
                        pltpu.VMEM((tile, tile), F32), pltpu.VMEM((1, tile), F32),
                        pltpu.VMEM((VT_ROWS, tile), F32)],
        compiler_params=_params("parallel", "arbitrary", "arbitrary"),
        name="moba_attention",
    )(proj, proj, proj, slope_rows)


def _sb_kernel(q_ref, k_ref, v_ref, o_ref, decay_ref, acc_ref, *, tile):
    qi = pl.program_id(2)
    q = q_ref[...]
    row = lax.broadcasted_iota(jnp.int32, (tile, tile), 0)
    col = lax.broadcasted_iota(jnp.int32, (tile, tile), 1)
    after = jnp.where(row > col, 1.0, 0.0).astype(BF16)

    def block(j, causal):
        z = _dot_nt(q, _kv_block(k_ref, j, tile))
        drop = jnp.maximum(z, 0.0) + jnp.log2(1.0 + jnp.exp2(-jnp.abs(z)))
        if causal is not None:
            drop = jnp.where(causal, drop, 0.0)
        hi, lo = _split_bf16(drop)
        decay = _dot(hi, after) + _dot(lo, after) + decay_ref[...]
        a = jnp.exp2(z - drop - decay)
        if causal is not None:
            a = jnp.where(causal, a, 0.0)
        acc_ref[...] += _dot(a.astype(BF16), _kv_block(v_ref, j, tile))
        decay_ref[...] += jnp.sum(drop, axis=-1, keepdims=True)

    decay_ref[...] = jnp.zeros_like(decay_ref)
    acc_ref[...] = jnp.zeros_like(acc_ref)
    block(qi, col < row)

    def cond(carry):
        jj, least = carry
        return jnp.logical_and(jj < qi, least < SB_EXIT_BITS)

    def body(carry):
        jj, _ = carry
        block(qi - 1 - jj, None)
        return jj + 1, jnp.min(decay_ref[...])

    lax.while_loop(cond, body, (jnp.int32(0), jnp.min(decay_ref[...])))
    o_ref[...] = acc_ref[...].astype(o_ref.dtype)


def _sb_attention(proj, n_heads, col_block):
    b, s, _ = proj.shape
    tile = min(s, 256)
    return pl.pallas_call(
        functools.partial(_sb_kernel, tile=tile),
        grid=(b, n_heads, s // tile),
        in_specs=[pl.BlockSpec((None, tile, HEAD_DIM), lambda i, h, t: (i, t, col_block + h)),
                  pl.BlockSpec((None, s, HEAD_DIM), lambda i, h, t: (i, 0, col_block + n_heads + h)),
                  pl.BlockSpec((None, s, HEAD_DIM), lambda i, h, t: (i, 0, col_block + 2 * n_heads + h))],
        out_specs=pl.BlockSpec((None, tile, HEAD_DIM), lambda i, h, t: (i, t, h)),
        out_shape=jax.ShapeDtypeStruct((b, s, n_heads * HEAD_DIM), BF16),
        scratch_shapes=[pltpu.VMEM((tile, 1), F32), pltpu.VMEM((tile, HEAD_DIM), F32)],
        compiler_params=_params("parallel", "parallel", "arbitrary"),
        name="sb_attention",
    )(proj, proj, proj)


def _outproj_kernel(*refs, n_parts):
    x_ref, gate_ref = refs[0], refs[1]
    parts = refs[2:2 + 2 * n_parts]
    o_ref = refs[2 + 2 * n_parts]
    y = _dot(parts[0][...], parts[1][...])
    for p in range(1, n_parts):
        y += _dot(parts[2 * p][...], parts[2 * p + 1][...])
    o_ref[...] = x_ref[...] + gate_ref[...] * y


def _out_proj_residual(x, gate, parts):
    b, s, d = x.shape
    ts = min(s, 512)
    in_specs = [pl.BlockSpec((None, ts, d), lambda i, j: (i, j, 0)),
                pl.BlockSpec((None, 1, d), lambda i, j: (i, 0, 0))]
    args = [x, gate.reshape(b, 1, d)]
    for a, w in parts:
        kp = a.shape[-1]
        in_specs += [pl.BlockSpec((None, ts, kp), lambda i, j: (i, j, 0)),
                     pl.BlockSpec((kp, d), lambda i, j: (0, 0))]
        args += [a, w]
    return pl.pallas_call(
        functools.partial(_outproj_kernel, n_parts=len(parts)),
        grid=(b, s // ts),
        in_specs=in_specs,
        out_specs=pl.BlockSpec((None, ts, d), lambda i, j: (i, j, 0)),
        out_shape=jax.ShapeDtypeStruct((b, s, d), F32),
        compiler_params=_params("parallel", "parallel"),
        name="out_proj",
    )(*args)


def _router_kernel(h_ref, w_ref, b_ref, idx_ref, wgt_ref, rank_ref, cnt_ref, carry_ref):
    @pl.when(pl.program_id(0) == 0)
    def _():
        carry_ref[...] = jnp.zeros_like(carry_ref)

    tm = h_ref.shape[0]
    logits = _dot(h_ref[...], w_ref[...]) + b_ref[...]
    lane = lax.broadcasted_iota(jnp.int32, (tm, LANES), 1).astype(F32)
    vals, idxs = [], []
    onehot = jnp.zeros((tm, LANES), F32)
    for _ in range(TOP_K):
        best = jnp.max(logits, axis=-1, keepdims=True)
        idx = jnp.min(jnp.where(logits == best, lane, float(LANES)), axis=-1, keepdims=True)
        hit = lane == idx
        onehot = jnp.where(hit, 1.0, onehot)
        logits = jnp.where(hit, -jnp.inf, logits)
        vals.append(best)
        idxs.append(idx)
    exps = [jnp.exp(v - vals[0]) for v in vals]
    denom = exps[0]
    for e in exps[1:]:
        denom = denom + e

    row = lax.broadcasted_iota(jnp.int32, (tm, tm), 0)
    col = lax.broadcasted_iota(jnp.int32, (tm, tm), 1)
    before = jnp.where(col < row, 1.0, 0.0).astype(BF16)
    prior = _dot(before, onehot.astype(BF16)) + carry_ref[0:1, :]

    idx_out = jnp.zeros((tm, LANES), F32)
    wgt_out = jnp.zeros((tm, LANES), F32)
    rank_out = jnp.zeros((tm, LANES), F32)
    for k in range(TOP_K):
        rank_k = jnp.sum(jnp.where(lane == idxs[k], prior, 0.0), axis=-1, keepdims=True)
        slot = lane == float(k)
        idx_out = jnp.where(slot, idxs[k], idx_out)
        wgt_out = jnp.where(slot, exps[k] / denom, wgt_out)
        rank_out = jnp.where(slot, rank_k, rank_out)
    idx_ref[...] = idx_out.astype(jnp.int32)
    wgt_ref[...] = wgt_out
    rank_ref[...] = rank_out.astype(jnp.int32)
    counts = carry_ref[...] + jnp.sum(onehot, axis=0, keepdims=True)
    carry_ref[...] = counts
    cnt_ref[...] = counts.astype(jnp.int32)


def _route(h, w_router, b_router):
    t, d = h.shape
    e = w_router.shape[1]
    assert e <= LANES
    tm = min(t, 512)
    w_pad = jnp.zeros((d, LANES), BF16).at[:, :e].set(w_router.astype(BF16))
    b_pad = jnp.full((1, LANES), MASKED, F32).at[0, :e].set(b_router)
    tok_spec = pl.BlockSpec((tm, LANES), lambda i: (i, 0))
    return pl.pallas_call(
        _router_kernel,
        grid=(t // tm,),
        in_specs=[pl.BlockSpec((tm, d), lambda i: (i, 0)),
                  pl.BlockSpec((d, LANES), lambda i: (0, 0)),
                  pl.BlockSpec((1, LANES), lambda i: (0, 0))],
        out_specs=[tok_spec, tok_spec, tok_spec, pl.BlockSpec((SUBLANES, LANES), lambda i: (0, 0))],
        out_shape=[jax.ShapeDtypeStruct((t, LANES), jnp.int32), jax.ShapeDtypeStruct((t, LANES), F32),
                   jax.ShapeDtypeStruct((t, LANES), jnp.int32),
                   jax.ShapeDtypeStruct((SUBLANES, LANES), jnp.int32)],
        scratch_shapes=[pltpu.VMEM((SUBLANES, LANES), F32)],
        compiler_params=_params("arbitrary"),
        name="moe_router",
    )(h, w_pad, b_pad)


def _moe_up_kernel(te_ref, first_ref, valid_ref, x_ref, wg_ref, bg_ref, wu_ref, bu_ref, o_ref,
                   wg_bf, wu_bf):
    i = pl.program_id(0)

    @pl.when(first_ref[i] == 1)
    def _():
        wg_bf[...] = wg_ref[...].astype(BF16)
        wu_bf[...] = wu_ref[...].astype(BF16)

    @pl.when(valid_ref[i] == 1)
    def _():
        x = x_ref[...]
        g = jnp.minimum(_dot(x, wg_bf[...]) + bg_ref[...], SWIGLU_LIMIT)
        u = jnp.clip(_dot(x, wu_bf[...]) + bu_ref[...], -SWIGLU_LIMIT, SWIGLU_LIMIT)
        act = (u + 1.0) * (g / (1.0 + jnp.exp(-SWIGLU_ALPHA * g)))
        o_ref[...] = act.astype(o_ref.dtype)

    @pl.when(valid_ref[i] == 0)
    def _():
        o_ref[...] = jnp.zeros_like(o_ref)


def _moe_down_kernel(te_ref, first_ref, valid_ref, a_ref, wd_ref, bd_ref, o_ref, wd_bf):
    i = pl.program_id(0)

    @pl.when(first_ref[i] == 1)
    def _():
        wd_bf[...] = wd_ref[...].astype(BF16)

    @pl.when(valid_ref[i] == 1)
    def _():
        o_ref[...] = (_dot(a_ref[...], wd_bf[...]) + bd_ref[...]).astype(o_ref.dtype)

    @pl.when(valid_ref[i] == 0)
    def _():
        o_ref[...] = jnp.zeros_like(o_ref)


def _expert_spec(layer, shape):
    return pl.BlockSpec((None, None) + shape, lambda i, te, first, valid: (layer, te[i], 0, 0))


def _moe_experts(xs, tile_expert, tile_first, tile_valid, layer, w_gate, b_gate, w_up, b_up, w_down, b_down,
                 tm):
    p, d = xs.shape
    depth, e, _, f = w_gate.shape
    n_tiles = p // tm
    row_spec = lambda width: pl.BlockSpec((tm, width), lambda i, te, first, valid: (i, 0))
    act = pl.pallas_call(
        _moe_up_kernel,
        grid_spec=pltpu.PrefetchScalarGridSpec(
            num_scalar_prefetch=3, grid=(n_tiles,),
            in_specs=[row_spec(d), _expert_spec(layer, (d, f)), _expert_spec(layer, (1, f)),
                      _expert_spec(layer, (d, f)), _expert_spec(layer, (1, f))],
            out_specs=row_spec(f),
            scratch_shapes=[pltpu.VMEM((d, f), BF16), pltpu.VMEM((d, f), BF16)]),
        out_shape=jax.ShapeDtypeStruct((p, f), BF16),
        compiler_params=_params("arbitrary"),
        name="moe_up",
    )(tile_expert, tile_first, tile_valid, xs, w_gate, b_gate.reshape(depth, e, 1, f),
      w_up, b_up.reshape(depth, e, 1, f))
    return pl.pallas_call(
        _moe_down_kernel,
        grid_spec=pltpu.PrefetchScalarGridSpec(
            num_scalar_prefetch=3, grid=(n_tiles,),
            in_specs=[row_spec(f), _expert_spec(layer, (f, d)), _expert_spec(layer, (1, d))],
            out_specs=row_spec(d),
            scratch_shapes=[pltpu.VMEM((f, d), BF16)]),
        out_shape=jax.ShapeDtypeStruct((p, d), BF16),
        compiler_params=_params("arbitrary"),
        name="moe_down",
    )(tile_expert, tile_first, tile_valid, act, w_down, b_down.reshape(depth, e, 1, d))


def _combine_kernel(x_ref, gate_ref, y_ref, w_ref, o_ref):
    w = w_ref[...]
    y = w[:, 0:1] * y_ref[0].astype(F32)
    for k in range(1, TOP_K):
        y += w[:, k:k + 1] * y_ref[k].astype(F32)
    o_ref[...] = x_ref[...] + gate_ref[...] * y


def _moe_combine(x, gate, y_sel, weights):
    b, s, d = x.shape
    ts = min(s, 256)
    per_b = s // ts
    return pl.pallas_call(
        _combine_kernel,
        grid=(b, per_b),
        in_specs=[pl.BlockSpec((None, ts, d), lambda i, j: (i, j, 0)),
                  pl.BlockSpec((None, 1, d), lambda i, j: (i, 0, 0)),
                  pl.BlockSpec((TOP_K, ts, d), lambda i, j: (0, i * per_b + j, 0)),
                  pl.BlockSpec((ts, LANES), lambda i, j: (i * per_b + j, 0))],
        out_specs=pl.BlockSpec((None, ts, d), lambda i, j: (i, j, 0)),
        out_shape=jax.ShapeDtypeStruct((b, s, d), F32),
        compiler_params=_params("parallel", "parallel"),
        name="moe_combine",
    )(x, gate.reshape(b, 1, d), y_sel, weights)


def _moe_ffn(x, h, gate, layer, w_router, b_router, w_gate, b_gate, w_up, b_up, w_down, b_down):
    b, s, d = x.shape
    t = b * s
    e = w_router.shape[1]
    tm = 256
    h2 = h.reshape(t, d)
    idx_pad, wgt_pad, rank_pad, counts_pad = _route(h2, w_router, b_router)
    idx = idx_pad[:, :TOP_K]
    counts = counts_pad[0, :e]

    padded = ((counts + tm - 1) // tm) * tm
    ends = jnp.cumsum(padded)
    starts = ends - padded
    pos = starts[idx] + rank_pad[:, :TOP_K]
    n_slots = t * TOP_K + e * tm
    n_tiles = n_slots // tm
    tile_start = jnp.arange(n_tiles, dtype=jnp.int32) * tm
    tile_expert = jnp.sum((tile_start[:, None] >= ends[None, :]).astype(jnp.int32), axis=1)
    tile_expert = jnp.minimum(tile_expert, e - 1)
    tile_valid = (tile_start < ends[-1]).astype(jnp.int32)
    tile_first = jnp.concatenate([jnp.ones((1,), jnp.int32),
                                  (tile_expert[1:] != tile_expert[:-1]).astype(jnp.int32)])
    token_ids = jnp.repeat(jnp.arange(t, dtype=jnp.int32), TOP_K)
    slot_token = jnp.zeros((n_slots,), jnp.int32).at[pos.reshape(-1)].set(token_ids, unique_indices=True)

    xs = jnp.take(h2, slot_token, axis=0, mode="clip")
    ys = _moe_experts(xs, tile_expert, tile_first, tile_valid, layer,
                      w_gate, b_gate, w_up, b_up, w_down, b_down, tm)
    y_sel = jnp.take(ys, pos.T, axis=0, mode="clip")
    return _moe_combine(x, gate, y_sel, wgt_pad)


def kernel(x, c, mod_w, mod_b, mix_norm_g, ffn_norm_g, ab_w_in, ab_w_out, moba_q_gain, moba_k_gain,
           fox_w_in, fox_b_f, fox_w_out, fox_q_gain, fox_k_gain, router_w, router_b,
           exp_w_gate, exp_b_gate, exp_w_up, exp_b_up, exp_w_down, exp_b_down):
    b, s, d = x.shape
    depth = mod_w.shape[0]
    n_heads = d // HEAD_DIM
    n_moba = n_heads // 2
    n_sb = n_heads - n_moba
    mod = _adaln_mod(c, mod_w, mod_b)

    for layer in range(depth):
        sh1, sc1, g1, sh2, sc2, g2 = [mod[layer, :, i * d:(i + 1) * d] for i in range(N_MOD)]
        j = layer // 2
        h = _norm_mod(x, mix_norm_g[layer], sc1, sh1)
        if layer % 2 == 0:
            wa = n_moba * HEAD_DIM
            wb = n_sb * HEAD_DIM
            col_scale = jnp.concatenate([
                jnp.tile(moba_q_gain[j] * Q_PRESCALE, n_moba), jnp.tile(moba_k_gain[j], n_moba),
                jnp.ones((wa,), F32), jnp.full((wb,), Q_PRESCALE, F32), jnp.ones((2 * wb,), F32)])
            n_cols = col_scale.shape[0]
            proj = _in_proj(h.reshape(b * s, d), ab_w_in[j].astype(BF16), col_scale.reshape(1, n_cols), 2 * wa)
            proj = proj.reshape(b, s, n_cols)
            o_a = _moba_attention(proj, n_moba, 0)
            o_b = _sb_attention(proj, n_sb, 3 * n_moba)
            w_out = ab_w_out[j].astype(BF16)
            x = _out_proj_residual(x, g1, [(o_a, w_out[:wa]), (o_b, w_out[wa:])])
        else:
            w = n_heads * HEAD_DIM
            col_scale = jnp.concatenate([jnp.tile(fox_q_gain[j] * Q_PRESCALE, n_heads),
                                         jnp.tile(fox_k_gain[j], n_heads), jnp.ones((w,), F32)])
            proj = _in_proj(h.reshape(b * s, d), fox_w_in[j, :, :3 * w].astype(BF16),
                            col_scale.reshape(1, 3 * w), 2 * w)
            cum_f = _forget_cumsum(h, fox_w_in[j, :, 3 * w:], fox_b_f[j])
            o = _fox_attention(proj.reshape(b, s, 3 * w), cum_f, n_heads)
            x = _out_proj_residual(x, g1, [(o, fox_w_out[j].astype(BF16))])
        h = _norm_mod(x, ffn_norm_g[layer], sc2, sh2)
        x = _moe_ffn(x, h, g2, layer, router_w[layer], router_b[layer], exp_w_gate, exp_b_gate,
                     exp_w_up, exp_b_up, exp_w_down, exp_b_down)
    return x
```

```python
import functools
import math

import jax
import jax.numpy as jnp
from jax import lax
from jax.experimental import pallas as pl
from jax.experimental.pallas import tpu as pltpu

F32 = jnp.float32
BF16 = jnp.bfloat16

HEAD_DIM = 128
MOBA_BLOCK = 256
MOBA_TOPK = 3
TOP_K = 4
SWIGLU_LIMIT = 7.0
SWIGLU_ALPHA = 1.702
RMS_EPS = 1e-5
N_MOD = 6
LOG2E = math.log2(math.e)
Q_PRESCALE = HEAD_DIM ** -0.5 * LOG2E
LANES = 128
SUBLANES = 8
BF16_ROWS = 16
MASKED = -1e30
SB_EXIT_BITS = 160.0
ROW_DMA_UNROLL = 16
VMEM_LIMIT_BYTES = 56 * 1024 * 1024


def _params(*semantics):
    return pltpu.CompilerParams(dimension_semantics=semantics, vmem_limit_bytes=VMEM_LIMIT_BYTES)


def _dot(a, b):
    return jnp.dot(a, b, preferred_element_type=F32)


def _dot_nt(a, b):
    return lax.dot_general(a, b, (((1,), (1,)), ((), ())), preferred_element_type=F32)


def _split_bf16(x):
    hi = x.astype(BF16)
    lo = (x - hi.astype(F32)).astype(BF16)
    return hi, lo


def _kv_block(ref, j, tk):
    return ref[pl.ds(pl.multiple_of(j * tk, tk), tk), :]


def _pack_bf16_pair(lo, hi):
    lo_bits = lax.bitcast_convert_type(lo.astype(BF16).astype(F32), jnp.uint32)
    hi_bits = lax.bitcast_convert_type(hi.astype(BF16).astype(F32), jnp.uint32)
    return lax.shift_right_logical(lo_bits, jnp.uint32(16)) | (hi_bits & jnp.uint32(0xFFFF0000))


def _unpack_bf16_pair(word):
    lo = lax.bitcast_convert_type(lax.shift_left(word, jnp.uint32(16)), F32)
    hi = lax.bitcast_convert_type(word & jnp.uint32(0xFFFF0000), F32)
    return lo, hi


TOKEN_DIM = 2 * SUBLANES * LANES


def _store_token_tiles(ref, y, lead=()):
    n = y.shape[0]
    half = TOKEN_DIM // 2
    for s in range(SUBLANES):
        lo = y[:, s * LANES:(s + 1) * LANES]
        hi = y[:, half + s * LANES:half + (s + 1) * LANES]
        ref[lead + (pl.ds(s, n, stride=SUBLANES), slice(None))] = _pack_bf16_pair(lo, hi)


def _load_token_tiles(ref, n, lead=()):
    los, his = [], []
    for s in range(SUBLANES):
        lo, hi = _unpack_bf16_pair(ref[lead + (pl.ds(s, n, stride=SUBLANES), slice(None))])
        los.append(lo)
        his.append(hi)
    return los + his


def _load_token_tiles_bf16(ref, n, lead=()):
    return jnp.concatenate([c.astype(BF16) for c in _load_token_tiles(ref, n, lead)], axis=1)


def _mod_kernel(c_ref, w_ref, b_ref, o_ref):
    c = c_ref[...]
    c_act = c / (1.0 + jnp.exp(-c))
    o_ref[...] = _dot(c_act, w_ref[...]) + b_ref[...]


def _adaln_mod(c, mod_w, mod_b):
    depth, d, n = mod_w.shape
    b = c.shape[0]
    bp = -(-b // SUBLANES) * SUBLANES
    c_pad = jnp.zeros((bp, d), F32).at[:b].set(c)
    tn = min(n, 1024)
    out = pl.pallas_call(
        _mod_kernel,
        grid=(depth, n // tn),
        in_specs=[pl.BlockSpec((bp, d), lambda l, j: (0, 0)),
                  pl.BlockSpec((None, d, tn), lambda l, j: (l, 0, j)),
                  pl.BlockSpec((None, 1, tn), lambda l, j: (l, 0, j))],
        out_specs=pl.BlockSpec((None, bp, tn), lambda l, j: (l, 0, j)),
        out_shape=jax.ShapeDtypeStruct((depth, bp, n), F32),
        compiler_params=_params("arbitrary", "arbitrary"),
        name="adaln_mod",
    )(c_pad, mod_w, mod_b.reshape(depth, 1, n))
    return out[:, :b]


def _norm_kernel(x_ref, g_ref, sc_ref, sh_ref, o_ref, *, packed):
    x = x_ref[...]
    ms = jnp.mean(x * x, axis=-1, keepdims=True)
    y = x * lax.rsqrt(ms + RMS_EPS)
    y = (y * g_ref[...]) * (1.0 + sc_ref[...]) + sh_ref[...]
    if packed:
        _store_token_tiles(o_ref, y)
    else:
        o_ref[...] = y.astype(o_ref.dtype)


def _norm_mod(x, gain, scale, shift, packed=False):
    b, s, d = x.shape
    ts = min(s, 512)
    if packed:
        assert d == TOKEN_DIM
        out_block, out_shape = (None, ts * SUBLANES, LANES), (b, s * SUBLANES, LANES)
    else:
        out_block, out_shape = (None, ts, d), (b, s, d)
    return pl.pallas_call(
        functools.partial(_norm_kernel, packed=packed),
        grid=(b, s // ts),
        in_specs=[pl.BlockSpec((None, ts, d), lambda i, j: (i, j, 0)),
                  pl.BlockSpec((1, d), lambda i, j: (0, 0)),
                  pl.BlockSpec((None, 1, d), lambda i, j: (i, 0, 0)),
                  pl.BlockSpec((None, 1, d), lambda i, j: (i, 0, 0))],
        out_specs=pl.BlockSpec(out_block, lambda i, j: (i, j, 0)),
        out_shape=jax.ShapeDtypeStruct(out_shape, jnp.uint32 if packed else BF16),
        compiler_params=_params("arbitrary", "arbitrary"),
        name="norm_mod",
    )(x, gain.reshape(1, d), scale.reshape(b, 1, d), shift.reshape(b, 1, d))


def _inproj_kernel(h_ref, w_ref, gain_ref, o_ref, *, n_norm_tiles):
    acc = _dot(h_ref[...], w_ref[...])
    j = pl.program_id(0)

    @pl.when(j < n_norm_tiles)
    def _():
        for g in range(acc.shape[1] // HEAD_DIM):
            sl = slice(g * HEAD_DIM, (g + 1) * HEAD_DIM)
            blk = acc[:, sl]
            ms = jnp.mean(blk * blk, axis=-1, keepdims=True)
            o_ref[:, sl] = (blk * lax.rsqrt(ms + RMS_EPS) * gain_ref[:, sl]).astype(o_ref.dtype)

    @pl.when(j >= n_norm_tiles)
    def _():
        o_ref[...] = (acc * gain_ref[...]).astype(o_ref.dtype)


def _in_proj(h, w, col_scale, n_norm_cols):
    t, d = h.shape
    n = w.shape[1]
    tm = min(t, 1024)
    tn = min(n, 1024)
    assert n_norm_cols % tn == 0 and n % tn == 0 and t % tm == 0
    return pl.pallas_call(
        functools.partial(_inproj_kernel, n_norm_tiles=n_norm_cols // tn),
        grid=(n // tn, t // tm),
        in_specs=[pl.BlockSpec((tm, d), lambda j, i: (i, 0)),
                  pl.BlockSpec((d, tn), lambda j, i: (0, j)),
                  pl.BlockSpec((1, tn), lambda j, i: (0, j))],
        out_specs=pl.BlockSpec((tm, tn), lambda j, i: (i, j)),
        out_shape=jax.ShapeDtypeStruct((t, n), BF16),
        compiler_params=_params("arbitrary", "arbitrary"),
        name="in_proj",
    )(h, w, col_scale)


def _fgate_kernel(h_ref, wf_ref, bf_ref, o_ref, carry_ref):
    @pl.when(pl.program_id(1) == 0)
    def _():
        carry_ref[...] = jnp.zeros_like(carry_ref)

    ts = h_ref.shape[0]
    logit = _dot(h_ref[...], wf_ref[...]) + bf_ref[...]
    log_f = jnp.minimum(logit, 0.0) - jnp.log(1.0 + jnp.exp(-jnp.abs(logit)))
    row = lax.broadcasted_iota(jnp.int32, (ts, ts), 0)
    col = lax.broadcasted_iota(jnp.int32, (ts, ts), 1)
    lower = jnp.where(col <= row, 1.0, 0.0).astype(BF16)
    hi, lo = _split_bf16(log_f)
    cum = _dot(lower, hi) + _dot(lower, lo) + carry_ref[0:1, :]
    o_ref[...] = cum
    carry_ref[...] = jnp.broadcast_to(cum[ts - 1:ts, :], carry_ref.shape)


def _forget_cumsum(h, w_f, b_f):
    b, s, d = h.shape
    nh = w_f.shape[1]
    assert nh <= LANES
    ts = min(s, 512)
    w_pad = jnp.zeros((d, LANES), BF16).at[:, :nh].set(w_f.astype(BF16))
    b_pad = jnp.zeros((1, LANES), F32).at[0, :nh].set(b_f)
    return pl.pallas_call(
        _fgate_kernel,
        grid=(b, s // ts),
        in_specs=[pl.BlockSpec((None, ts, d), lambda i, j: (i, j, 0)),
                  pl.BlockSpec((d, LANES), lambda i, j: (0, 0)),
                  pl.BlockSpec((1, LANES), lambda i, j: (0, 0))],
        out_specs=pl.BlockSpec((None, ts, LANES), lambda i, j: (i, j, 0)),
        out_shape=jax.ShapeDtypeStruct((b, s, LANES), F32),
        scratch_shapes=[pltpu.VMEM((SUBLANES, LANES), F32)],
        compiler_params=_params("arbitrary", "arbitrary"),
        name="forget_cumsum",
    )(h, w_pad, b_pad)


VT_ROWS = HEAD_DIM + BF16_ROWS


def _store_v_transposed(v_ref, vt_ref, c, tile):
    vb = _kv_block(v_ref, c, tile).astype(F32)
    vt_ref[c, 0:HEAD_DIM, :] = vb.T.astype(BF16)
    vt_ref[c, HEAD_DIM:VT_ROWS, :] = jnp.ones((BF16_ROWS, tile), BF16)


def _softmax_accumulate(s, vt_blk, m_ref, acc_ref):
    m_old = m_ref[...]
    m_new = jnp.maximum(m_old, jnp.max(s, axis=0, keepdims=True))
    alpha = jnp.exp2(m_old - m_new)
    p = jnp.exp2((s - m_new).astype(BF16))
    m_ref[...] = m_new
    acc_ref[...] = alpha * acc_ref[...] + _dot(vt_blk, p)


def _softmax_finish(acc_ref, o_ref):
    acc = acc_ref[...]
    out_t = acc[0:HEAD_DIM, :] / acc[HEAD_DIM:HEAD_DIM + 1, :]
    o_ref[...] = out_t.T.astype(o_ref.dtype)


def _flash_sweep(qi, produce, s0_ref, s1_ref, vt_ref, m_ref, acc_ref, tile):
    def block_at(pos):
        return jnp.where(pos == 0, qi, pos - 1)

    m_ref[...] = jnp.full(m_ref.shape, MASKED, F32)
    acc_ref[...] = jnp.zeros_like(acc_ref)
    key = lax.broadcasted_iota(jnp.int32, (tile, tile), 0)
    qry = lax.broadcasted_iota(jnp.int32, (tile, tile), 1)
    s0_ref[...] = jnp.where(key <= qry, produce(qi), MASKED)
    n_pos = qi + 1
    last_past = jnp.maximum(qi - 1, 0)

    def pair(i, carry):
        pos = 2 * i
        s1_ref[...] = produce(pos)
        _softmax_accumulate(s0_ref[...], vt_ref[block_at(pos)], m_ref, acc_ref)
        s0_ref[...] = produce(jnp.minimum(pos + 1, last_past))
        _softmax_accumulate(s1_ref[...], vt_ref[pos], m_ref, acc_ref)
        return carry

    lax.fori_loop(0, n_pos // 2, pair, 0)

    @pl.when(lax.rem(n_pos, 2) == 1)
    def _():
        _softmax_accumulate(s0_ref[...], vt_ref[block_at(n_pos - 1)], m_ref, acc_ref)


def _fox_kernel(q_ref, k_ref, v_ref, f_ref, o_ref, vt_ref, frep_ref, s0_ref, s1_ref, m_ref, acc_ref,
                *, tile, n_kv):
    head = pl.program_id(1)
    qi = pl.program_id(2)

    @pl.when(qi == 0)
    def _():
        lane = lax.broadcasted_iota(jnp.int32, (tile, LANES), 1)

        def prep(c, carry):
            _store_v_transposed(v_ref, vt_ref, c, tile)
            f_blk = _kv_block(f_ref, c, tile)
            f_col = jnp.sum(jnp.where(lane == head, f_blk, 0.0), axis=-1, keepdims=True)
            frep_ref[pl.ds(pl.multiple_of(c * tile, tile), tile), :] = jnp.broadcast_to(
                f_col * LOG2E, (tile, LANES))
            return carry

        lax.fori_loop(0, n_kv, prep, 0)

    q = q_ref[...]

    def produce(j):
        f_rep = _kv_block(frep_ref, j, tile)
        return _dot_nt(_kv_block(k_ref, j, tile), q) - jnp.concatenate([f_rep] * (tile // LANES), axis=1)

    _flash_sweep(qi, produce, s0_ref, s1_ref, vt_ref, m_ref, acc_ref, tile)
    _softmax_finish(acc_ref, o_ref)


def _fox_attention(proj, cum_f, n_heads):
    b, s, _ = proj.shape
    tile = min(s, 512)
    n_kv = s // tile
    return pl.pallas_call(
        functools.partial(_fox_kernel, tile=tile, n_kv=n_kv),
        grid=(b, n_heads, s // tile),
        in_specs=[pl.BlockSpec((None, tile, HEAD_DIM), lambda i, h, t: (i, t, h)),
                  pl.BlockSpec((None, s, HEAD_DIM), lambda i, h, t: (i, 0, n_heads + h)),
                  pl.BlockSpec((None, s, HEAD_DIM), lambda i, h, t: (i, 0, 2 * n_heads + h)),
                  pl.BlockSpec((None, s, LANES), lambda i, h, t: (i, 0, 0))],
        out_specs=pl.BlockSpec((None, tile, HEAD_DIM), lambda i, h, t: (i, t, h)),
        out_shape=jax.ShapeDtypeStruct((b, s, n_heads * HEAD_DIM), BF16),
        scratch_shapes=[pltpu.VMEM((n_kv, VT_ROWS, tile), BF16), pltpu.VMEM((s, LANES), F32),
                        pltpu.VMEM((tile, tile), F32), pltpu.VMEM((tile, tile), F32),
                        pltpu.VMEM((1, tile), F32),
                        pltpu.VMEM((VT_ROWS, tile), F32)],
        compiler_params=_params("arbitrary", "arbitrary", "arbitrary"),
        name="fox_attention",
    )(proj, proj, proj, cum_f)


def _moba_kernel(q_ref, k_ref, v_ref, slope_ref, o_ref, kmean_ref, vt_ref, sel_ref, s0_ref, s1_ref, m_ref,
                 acc_ref, *, tile, n_blocks):
    blk = MOBA_BLOCK
    per_tile = tile // blk
    qi = pl.program_id(2)

    @pl.when(qi == 0)
    def _():
        kmean_ref[...] = jnp.zeros_like(kmean_ref)

        def block_mean(n, carry):
            kb = _kv_block(k_ref, n, blk).astype(F32)
            kmean_ref[pl.ds(n, 1), :] = jnp.mean(kb, axis=0, keepdims=True)
            return carry

        def transpose_v(c, carry):
            _store_v_transposed(v_ref, vt_ref, c, tile)
            return carry

        lax.fori_loop(0, n_blocks, block_mean, 0)
        lax.fori_loop(0, n_blocks // per_tile, transpose_v, 0)

    q = q_ref[...]
    block_id = lax.broadcasted_iota(jnp.int32, (LANES, tile), 0).astype(F32)
    qry = lax.broadcasted_iota(jnp.int32, (LANES, tile), 1)
    own = (qi * per_tile).astype(F32) + jnp.floor(qry.astype(F32) * (1.0 / blk))
    gate = _dot_nt(kmean_ref[...].astype(BF16), q)
    gate = jnp.where(block_id < own, gate, -jnp.inf)
    sel = jnp.where(block_id == own, 1.0, 0.0)
    for slot in range(MOBA_TOPK):
        best = jnp.max(gate, axis=0, keepdims=True)
        idx = jnp.min(jnp.where(gate == best, block_id, float(LANES)), axis=0, keepdims=True)
        hit = block_id == idx
        slot_ok = jnp.where(float(slot) < own, 1.0, 0.0)
        sel = jnp.maximum(sel, jnp.where(hit, slot_ok, 0.0))
        gate = jnp.where(hit, -jnp.inf, gate)
    sel_ref[...] = sel

    slope = slope_ref[...] * LOG2E
    key_in_tile = lax.broadcasted_iota(jnp.int32, (tile, tile), 0).astype(F32)
    alibi = slope * key_in_tile

    def produce(j):
        shift = slope * ((j - qi) * tile).astype(F32)
        s = _dot_nt(_kv_block(k_ref, j, tile), q) + (alibi + shift)
        picked = jnp.concatenate(
            [jnp.broadcast_to(sel_ref[pl.ds(j * per_tile + r, 1), :], (blk, tile)) for r in range(per_tile)],
            axis=0)
        return jnp.where(picked > 0.5, s, MASKED)

    _flash_sweep(qi, produce, s0_ref, s1_ref, vt_ref, m_ref, acc_ref, tile)
    _softmax_finish(acc_ref, o_ref)


def _moba_attention(proj, n_heads, col_block):
    b, s, _ = proj.shape
    blk = MOBA_BLOCK
    n_blocks = s // blk
    tile = min(s, 2 * blk)
    assert s % tile == 0 and tile % blk == 0 and n_blocks <= LANES
    slopes = jnp.asarray([2.0 ** (-8.0 * (i + 1) / n_heads) for i in range(n_heads)], F32)
    slope_rows = jnp.broadcast_to(slopes[:, None, None], (n_heads, 1, tile))
    return pl.pallas_call(
        functools.partial(_moba_kernel, tile=tile, n_blocks=n_blocks),
        grid=(b, n_heads, s // tile),
        in_specs=[pl.BlockSpec((None, tile, HEAD_DIM), lambda i, h, t: (i, t, col_block + h)),
                  pl.BlockSpec((None, s, HEAD_DIM), lambda i, h, t: (i, 0, col_block + n_heads + h)),
                  pl.BlockSpec((None, s, HEAD_DIM), lambda i, h, t: (i, 0, col_block + 2 * n_heads + h)),
                  pl.BlockSpec((None, 1, tile), lambda i, h, t: (h, 0, 0))],
        out_specs=pl.BlockSpec((None, tile, HEAD_DIM), lambda i, h, t: (i, t, h)),
        out_shape=jax.ShapeDtypeStruct((b, s, n_heads * HEAD_DIM), BF16),
        scratch_shapes=[pltpu.VMEM((LANES, HEAD_DIM), F32), pltpu.VMEM((s // tile, VT_ROWS, tile), BF16),
                        pltpu.VMEM((LANES, tile), F32), pltpu.VMEM((tile, tile), F32),
                        pltpu.VMEM((tile, tile), F32), pltpu.VMEM((1, tile), F32),
                        pltpu.VMEM((VT_ROWS, tile), F32)],
        compiler_params=_params("arbitrary", "arbitrary", "arbitrary"),
        name="moba_attention",
    )(proj, proj, proj, slope_rows)


def _sb_kernel(q_ref, k_ref, v_ref, o_ref, decay_ref, acc_ref, *, tile):
    qi = pl.program_id(2)
    q = q_ref[...]
    row = lax.broadcasted_iota(jnp.int32, (tile, tile), 0)
    col = lax.broadcasted_iota(jnp.int32, (tile, tile), 1)
    after = jnp.where(row > col, 1.0, 0.0).astype(BF16)

    def block(j, causal):
        z = _dot_nt(q, _kv_block(k_ref, j, tile))
        drop = jnp.maximum(z, 0.0) + jnp.log2(1.0 + jnp.exp2(-jnp.abs(z)))
        if causal is not None:
            drop = jnp.where(causal, drop, 0.0)
        hi, lo = _split_bf16(drop)
        decay = _dot(hi, after) + _dot(lo, after) + decay_ref[...]
        a = jnp.exp2(z - drop - decay)
        if causal is not None:
            a = jnp.where(causal, a, 0.0)
        acc_ref[...] += _dot(a.astype(BF16), _kv_block(v_ref, j, tile))
        decay_ref[...] += jnp.sum(drop, axis=-1, keepdims=True)

    decay_ref[...] = jnp.zeros_like(decay_ref)
    acc_ref[...] = jnp.zeros_like(acc_ref)
    block(qi, col < row)

    def cond(carry):
        jj, least = carry
        return jnp.logical_and(jj < qi, least < SB_EXIT_BITS)

    def body(carry):
        jj, _ = carry
        block(qi - 1 - jj, None)
        return jj + 1, jnp.min(decay_ref[...])

    lax.while_loop(cond, body, (jnp.int32(0), jnp.min(decay_ref[...])))
    o_ref[...] = acc_ref[...].astype(o_ref.dtype)


def _sb_attention(proj, n_heads, col_block):
    b, s, _ = proj.shape
    tile = min(s, 256)
    return pl.pallas_call(
        functools.partial(_sb_kernel, tile=tile),
        grid=(b, n_heads, s // tile),
        in_specs=[pl.BlockSpec((None, tile, HEAD_DIM), lambda i, h, t: (i, t, col_block + h)),
                  pl.BlockSpec((None, s, HEAD_DIM), lambda i, h, t: (i, 0, col_block + n_heads + h)),
                  pl.BlockSpec((None, s, HEAD_DIM), lambda i, h, t: (i, 0, col_block + 2 * n_heads + h))],
        out_specs=pl.BlockSpec((None, tile, HEAD_DIM), lambda i, h, t: (i, t, h)),
        out_shape=jax.ShapeDtypeStruct((b, s, n_heads * HEAD_DIM), BF16),
        scratch_shapes=[pltpu.VMEM((tile, 1), F32), pltpu.VMEM((tile, HEAD_DIM), F32)],
        compiler_params=_params("arbitrary", "arbitrary", "arbitrary"),
        name="sb_attention",
    )(proj, proj, proj)


def _outproj_kernel(*refs, n_parts):
    x_ref, gate_ref = refs[0], refs[1]
    parts = refs[2:2 + 2 * n_parts]
    o_ref = refs[2 + 2 * n_parts]
    y = _dot(parts[0][...], parts[1][...])
    for p in range(1, n_parts):
        y += _dot(parts[2 * p][...], parts[2 * p + 1][...])
    o_ref[...] = x_ref[...] + gate_ref[...] * y


def _out_proj_residual(x, gate, parts):
    b, s, d = x.shape
    ts = min(s, 512)
    in_specs = [pl.BlockSpec((None, ts, d), lambda i, j: (i, j, 0)),
                pl.BlockSpec((None, 1, d), lambda i, j: (i, 0, 0))]
    args = [x, gate.reshape(b, 1, d)]
    for a, w in parts:
        kp = a.shape[-1]
        in_specs += [pl.BlockSpec((None, ts, kp), lambda i, j: (i, j, 0)),
                     pl.BlockSpec((kp, d), lambda i, j: (0, 0))]
        args += [a, w]
    return pl.pallas_call(
        functools.partial(_outproj_kernel, n_parts=len(parts)),
        grid=(b, s // ts),
        in_specs=in_specs,
        out_specs=pl.BlockSpec((None, ts, d), lambda i, j: (i, j, 0)),
        out_shape=jax.ShapeDtypeStruct((b, s, d), F32),
        compiler_params=_params("arbitrary", "arbitrary"),
        name="out_proj",
    )(*args)


def _router_kernel(h_ref, w_ref, b_ref, idx_ref, wgt_ref, rank_ref, cnt_ref, carry_ref):
    @pl.when(pl.program_id(0) == 0)
    def _():
        carry_ref[...] = jnp.zeros_like(carry_ref)

    tm = h_ref.shape[0] // SUBLANES
    logits = _dot(_load_token_tiles_bf16(h_ref, tm), w_ref[...]) + b_ref[...]
    lane = lax.broadcasted_iota(jnp.int32, (tm, LANES), 1).astype(F32)
    vals, idxs = [], []
    onehot = jnp.zeros((tm, LANES), F32)
    for _ in range(TOP_K):
        best = jnp.max(logits, axis=-1, keepdims=True)
        idx = jnp.min(jnp.where(logits == best, lane, float(LANES)), axis=-1, keepdims=True)
        hit = lane == idx
        onehot = jnp.where(hit, 1.0, onehot)
        logits = jnp.where(hit, -jnp.inf, logits)
        vals.append(best)
        idxs.append(idx)
    exps = [jnp.exp(v - vals[0]) for v in vals]
    denom = exps[0]
    for e in exps[1:]:
        denom = denom + e

    row = lax.broadcasted_iota(jnp.int32, (tm, tm), 0)
    col = lax.broadcasted_iota(jnp.int32, (tm, tm), 1)
    before = jnp.where(col < row, 1.0, 0.0).astype(BF16)
    prior = _dot(before, onehot.astype(BF16)) + carry_ref[0:1, :]

    idx_out = jnp.zeros((tm, LANES), F32)
    wgt_out = jnp.zeros((tm, LANES), F32)
    rank_out = jnp.zeros((tm, LANES), F32)
    for k in range(TOP_K):
        rank_k = jnp.sum(jnp.where(lane == idxs[k], prior, 0.0), axis=-1, keepdims=True)
        slot = lane == float(k)
        idx_out = jnp.where(slot, idxs[k], idx_out)
        wgt_out = jnp.where(slot, exps[k] / denom, wgt_out)
        rank_out = jnp.where(slot, rank_k, rank_out)
    idx_ref[...] = idx_out.astype(jnp.int32)
    wgt_ref[...] = wgt_out
    rank_ref[...] = rank_out.astype(jnp.int32)
    counts = carry_ref[...] + jnp.sum(onehot, axis=0, keepdims=True)
    carry_ref[...] = counts
    cnt_ref[...] = counts.astype(jnp.int32)


def _route(h, w_router, b_router):
    t = h.shape[0] // SUBLANES
    d = TOKEN_DIM
    e = w_router.shape[1]
    assert e <= LANES
    tm = min(t, 512)
    w_pad = jnp.zeros((d, LANES), BF16).at[:, :e].set(w_router.astype(BF16))
    b_pad = jnp.full((1, LANES), MASKED, F32).at[0, :e].set(b_router)
    tok_spec = pl.BlockSpec((tm, LANES), lambda i: (i, 0))
    return pl.pallas_call(
        _router_kernel,
        grid=(t // tm,),
        in_specs=[pl.BlockSpec((tm * SUBLANES, LANES), lambda i: (i, 0)),
                  pl.BlockSpec((d, LANES), lambda i: (0, 0)),
                  pl.BlockSpec((1, LANES), lambda i: (0, 0))],
        out_specs=[tok_spec, tok_spec, tok_spec, pl.BlockSpec((SUBLANES, LANES), lambda i: (0, 0))],
        out_shape=[jax.ShapeDtypeStruct((t, LANES), jnp.int32), jax.ShapeDtypeStruct((t, LANES), F32),
                   jax.ShapeDtypeStruct((t, LANES), jnp.int32),
                   jax.ShapeDtypeStruct((SUBLANES, LANES), jnp.int32)],
        scratch_shapes=[pltpu.VMEM((SUBLANES, LANES), F32)],
        compiler_params=_params("arbitrary"),
        name="moe_router",
    )(h, w_pad, b_pad)


def _for_each_row(n_rows, fn):
    def body(r, carry):
        fn(r)
        return carry

    lax.fori_loop(0, n_rows, body, 0, unroll=ROW_DMA_UNROLL)


def _moe_up_kernel(te_ref, first_ref, nvalid_ref, tok_ref, tok_next_ref, h_hbm, wg_ref, bg_ref, wu_ref, bu_ref,
                   o_ref, xbuf, sem, wg_bf, wu_bf):
    i = pl.program_id(0)
    slot = lax.rem(i, 2)
    n_valid = nvalid_ref[0]
    tm = xbuf.shape[1] // SUBLANES

    def row_copy(tok_smem, buf, r):
        src = h_hbm.at[pl.ds(pl.multiple_of(tok_smem[0, r], SUBLANES), SUBLANES)]
        dst = xbuf.at[buf, pl.ds(pl.multiple_of(r * SUBLANES, SUBLANES), SUBLANES)]
        return pltpu.make_async_copy(src, dst, sem.at[buf])

    @pl.when(jnp.logical_and(i == 0, n_valid > 0))
    def _():
        _for_each_row(tm, lambda r: row_copy(tok_ref, 0, r).start())

    @pl.when(i + 1 < n_valid)
    def _():
        _for_each_row(tm, lambda r: row_copy(tok_next_ref, 1 - slot, r).start())

    @pl.when(first_ref[i] == 1)
    def _():
        wg_bf[...] = wg_ref[...].astype(BF16)
        wu_bf[...] = wu_ref[...].astype(BF16)

    @pl.when(i < n_valid)
    def _():
        _for_each_row(tm, lambda r: row_copy(tok_ref, slot, r).wait())
        x = _load_token_tiles_bf16(xbuf, tm, lead=(slot,))
        g = jnp.minimum(_dot(x, wg_bf[...]) + bg_ref[...], SWIGLU_LIMIT)
        u = jnp.clip(_dot(x, wu_bf[...]) + bu_ref[...], -SWIGLU_LIMIT, SWIGLU_LIMIT)
        act = (u + 1.0) * (g / (1.0 + jnp.exp(-SWIGLU_ALPHA * g)))
        o_ref[...] = act.astype(o_ref.dtype)

    @pl.when(i >= n_valid)
    def _():
        o_ref[...] = jnp.zeros_like(o_ref)


def _moe_down_kernel(te_ref, first_ref, nvalid_ref, dst_ref, dst_prev_ref, a_ref, wd_ref, bd_ref, y_hbm,
                     obuf, sem, wd_bf, *, n_tiles, spare_row0):
    i = pl.program_id(0)
    slot = lax.rem(i, 2)
    n_valid = nvalid_ref[0]
    tm = obuf.shape[1] // SUBLANES

    def row_copy(dst_smem, buf, r):
        src = obuf.at[buf, pl.ds(pl.multiple_of(r * SUBLANES, SUBLANES), SUBLANES)]
        dst = y_hbm.at[pl.ds(pl.multiple_of(dst_smem[0, r], SUBLANES), SUBLANES)]
        return pltpu.make_async_copy(src, dst, sem.at[buf])

    @pl.when(i == 0)
    def _():
        obuf[...] = jnp.zeros_like(obuf)
        rows = tm * SUBLANES
        fills = [pltpu.make_async_copy(obuf.at[half], y_hbm.at[pl.ds(spare_row0 + half * rows, rows)], sem.at[half])
                 for half in range(2)]
        for fill in fills:
            fill.start()
        for fill in fills:
            fill.wait()

    @pl.when(first_ref[i] == 1)
    def _():
        wd_bf[...] = wd_ref[...].astype(BF16)

    @pl.when(i < n_valid)
    def _():
        _store_token_tiles(obuf, _dot(a_ref[...], wd_bf[...]) + bd_ref[...], lead=(slot,))
        _for_each_row(tm, lambda r: row_copy(dst_ref, slot, r).start())

    @pl.when(jnp.logical_and(i >= 1, i - 1 < n_valid))
    def _():
        _for_each_row(tm, lambda r: row_copy(dst_prev_ref, 1 - slot, r).wait())

    @pl.when(jnp.logical_and(i == n_tiles - 1, i < n_valid))
    def _():
        _for_each_row(tm, lambda r: row_copy(dst_ref, slot, r).wait())


def _expert_spec(layer, shape):
    return pl.BlockSpec((None, None) + shape, lambda i, te, first, nvalid: (layer, te[i], 0, 0))


def _moe_experts(h_packed, slot_token, slot_dst, n_out_rows, tile_expert, tile_first, n_valid, layer,
                 w_gate, b_gate, w_up, b_up, w_down, b_down, tm):
    d = TOKEN_DIM
    buf_shape = (2, tm * SUBLANES, LANES)
    depth, e, _, f = w_gate.shape
    n_tiles = slot_token.shape[0]
    last = n_tiles - 1
    smem_rows = lambda index: pl.BlockSpec((None, 1, tm), index, memory_space=pltpu.SMEM)
    row_spec = lambda width: pl.BlockSpec((tm, width), lambda i, te, first, nvalid: (i, 0))
    act = pl.pallas_call(
        _moe_up_kernel,
        grid_spec=pltpu.PrefetchScalarGridSpec(
            num_scalar_prefetch=3, grid=(n_tiles,),
            in_specs=[smem_rows(lambda i, te, first, nvalid: (i, 0, 0)),
                      smem_rows(lambda i, te, first, nvalid: (jnp.minimum(i + 1, last), 0, 0)),
                      pl.BlockSpec(memory_space=pl.ANY),
                      _expert_spec(layer, (d, f)), _expert_spec(layer, (1, f)),
                      _expert_spec(layer, (d, f)), _expert_spec(layer, (1, f))],
            out_specs=row_spec(f),
            scratch_shapes=[pltpu.VMEM(buf_shape, jnp.uint32), pltpu.SemaphoreType.DMA((2,)),
                            pltpu.VMEM((d, f), BF16), pltpu.VMEM((d, f), BF16)]),
        out_shape=jax.ShapeDtypeStruct((n_tiles * tm, f), BF16),
        compiler_params=_params("arbitrary"),
        name="moe_up",
    )(tile_expert, tile_first, n_valid, slot_token, slot_token, h_packed,
      w_gate, b_gate.reshape(depth, e, 1, f), w_up, b_up.reshape(depth, e, 1, f))
    return pl.pallas_call(
        functools.partial(_moe_down_kernel, n_tiles=n_tiles, spare_row0=(n_out_rows - 2 * tm) * SUBLANES),
        grid_spec=pltpu.PrefetchScalarGridSpec(
            num_scalar_prefetch=3, grid=(n_tiles,),
            in_specs=[smem_rows(lambda i, te, first, nvalid: (i, 0, 0)),
                      smem_rows(lambda i, te, first, nvalid: (jnp.maximum(i - 1, 0), 0, 0)),
                      row_spec(f), _expert_spec(layer, (f, d)), _expert_spec(layer, (1, d))],
            out_specs=pl.BlockSpec(memory_space=pl.ANY),
            scratch_shapes=[pltpu.VMEM(buf_shape, jnp.uint32), pltpu.SemaphoreType.DMA((2,)),
                            pltpu.VMEM((f, d), BF16)]),
        out_shape=jax.ShapeDtypeStruct((n_out_rows * SUBLANES, LANES), jnp.uint32),
        compiler_params=_params("arbitrary"),
        name="moe_down",
    )(tile_expert, tile_first, n_valid, slot_dst, slot_dst, act, w_down, b_down.reshape(depth, e, 1, d))


def _combine_kernel(*refs):
    x_ref, gate_ref, w_ref = refs[0], refs[1], refs[2]
    y_refs = refs[3:3 + TOP_K]
    o_ref = refs[3 + TOP_K]
    ts = x_ref.shape[0]
    w = w_ref[...]
    cols = None
    for k in range(TOP_K):
        chunks = [w[:, k:k + 1] * c for c in _load_token_tiles(y_refs[k], ts)]
        cols = chunks if cols is None else [a + c for a, c in zip(cols, chunks)]
    o_ref[...] = x_ref[...] + gate_ref[...] * jnp.concatenate(cols, axis=1)


def _moe_combine(x, gate, y, weights):
    b, s, d = x.shape
    ts = min(s, 256)
    per_b = s // ts
    tiles_per_k = b * per_b
    y_spec = lambda k: pl.BlockSpec((ts * SUBLANES, LANES), lambda i, j: (k * tiles_per_k + i * per_b + j, 0))
    return pl.pallas_call(
        _combine_kernel,
        grid=(b, per_b),
        in_specs=[pl.BlockSpec((None, ts, d), lambda i, j: (i, j, 0)),
                  pl.BlockSpec((None, 1, d), lambda i, j: (i, 0, 0)),
                  pl.BlockSpec((ts, LANES), lambda i, j: (i * per_b + j, 0))]
                 + [y_spec(k) for k in range(TOP_K)],
        out_specs=pl.BlockSpec((None, ts, d), lambda i, j: (i, j, 0)),
        out_shape=jax.ShapeDtypeStruct((b, s, d), F32),
        compiler_params=_params("arbitrary", "arbitrary"),
        name="moe_combine",
    )(x, gate.reshape(b, 1, d), weights, *([y] * TOP_K))


def _moe_ffn(x, h, gate, layer, w_router, b_router, w_gate, b_gate, w_up, b_up, w_down, b_down):
    b, s, d = x.shape
    t = b * s
    e = w_router.shape[1]
    tm = 256
    h2 = h.reshape(t * SUBLANES, LANES)
    idx_pad, wgt_pad, rank_pad, counts_pad = _route(h2, w_router, b_router)
    idx = idx_pad[:, :TOP_K]
    counts = counts_pad[0, :e]

    padded = ((counts + tm - 1) // tm) * tm
    ends = jnp.cumsum(padded)
    starts = ends - padded
    pos = starts[idx] + rank_pad[:, :TOP_K]
    n_slots = t * TOP_K + e * tm
    n_tiles = n_slots // tm
    tile_start = jnp.arange(n_tiles, dtype=jnp.int32) * tm
    tile_expert = jnp.sum((tile_start[:, None] >= ends[None, :]).astype(jnp.int32), axis=1)
    tile_expert = jnp.minimum(tile_expert, e - 1)
    n_valid = (ends[-1:] // tm).astype(jnp.int32)
    tile_first = jnp.concatenate([jnp.ones((1,), jnp.int32),
                                  (tile_expert[1:] != tile_expert[:-1]).astype(jnp.int32)])

    n_pairs = t * TOP_K
    slot_pair = jnp.full((n_slots,), -1, jnp.int32).at[pos.reshape(-1)].set(
        jnp.arange(n_pairs, dtype=jnp.int32), unique_indices=True)
    slot_id = jnp.arange(n_slots, dtype=jnp.int32)
    spare = n_pairs + ((slot_id // tm) % 2) * tm + slot_id % tm
    used = slot_pair >= 0
    pair_token, pair_k = slot_pair // TOP_K, slot_pair % TOP_K
    slot_token = (jnp.where(used, pair_token, 0) * SUBLANES).reshape(n_tiles, 1, tm)
    slot_dst = (jnp.where(used, pair_k * t + pair_token, spare) * SUBLANES).reshape(n_tiles, 1, tm)

    y = _moe_experts(h2, slot_token, slot_dst, n_pairs + 2 * tm, tile_expert, tile_first, n_valid, layer,
                     w_gate, b_gate, w_up, b_up, w_down, b_down, tm)
    return _moe_combine(x, gate, y, wgt_pad)


def kernel(x, c, mod_w, mod_b, mix_norm_g, ffn_norm_g, ab_w_in, ab_w_out, moba_q_gain, moba_k_gain,
           fox_w_in, fox_b_f, fox_w_out, fox_q_gain, fox_k_gain, router_w, router_b,
           exp_w_gate, exp_b_gate, exp_w_up, exp_b_up, exp_w_down, exp_b_down):
    b, s, d = x.shape
    depth = mod_w.shape[0]
    n_heads = d // HEAD_DIM
    n_moba = n_heads // 2
    n_sb = n_heads - n_moba
    mod = _adaln_mod(c, mod_w, mod_b)

    for layer in range(depth):
        sh1, sc1, g1, sh2, sc2, g2 = [mod[layer, :, i * d:(i + 1) * d] for i in range(N_MOD)]
        j = layer // 2
        h = _norm_mod(x, mix_norm_g[layer], sc1, sh1)
        if layer % 2 == 0:
            wa = n_moba * HEAD_DIM
            wb = n_sb * HEAD_DIM
            col_scale = jnp.concatenate([
                jnp.tile(moba_q_gain[j] * Q_PRESCALE, n_moba), jnp.tile(moba_k_gain[j], n_moba),
                jnp.ones((wa,), F32), jnp.full((wb,), Q_PRESCALE, F32), jnp.ones((2 * wb,), F32)])
            n_cols = col_scale.shape[0]
            proj = _in_proj(h.reshape(b * s, d), ab_w_in[j].astype(BF16), col_scale.reshape(1, n_cols), 2 * wa)
            proj = proj.reshape(b, s, n_cols)
            o_a = _moba_attention(proj, n_moba, 0)
            o_b = _sb_attention(proj, n_sb, 3 * n_moba)
            w_out = ab_w_out[j].astype(BF16)
            x = _out_proj_residual(x, g1, [(o_a, w_out[:wa]), (o_b, w_out[wa:])])
        else:
            w = n_heads * HEAD_DIM
            col_scale = jnp.concatenate([jnp.tile(fox_q_gain[j] * Q_PRESCALE, n_heads),
                                         jnp.tile(fox_k_gain[j], n_heads), jnp.ones((w,), F32)])
            proj = _in_proj(h.reshape(b * s, d), fox_w_in[j, :, :3 * w].astype(BF16),
                            col_scale.reshape(1, 3 * w), 2 * w)
            cum_f = _forget_cumsum(h, fox_w_in[j, :, 3 * w:], fox_b_f[j])
            o = _fox_attention(proj.reshape(b, s, 3 * w), cum_f, n_heads)
            x = _out_proj_residual(x, g1, [(o, fox_w_out[j].astype(BF16))])
        h = _norm_mod(x, ffn_norm_g[layer], sc2, sh2, packed=True)
        x = _moe_ffn(x, h, g2, layer, router_w[layer], router_b[layer], exp_w_gate, exp_b_gate,
                     exp_w_up, exp_b_up, exp_w_down, exp_b_down)
    return x
```

```python
import functools
import math

import jax
import jax.numpy as jnp
from jax import lax
from jax.experimental import pallas as pl
from jax.experimental.pallas import tpu as pltpu

F32 = jnp.float32
BF16 = jnp.bfloat16

HEAD_DIM = 128
MOBA_BLOCK = 256
MOBA_TOPK = 3
TOP_K = 4
SWIGLU_LIMIT = 7.0
SWIGLU_ALPHA = 1.702
RMS_EPS = 1e-5
N_MOD = 6
LOG2E = math.log2(math.e)
Q_PRESCALE = HEAD_DIM ** -0.5 * LOG2E
LANES = 128
SUBLANES = 8
BF16_ROWS = 16
MASKED = -1e30
SB_EXIT_BITS = 160.0
ROW_DMA_UNROLL = 16
VMEM_LIMIT_BYTES = 56 * 1024 * 1024


def _params(*semantics):
    return pltpu.CompilerParams(dimension_semantics=semantics, vmem_limit_bytes=VMEM_LIMIT_BYTES)


def _dot(a, b):
    return jnp.dot(a, b, preferred_element_type=F32)


def _dot_nt(a, b):
    return lax.dot_general(a, b, (((1,), (1,)), ((), ())), preferred_element_type=F32)


def _split_bf16(x):
    hi = x.astype(BF16)
    lo = (x - hi.astype(F32)).astype(BF16)
    return hi, lo


def _kv_block(ref, j, tk):
    return ref[pl.ds(pl.multiple_of(j * tk, tk), tk), :]


def _pack_bf16_pair(lo, hi):
    lo_bits = lax.bitcast_convert_type(lo.astype(BF16).astype(F32), jnp.uint32)
    hi_bits = lax.bitcast_convert_type(hi.astype(BF16).astype(F32), jnp.uint32)
    return lax.shift_right_logical(lo_bits, jnp.uint32(16)) | (hi_bits & jnp.uint32(0xFFFF0000))


def _unpack_bf16_pair(word):
    lo = lax.bitcast_convert_type(lax.shift_left(word, jnp.uint32(16)), F32)
    hi = lax.bitcast_convert_type(word & jnp.uint32(0xFFFF0000), F32)
    return lo, hi


TOKEN_DIM = 2 * SUBLANES * LANES


def _store_token_tiles(ref, y, lead=()):
    n = y.shape[0]
    half = TOKEN_DIM // 2
    for s in range(SUBLANES):
        lo = y[:, s * LANES:(s + 1) * LANES]
        hi = y[:, half + s * LANES:half + (s + 1) * LANES]
        ref[lead + (pl.ds(s, n, stride=SUBLANES), slice(None))] = _pack_bf16_pair(lo, hi)


def _load_token_tiles(ref, n, lead=()):
    los, his = [], []
    for s in range(SUBLANES):
        lo, hi = _unpack_bf16_pair(ref[lead + (pl.ds(s, n, stride=SUBLANES), slice(None))])
        los.append(lo)
        his.append(hi)
    return los + his


def _load_token_tiles_bf16(ref, n, lead=()):
    return jnp.concatenate([c.astype(BF16) for c in _load_token_tiles(ref, n, lead)], axis=1)


def _mod_kernel(c_ref, w_ref, b_ref, o_ref):
    c = c_ref[...]
    c_act = c / (1.0 + jnp.exp(-c))
    o_ref[...] = _dot(c_act, w_ref[...]) + b_ref[...]


def _adaln_mod(c, mod_w, mod_b):
    depth, d, n = mod_w.shape
    b = c.shape[0]
    bp = -(-b // SUBLANES) * SUBLANES
    c_pad = jnp.zeros((bp, d), F32).at[:b].set(c)
    tn = min(n, 1024)
    out = pl.pallas_call(
        _mod_kernel,
        grid=(depth, n // tn),
        in_specs=[pl.BlockSpec((bp, d), lambda l, j: (0, 0)),
                  pl.BlockSpec((None, d, tn), lambda l, j: (l, 0, j)),
                  pl.BlockSpec((None, 1, tn), lambda l, j: (l, 0, j))],
        out_specs=pl.BlockSpec((None, bp, tn), lambda l, j: (l, 0, j)),
        out_shape=jax.ShapeDtypeStruct((depth, bp, n), F32),
        compiler_params=_params("arbitrary", "arbitrary"),
        name="adaln_mod",
    )(c_pad, mod_w, mod_b.reshape(depth, 1, n))
    return out[:, :b]


def _norm_kernel(x_ref, g_ref, sc_ref, sh_ref, o_ref, *, packed):
    x = x_ref[...]
    ms = jnp.mean(x * x, axis=-1, keepdims=True)
    y = x * lax.rsqrt(ms + RMS_EPS)
    y = (y * g_ref[...]) * (1.0 + sc_ref[...]) + sh_ref[...]
    if packed:
        _store_token_tiles(o_ref, y)
    else:
        o_ref[...] = y.astype(o_ref.dtype)


def _norm_mod(x, gain, scale, shift, packed=False):
    b, s, d = x.shape
    ts = min(s, 512)
    if packed:
        assert d == TOKEN_DIM
        out_block, out_shape = (None, ts * SUBLANES, LANES), (b, s * SUBLANES, LANES)
    else:
        out_block, out_shape = (None, ts, d), (b, s, d)
    return pl.pallas_call(
        functools.partial(_norm_kernel, packed=packed),
        grid=(b, s // ts),
        in_specs=[pl.BlockSpec((None, ts, d), lambda i, j: (i, j, 0)),
                  pl.BlockSpec((1, d), lambda i, j: (0, 0)),
                  pl.BlockSpec((None, 1, d), lambda i, j: (i, 0, 0)),
                  pl.BlockSpec((None, 1, d), lambda i, j: (i, 0, 0))],
        out_specs=pl.BlockSpec(out_block, lambda i, j: (i, j, 0)),
        out_shape=jax.ShapeDtypeStruct(out_shape, jnp.uint32 if packed else BF16),
        compiler_params=_params("arbitrary", "arbitrary"),
        name="norm_mod",
    )(x, gain.reshape(1, d), scale.reshape(b, 1, d), shift.reshape(b, 1, d))


def _inproj_kernel(h_ref, w_ref, gain_ref, o_ref, *, n_norm_tiles):
    acc = _dot(h_ref[...], w_ref[...])
    j = pl.program_id(0)

    @pl.when(j < n_norm_tiles)
    def _():
        for g in range(acc.shape[1] // HEAD_DIM):
            sl = slice(g * HEAD_DIM, (g + 1) * HEAD_DIM)
            blk = acc[:, sl]
            ms = jnp.mean(blk * blk, axis=-1, keepdims=True)
            o_ref[:, sl] = (blk * lax.rsqrt(ms + RMS_EPS) * gain_ref[:, sl]).astype(o_ref.dtype)

    @pl.when(j >= n_norm_tiles)
    def _():
        o_ref[...] = (acc * gain_ref[...]).astype(o_ref.dtype)


def _in_proj(h, w, col_scale, n_norm_cols):
    t, d = h.shape
    n = w.shape[1]
    tm = min(t, 1024)
    tn = min(n, 1024)
    assert n_norm_cols % tn == 0 and n % tn == 0 and t % tm == 0
    return pl.pallas_call(
        functools.partial(_inproj_kernel, n_norm_tiles=n_norm_cols // tn),
        grid=(n // tn, t // tm),
        in_specs=[pl.BlockSpec((tm, d), lambda j, i: (i, 0)),
                  pl.BlockSpec((d, tn), lambda j, i: (0, j)),
                  pl.BlockSpec((1, tn), lambda j, i: (0, j))],
        out_specs=pl.BlockSpec((tm, tn), lambda j, i: (i, j)),
        out_shape=jax.ShapeDtypeStruct((t, n), BF16),
        compiler_params=_params("arbitrary", "arbitrary"),
        name="in_proj",
    )(h, w, col_scale)


def _fgate_kernel(h_ref, wf_ref, bf_ref, o_ref, carry_ref):
    @pl.when(pl.program_id(1) == 0)
    def _():
        carry_ref[...] = jnp.zeros_like(carry_ref)

    ts = h_ref.shape[0]
    logit = _dot(h_ref[...], wf_ref[...]) + bf_ref[...]
    log_f = jnp.minimum(logit, 0.0) - jnp.log(1.0 + jnp.exp(-jnp.abs(logit)))
    row = lax.broadcasted_iota(jnp.int32, (ts, ts), 0)
    col = lax.broadcasted_iota(jnp.int32, (ts, ts), 1)
    lower = jnp.where(col <= row, 1.0, 0.0).astype(BF16)
    hi, lo = _split_bf16(log_f)
    cum = _dot(lower, hi) + _dot(lower, lo) + carry_ref[0:1, :]
    o_ref[...] = cum
    carry_ref[...] = jnp.broadcast_to(cum[ts - 1:ts, :], carry_ref.shape)


def _forget_cumsum(h, w_f, b_f):
    b, s, d = h.shape
    nh = w_f.shape[1]
    assert nh <= LANES
    ts = min(s, 512)
    w_pad = jnp.zeros((d, LANES), BF16).at[:, :nh].set(w_f.astype(BF16))
    b_pad = jnp.zeros((1, LANES), F32).at[0, :nh].set(b_f)
    return pl.pallas_call(
        _fgate_kernel,
        grid=(b, s // ts),
        in_specs=[pl.BlockSpec((None, ts, d), lambda i, j: (i, j, 0)),
                  pl.BlockSpec((d, LANES), lambda i, j: (0, 0)),
                  pl.BlockSpec((1, LANES), lambda i, j: (0, 0))],
        out_specs=pl.BlockSpec((None, ts, LANES), lambda i, j: (i, j, 0)),
        out_shape=jax.ShapeDtypeStruct((b, s, LANES), F32),
        scratch_shapes=[pltpu.VMEM((SUBLANES, LANES), F32)],
        compiler_params=_params("arbitrary", "arbitrary"),
        name="forget_cumsum",
    )(h, w_pad, b_pad)


VT_ROWS = HEAD_DIM + BF16_ROWS


def _store_v_transposed(v_ref, vt_ref, c, tile):
    vb = _kv_block(v_ref, c, tile).astype(F32)
    vt_ref[c, 0:HEAD_DIM, :] = vb.T.astype(BF16)
    vt_ref[c, HEAD_DIM:VT_ROWS, :] = jnp.ones((BF16_ROWS, tile), BF16)


def _softmax_accumulate(s, vt_blk, m_ref, acc_ref):
    m_old = m_ref[...]
    m_new = jnp.maximum(m_old, jnp.max(s, axis=0, keepdims=True))
    alpha = jnp.exp2(m_old - m_new)
    p = jnp.exp2((s - m_new).astype(BF16))
    m_ref[...] = m_new
    acc_ref[...] = alpha * acc_ref[...] + _dot(vt_blk, p)


def _softmax_finish(acc_ref, o_ref):
    acc = acc_ref[...]
    out_t = acc[0:HEAD_DIM, :] / acc[HEAD_DIM:HEAD_DIM + 1, :]
    o_ref[...] = out_t.T.astype(o_ref.dtype)


def _flash_sweep(qi, n_past, produce, s0_ref, s1_ref, vt_ref, m_ref, acc_ref, tile):
    m_ref[...] = jnp.full(m_ref.shape, MASKED, F32)
    acc_ref[...] = jnp.zeros_like(acc_ref)
    key = lax.broadcasted_iota(jnp.int32, (tile, tile), 0)
    qry = lax.broadcasted_iota(jnp.int32, (tile, tile), 1)
    s0_ref[...] = jnp.where(key <= qry, produce(qi), MASKED)
    n_pos = n_past + 1
    farthest = qi - n_past

    def pair(i, carry):
        blk = qi - 2 * i
        s1_ref[...] = produce(blk - 1)
        _softmax_accumulate(s0_ref[...], vt_ref[blk], m_ref, acc_ref)
        s0_ref[...] = produce(jnp.maximum(blk - 2, farthest))
        _softmax_accumulate(s1_ref[...], vt_ref[blk - 1], m_ref, acc_ref)
        return carry

    lax.fori_loop(0, n_pos // 2, pair, 0)

    @pl.when(lax.rem(n_pos, 2) == 1)
    def _():
        _softmax_accumulate(s0_ref[...], vt_ref[farthest], m_ref, acc_ref)


SKIP_BITS = 162.0


def _logit_spread_bound(q, kmax_sq_ref):
    q32 = q.astype(F32)
    qmax_sq = jnp.max(jnp.sum(q32 * q32, axis=-1, keepdims=True), axis=0, keepdims=True)
    return 2.0 * jnp.sqrt(qmax_sq * kmax_sq_ref[0:1, 0:1])


def _update_kmax_sq(kmax_sq_ref, kb):
    blk_max = jnp.max(jnp.sum(kb * kb, axis=-1, keepdims=True), axis=0, keepdims=True)
    kmax_sq_ref[...] = jnp.maximum(kmax_sq_ref[...], jnp.broadcast_to(blk_max, kmax_sq_ref.shape))


def _fox_kernel(q_ref, k_ref, v_ref, f_ref, o_ref, vt_ref, frep_ref, fend_ref, kmax_sq_ref, s0_ref, s1_ref, m_ref,
                acc_ref, *, tile, n_kv):
    head = pl.program_id(1)
    qi = pl.program_id(2)

    @pl.when(qi == 0)
    def _():
        lane = lax.broadcasted_iota(jnp.int32, (tile, LANES), 1)
        fend_ref[...] = jnp.zeros_like(fend_ref)
        kmax_sq_ref[...] = jnp.zeros_like(kmax_sq_ref)

        def prep(c, carry):
            _store_v_transposed(v_ref, vt_ref, c, tile)
            _update_kmax_sq(kmax_sq_ref, _kv_block(k_ref, c, tile).astype(F32))
            f_blk = _kv_block(f_ref, c, tile)
            f_col = jnp.sum(jnp.where(lane == head, f_blk, 0.0), axis=-1, keepdims=True)
            f_rep = jnp.broadcast_to(f_col * LOG2E, (tile, LANES))
            frep_ref[pl.ds(pl.multiple_of(c * tile, tile), tile), :] = f_rep
            fend_ref[pl.ds(c, 1), :] = f_rep[tile - 1:tile, :]
            return carry

        lax.fori_loop(0, n_kv, prep, 0)

    q = q_ref[...]
    f_first = frep_ref[pl.ds(pl.multiple_of(qi * tile, tile), 1), :]
    gap = fend_ref[...] - f_first
    block_id = lax.broadcasted_iota(jnp.int32, fend_ref.shape, 0)
    needed = jnp.logical_and(block_id < qi, gap <= _logit_spread_bound(q, kmax_sq_ref) + SKIP_BITS)
    n_past = jnp.sum(jnp.where(needed, 1.0, 0.0)[:, 0:1]).astype(jnp.int32)

    def produce(j):
        f_rep = _kv_block(frep_ref, j, tile)
        return _dot_nt(_kv_block(k_ref, j, tile), q) - jnp.concatenate([f_rep] * (tile // LANES), axis=1)

    _flash_sweep(qi, n_past, produce, s0_ref, s1_ref, vt_ref, m_ref, acc_ref, tile)
    _softmax_finish(acc_ref, o_ref)


def _fox_attention(proj, cum_f, n_heads):
    b, s, _ = proj.shape
    tile = min(s, 512)
    n_kv = s // tile
    return pl.pallas_call(
        functools.partial(_fox_kernel, tile=tile, n_kv=n_kv),
        grid=(b, n_heads, s // tile),
        in_specs=[pl.BlockSpec((None, tile, HEAD_DIM), lambda i, h, t: (i, t, h)),
                  pl.BlockSpec((None, s, HEAD_DIM), lambda i, h, t: (i, 0, n_heads + h)),
                  pl.BlockSpec((None, s, HEAD_DIM), lambda i, h, t: (i, 0, 2 * n_heads + h)),
                  pl.BlockSpec((None, s, LANES), lambda i, h, t: (i, 0, 0))],
        out_specs=pl.BlockSpec((None, tile, HEAD_DIM), lambda i, h, t: (i, t, h)),
        out_shape=jax.ShapeDtypeStruct((b, s, n_heads * HEAD_DIM), BF16),
        scratch_shapes=[pltpu.VMEM((n_kv, VT_ROWS, tile), BF16), pltpu.VMEM((s, LANES), F32),
                        pltpu.VMEM((-(-n_kv // SUBLANES) * SUBLANES, LANES), F32),
                        pltpu.VMEM((SUBLANES, LANES), F32),
                        pltpu.VMEM((tile, tile), F32), pltpu.VMEM((tile, tile), F32),
                        pltpu.VMEM((1, tile), F32),
                        pltpu.VMEM((VT_ROWS, tile), F32)],
        compiler_params=_params("arbitrary", "arbitrary", "arbitrary"),
        name="fox_attention",
    )(proj, proj, proj, cum_f)


def _moba_kernel(q_ref, k_ref, v_ref, slope_ref, o_ref, kmean_ref, kmax_sq_ref, vt_ref, sel_ref, s0_ref, s1_ref,
                 m_ref, acc_ref, *, tile, n_blocks):
    blk = MOBA_BLOCK
    per_tile = tile // blk
    qi = pl.program_id(2)

    @pl.when(qi == 0)
    def _():
        kmean_ref[...] = jnp.zeros_like(kmean_ref)
        kmax_sq_ref[...] = jnp.zeros_like(kmax_sq_ref)

        def block_mean(n, carry):
            kb = _kv_block(k_ref, n, blk).astype(F32)
            kmean_ref[pl.ds(n, 1), :] = jnp.mean(kb, axis=0, keepdims=True)
            _update_kmax_sq(kmax_sq_ref, kb)
            return carry

        def transpose_v(c, carry):
            _store_v_transposed(v_ref, vt_ref, c, tile)
            return carry

        lax.fori_loop(0, n_blocks, block_mean, 0)
        lax.fori_loop(0, n_blocks // per_tile, transpose_v, 0)

    q = q_ref[...]
    block_id = lax.broadcasted_iota(jnp.int32, (LANES, tile), 0).astype(F32)
    qry = lax.broadcasted_iota(jnp.int32, (LANES, tile), 1)
    own = (qi * per_tile).astype(F32) + jnp.floor(qry.astype(F32) * (1.0 / blk))
    gate = _dot_nt(kmean_ref[...].astype(BF16), q)
    gate = jnp.where(block_id < own, gate, -jnp.inf)
    sel = jnp.where(block_id == own, 1.0, 0.0)
    for slot in range(MOBA_TOPK):
        best = jnp.max(gate, axis=0, keepdims=True)
        idx = jnp.min(jnp.where(gate == best, block_id, float(LANES)), axis=0, keepdims=True)
        hit = block_id == idx
        slot_ok = jnp.where(float(slot) < own, 1.0, 0.0)
        sel = jnp.maximum(sel, jnp.where(hit, slot_ok, 0.0))
        gate = jnp.where(hit, -jnp.inf, gate)
    sel_ref[...] = sel

    slope = slope_ref[...] * LOG2E
    key_in_tile = lax.broadcasted_iota(jnp.int32, (tile, tile), 0).astype(F32)
    alibi = slope * key_in_tile

    def produce(j):
        shift = slope * ((j - qi) * tile).astype(F32)
        s = _dot_nt(_kv_block(k_ref, j, tile), q) + (alibi + shift)
        picked = jnp.concatenate(
            [jnp.broadcast_to(sel_ref[pl.ds(j * per_tile + r, 1), :], (blk, tile)) for r in range(per_tile)],
            axis=0)
        return jnp.where(picked > 0.5, s, MASKED)

    reach = (_logit_spread_bound(q, kmax_sq_ref) + SKIP_BITS) / slope
    tiles_back = jnp.floor((reach - 1.0) * (1.0 / tile)) + 2.0
    n_past = jnp.minimum(jnp.minimum(jnp.max(tiles_back), float(n_blocks)).astype(jnp.int32), qi)

    _flash_sweep(qi, n_past, produce, s0_ref, s1_ref, vt_ref, m_ref, acc_ref, tile)
    _softmax_finish(acc_ref, o_ref)


def _moba_attention(proj, n_heads, col_block):
    b, s, _ = proj.shape
    blk = MOBA_BLOCK
    n_blocks = s // blk
    tile = min(s, 2 * blk)
    assert s % tile == 0 and tile % blk == 0 and n_blocks <= LANES
    slopes = jnp.asarray([2.0 ** (-8.0 * (i + 1) / n_heads) for i in range(n_heads)], F32)
    slope_rows = jnp.broadcast_to(slopes[:, None, None], (n_heads, 1, tile))
    return pl.pallas_call(
        functools.partial(_moba_kernel, tile=tile, n_blocks=n_blocks),
        grid=(b, n_heads, s // tile),
        in_specs=[pl.BlockSpec((None, tile, HEAD_DIM), lambda i, h, t: (i, t, col_block + h)),
                  pl.BlockSpec((None, s, HEAD_DIM), lambda i, h, t: (i, 0, col_block + n_heads + h)),
                  pl.BlockSpec((None, s, HEAD_DIM), lambda i, h, t: (i, 0, col_block + 2 * n_heads + h)),
                  pl.BlockSpec((None, 1, tile), lambda i, h, t: (h, 0, 0))],
        out_specs=pl.BlockSpec((None, tile, HEAD_DIM), lambda i, h, t: (i, t, h)),
        out_shape=jax.ShapeDtypeStruct((b, s, n_heads * HEAD_DIM), BF16),
        scratch_shapes=[pltpu.VMEM((LANES, HEAD_DIM), F32), pltpu.VMEM((SUBLANES, LANES), F32),
                        pltpu.VMEM((s // tile, VT_ROWS, tile), BF16),
                        pltpu.VMEM((LANES, tile), F32), pltpu.VMEM((tile, tile), F32),
                        pltpu.VMEM((tile, tile), F32), pltpu.VMEM((1, tile), F32),
                        pltpu.VMEM((VT_ROWS, tile), F32)],
        compiler_params=_params("arbitrary", "arbitrary", "arbitrary"),
        name="moba_attention",
    )(proj, proj, proj, slope_rows)


def _sb_kernel(q_ref, k_ref, v_ref, o_ref, decay_ref, acc_ref, *, tile):
    qi = pl.program_id(2)
    q = q_ref[...]
    row = lax.broadcasted_iota(jnp.int32, (tile, tile), 0)
    col = lax.broadcasted_iota(jnp.int32, (tile, tile), 1)
    after = jnp.where(row > col, 1.0, 0.0).astype(BF16)

    def block(j, causal):
        z = _dot_nt(q, _kv_block(k_ref, j, tile))
        drop = jnp.maximum(z, 0.0) + jnp.log2(1.0 + jnp.exp2(-jnp.abs(z)))
        if causal is not None:
            drop = jnp.where(causal, drop, 0.0)
        hi, lo = _split_bf16(drop)
        decay = _dot(hi, after) + _dot(lo, after) + decay_ref[...]
        a = jnp.exp2(z - drop - decay)
        if causal is not None:
            a = jnp.where(causal, a, 0.0)
        acc_ref[...] += _dot(a.astype(BF16), _kv_block(v_ref, j, tile))
        decay_ref[...] += jnp.sum(drop, axis=-1, keepdims=True)

    decay_ref[...] = jnp.zeros_like(decay_ref)
    acc_ref[...] = jnp.zeros_like(acc_ref)
    block(qi, col < row)

    def cond(carry):
        jj, least = carry
        return jnp.logical_and(jj < qi, least < SB_EXIT_BITS)

    def body(carry):
        jj, _ = carry
        block(qi - 1 - jj, None)
        return jj + 1, jnp.min(decay_ref[...])

    lax.while_loop(cond, body, (jnp.int32(0), jnp.min(decay_ref[...])))
    o_ref[...] = acc_ref[...].astype(o_ref.dtype)


def _sb_attention(proj, n_heads, col_block):
    b, s, _ = proj.shape
    tile = min(s, 256)
    return pl.pallas_call(
        functools.partial(_sb_kernel, tile=tile),
        grid=(b, n_heads, s // tile),
        in_specs=[pl.BlockSpec((None, tile, HEAD_DIM), lambda i, h, t: (i, t, col_block + h)),
                  pl.BlockSpec((None, s, HEAD_DIM), lambda i, h, t: (i, 0, col_block + n_heads + h)),
                  pl.BlockSpec((None, s, HEAD_DIM), lambda i, h, t: (i, 0, col_block + 2 * n_heads + h))],
        out_specs=pl.BlockSpec((None, tile, HEAD_DIM), lambda i, h, t: (i, t, h)),
        out_shape=jax.ShapeDtypeStruct((b, s, n_heads * HEAD_DIM), BF16),
        scratch_shapes=[pltpu.VMEM((tile, 1), F32), pltpu.VMEM((tile, HEAD_DIM), F32)],
        compiler_params=_params("arbitrary", "arbitrary", "arbitrary"),
        name="sb_attention",
    )(proj, proj, proj)


def _outproj_kernel(*refs, n_parts):
    x_ref, gate_ref = refs[0], refs[1]
    parts = refs[2:2 + 2 * n_parts]
    o_ref = refs[2 + 2 * n_parts]
    y = _dot(parts[0][...], parts[1][...])
    for p in range(1, n_parts):
        y += _dot(parts[2 * p][...], parts[2 * p + 1][...])
    o_ref[...] = x_ref[...] + gate_ref[...] * y


def _out_proj_residual(x, gate, parts):
    b, s, d = x.shape
    ts = min(s, 512)
    in_specs = [pl.BlockSpec((None, ts, d), lambda i, j: (i, j, 0)),
                pl.BlockSpec((None, 1, d), lambda i, j: (i, 0, 0))]
    args = [x, gate.reshape(b, 1, d)]
    for a, w in parts:
        kp = a.shape[-1]
        in_specs += [pl.BlockSpec((None, ts, kp), lambda i, j: (i, j, 0)),
                     pl.BlockSpec((kp, d), lambda i, j: (0, 0))]
        args += [a, w]
    return pl.pallas_call(
        functools.partial(_outproj_kernel, n_parts=len(parts)),
        grid=(b, s // ts),
        in_specs=in_specs,
        out_specs=pl.BlockSpec((None, ts, d), lambda i, j: (i, j, 0)),
        out_shape=jax.ShapeDtypeStruct((b, s, d), F32),
        compiler_params=_params("arbitrary", "arbitrary"),
        name="out_proj",
    )(*args)


def _router_kernel(h_ref, w_ref, b_ref, idx_ref, wgt_ref, rank_ref, cnt_ref, carry_ref):
    @pl.when(pl.program_id(0) == 0)
    def _():
        carry_ref[...] = jnp.zeros_like(carry_ref)

    tm = h_ref.shape[0] // SUBLANES
    logits = _dot(_load_token_tiles_bf16(h_ref, tm), w_ref[...]) + b_ref[...]
    lane = lax.broadcasted_iota(jnp.int32, (tm, LANES), 1).astype(F32)
    vals, idxs = [], []
    onehot = jnp.zeros((tm, LANES), F32)
    for _ in range(TOP_K):
        best = jnp.max(logits, axis=-1, keepdims=True)
        idx = jnp.min(jnp.where(logits == best, lane, float(LANES)), axis=-1, keepdims=True)
        hit = lane == idx
        onehot = jnp.where(hit, 1.0, onehot)
        logits = jnp.where(hit, -jnp.inf, logits)
        vals.append(best)
        idxs.append(idx)
    exps = [jnp.exp(v - vals[0]) for v in vals]
    denom = exps[0]
    for e in exps[1:]:
        denom = denom + e

    row = lax.broadcasted_iota(jnp.int32, (tm, tm), 0)
    col = lax.broadcasted_iota(jnp.int32, (tm, tm), 1)
    before = jnp.where(col < row, 1.0, 0.0).astype(BF16)
    prior = _dot(before, onehot.astype(BF16)) + carry_ref[0:1, :]

    idx_out = jnp.zeros((tm, LANES), F32)
    wgt_out = jnp.zeros((tm, LANES), F32)
    rank_out = jnp.zeros((tm, LANES), F32)
    for k in range(TOP_K):
        rank_k = jnp.sum(jnp.where(lane == idxs[k], prior, 0.0), axis=-1, keepdims=True)
        slot = lane == float(k)
        idx_out = jnp.where(slot, idxs[k], idx_out)
        wgt_out = jnp.where(slot, exps[k] / denom, wgt_out)
        rank_out = jnp.where(slot, rank_k, rank_out)
    idx_ref[...] = idx_out.astype(jnp.int32)
    wgt_ref[...] = wgt_out
    rank_ref[...] = rank_out.astype(jnp.int32)
    counts = carry_ref[...] + jnp.sum(onehot, axis=0, keepdims=True)
    carry_ref[...] = counts
    cnt_ref[...] = counts.astype(jnp.int32)


def _route(h, w_router, b_router):
    t = h.shape[0] // SUBLANES
    d = TOKEN_DIM
    e = w_router.shape[1]
    assert e <= LANES
    tm = min(t, 512)
    w_pad = jnp.zeros((d, LANES), BF16).at[:, :e].set(w_router.astype(BF16))
    b_pad = jnp.full((1, LANES), MASKED, F32).at[0, :e].set(b_router)
    tok_spec = pl.BlockSpec((tm, LANES), lambda i: (i, 0))
    return pl.pallas_call(
        _router_kernel,
        grid=(t // tm,),
        in_specs=[pl.BlockSpec((tm * SUBLANES, LANES), lambda i: (i, 0)),
                  pl.BlockSpec((d, LANES), lambda i: (0, 0)),
                  pl.BlockSpec((1, LANES), lambda i: (0, 0))],
        out_specs=[tok_spec, tok_spec, tok_spec, pl.BlockSpec((SUBLANES, LANES), lambda i: (0, 0))],
        out_shape=[jax.ShapeDtypeStruct((t, LANES), jnp.int32), jax.ShapeDtypeStruct((t, LANES), F32),
                   jax.ShapeDtypeStruct((t, LANES), jnp.int32),
                   jax.ShapeDtypeStruct((SUBLANES, LANES), jnp.int32)],
        scratch_shapes=[pltpu.VMEM((SUBLANES, LANES), F32)],
        compiler_params=_params("arbitrary"),
        name="moe_router",
    )(h, w_pad, b_pad)


def _for_each_row(n_rows, fn):
    def body(r, carry):
        fn(r)
        return carry

    lax.fori_loop(0, n_rows, body, 0, unroll=ROW_DMA_UNROLL)


def _moe_up_kernel(te_ref, first_ref, nvalid_ref, tok_ref, tok_next_ref, h_hbm, wg_ref, bg_ref, wu_ref, bu_ref,
                   o_ref, xbuf, sem, wg_bf, wu_bf):
    i = pl.program_id(0)
    slot = lax.rem(i, 2)
    n_valid = nvalid_ref[0]
    tm = xbuf.shape[1] // SUBLANES

    def row_copy(tok_smem, buf, r):
        src = h_hbm.at[pl.ds(pl.multiple_of(tok_smem[0, r], SUBLANES), SUBLANES)]
        dst = xbuf.at[buf, pl.ds(pl.multiple_of(r * SUBLANES, SUBLANES), SUBLANES)]
        return pltpu.make_async_copy(src, dst, sem.at[buf])

    @pl.when(jnp.logical_and(i == 0, n_valid > 0))
    def _():
        _for_each_row(tm, lambda r: row_copy(tok_ref, 0, r).start())

    @pl.when(i + 1 < n_valid)
    def _():
        _for_each_row(tm, lambda r: row_copy(tok_next_ref, 1 - slot, r).start())

    @pl.when(first_ref[i] == 1)
    def _():
        wg_bf[...] = wg_ref[...].astype(BF16)
        wu_bf[...] = wu_ref[...].astype(BF16)

    @pl.when(i < n_valid)
    def _():
        _for_each_row(tm, lambda r: row_copy(tok_ref, slot, r).wait())
        x = _load_token_tiles_bf16(xbuf, tm, lead=(slot,))
        g = jnp.minimum(_dot(x, wg_bf[...]) + bg_ref[...], SWIGLU_LIMIT)
        u = jnp.clip(_dot(x, wu_bf[...]) + bu_ref[...], -SWIGLU_LIMIT, SWIGLU_LIMIT)
        act = (u + 1.0) * (g / (1.0 + jnp.exp(-SWIGLU_ALPHA * g)))
        o_ref[...] = act.astype(o_ref.dtype)

    @pl.when(i >= n_valid)
    def _():
        o_ref[...] = jnp.zeros_like(o_ref)


def _moe_down_kernel(te_ref, first_ref, nvalid_ref, dst_ref, dst_prev_ref, a_ref, wd_ref, bd_ref, y_hbm,
                     obuf, sem, wd_bf, *, n_tiles, spare_row0):
    i = pl.program_id(0)
    slot = lax.rem(i, 2)
    n_valid = nvalid_ref[0]
    tm = obuf.shape[1] // SUBLANES

    def row_copy(dst_smem, buf, r):
        src = obuf.at[buf, pl.ds(pl.multiple_of(r * SUBLANES, SUBLANES), SUBLANES)]
        dst = y_hbm.at[pl.ds(pl.multiple_of(dst_smem[0, r], SUBLANES), SUBLANES)]
        return pltpu.make_async_copy(src, dst, sem.at[buf])

    @pl.when(i == 0)
    def _():
        obuf[...] = jnp.zeros_like(obuf)
        rows = tm * SUBLANES
        fills = [pltpu.make_async_copy(obuf.at[half], y_hbm.at[pl.ds(spare_row0 + half * rows, rows)], sem.at[half])
                 for half in range(2)]
        for fill in fills:
            fill.start()
        for fill in fills:
            fill.wait()

    @pl.when(first_ref[i] == 1)
    def _():
        wd_bf[...] = wd_ref[...].astype(BF16)

    @pl.when(i < n_valid)
    def _():
        _store_token_tiles(obuf, _dot(a_ref[...], wd_bf[...]) + bd_ref[...], lead=(slot,))
        _for_each_row(tm, lambda r: row_copy(dst_ref, slot, r).start())

    @pl.when(jnp.logical_and(i >= 1, i - 1 < n_valid))
    def _():
        _for_each_row(tm, lambda r: row_copy(dst_prev_ref, 1 - slot, r).wait())

    @pl.when(jnp.logical_and(i == n_tiles - 1, i < n_valid))
    def _():
        _for_each_row(tm, lambda r: row_copy(dst_ref, slot, r).wait())


def _expert_spec(layer, shape):
    return pl.BlockSpec((None, None) + shape, lambda i, te, first, nvalid: (layer, te[i], 0, 0))


def _moe_experts(h_packed, slot_token, slot_dst, n_out_rows, tile_expert, tile_first, n_valid, layer,
                 w_gate, b_gate, w_up, b_up, w_down, b_down, tm):
    d = TOKEN_DIM
    buf_shape = (2, tm * SUBLANES, LANES)
    depth, e, _, f = w_gate.shape
    n_tiles = slot_token.shape[0]
    last = n_tiles - 1
    smem_rows = lambda index: pl.BlockSpec((None, 1, tm), index, memory_space=pltpu.SMEM)
    row_spec = lambda width: pl.BlockSpec((tm, width), lambda i, te, first, nvalid: (i, 0))
    act = pl.pallas_call(
        _moe_up_kernel,
        grid_spec=pltpu.PrefetchScalarGridSpec(
            num_scalar_prefetch=3, grid=(n_tiles,),
            in_specs=[smem_rows(lambda i, te, first, nvalid: (i, 0, 0)),
                      smem_rows(lambda i, te, first, nvalid: (jnp.minimum(i + 1, last), 0, 0)),
                      pl.BlockSpec(memory_space=pl.ANY),
                      _expert_spec(layer, (d, f)), _expert_spec(layer, (1, f)),
                      _expert_spec(layer, (d, f)), _expert_spec(layer, (1, f))],
            out_specs=row_spec(f),
            scratch_shapes=[pltpu.VMEM(buf_shape, jnp.uint32), pltpu.SemaphoreType.DMA((2,)),
                            pltpu.VMEM((d, f), BF16), pltpu.VMEM((d, f), BF16)]),
        out_shape=jax.ShapeDtypeStruct((n_tiles * tm, f), BF16),
        compiler_params=_params("arbitrary"),
        name="moe_up",
    )(tile_expert, tile_first, n_valid, slot_token, slot_token, h_packed,
      w_gate, b_gate.reshape(depth, e, 1, f), w_up, b_up.reshape(depth, e, 1, f))
    return pl.pallas_call(
        functools.partial(_moe_down_kernel, n_tiles=n_tiles, spare_row0=(n_out_rows - 2 * tm) * SUBLANES),
        grid_spec=pltpu.PrefetchScalarGridSpec(
            num_scalar_prefetch=3, grid=(n_tiles,),
            in_specs=[smem_rows(lambda i, te, first, nvalid: (i, 0, 0)),
                      smem_rows(lambda i, te, first, nvalid: (jnp.maximum(i - 1, 0), 0, 0)),
                      row_spec(f), _expert_spec(layer, (f, d)), _expert_spec(layer, (1, d))],
            out_specs=pl.BlockSpec(memory_space=pl.ANY),
            scratch_shapes=[pltpu.VMEM(buf_shape, jnp.uint32), pltpu.SemaphoreType.DMA((2,)),
                            pltpu.VMEM((f, d), BF16)]),
        out_shape=jax.ShapeDtypeStruct((n_out_rows * SUBLANES, LANES), jnp.uint32),
        compiler_params=_params("arbitrary"),
        name="moe_down",
    )(tile_expert, tile_first, n_valid, slot_dst, slot_dst, act, w_down, b_down.reshape(depth, e, 1, d))


def _combine_kernel(*refs):
    x_ref, gate_ref, w_ref = refs[0], refs[1], refs[2]
    y_refs = refs[3:3 + TOP_K]
    o_ref = refs[3 + TOP_K]
    ts = x_ref.shape[0]
    w = w_ref[...]
    cols = None
    for k in range(TOP_K):
        chunks = [w[:, k:k + 1] * c for c in _load_token_tiles(y_refs[k], ts)]
        cols = chunks if cols is None else [a + c for a, c in zip(cols, chunks)]
    o_ref[...] = x_ref[...] + gate_ref[...] * jnp.concatenate(cols, axis=1)


def _moe_combine(x, gate, y, weights):
    b, s, d = x.shape
    ts = min(s, 256)
    per_b = s // ts
    tiles_per_k = b * per_b
    y_spec = lambda k: pl.BlockSpec((ts * SUBLANES, LANES), lambda i, j: (k * tiles_per_k + i * per_b + j, 0))
    return pl.pallas_call(
        _combine_kernel,
        grid=(b, per_b),
        in_specs=[pl.BlockSpec((None, ts, d), lambda i, j: (i, j, 0)),
                  pl.BlockSpec((None, 1, d), lambda i, j: (i, 0, 0)),
                  pl.BlockSpec((ts, LANES), lambda i, j: (i * per_b + j, 0))]
                 + [y_spec(k) for k in range(TOP_K)],
        out_specs=pl.BlockSpec((None, ts, d), lambda i, j: (i, j, 0)),
        out_shape=jax.ShapeDtypeStruct((b, s, d), F32),
        compiler_params=_params("arbitrary", "arbitrary"),
        name="moe_combine",
    )(x, gate.reshape(b, 1, d), weights, *([y] * TOP_K))


def _moe_ffn(x, h, gate, layer, w_router, b_router, w_gate, b_gate, w_up, b_up, w_down, b_down):
    b, s, d = x.shape
    t = b * s
    e = w_router.shape[1]
    tm = 512
    h2 = h.reshape(t * SUBLANES, LANES)
    idx_pad, wgt_pad, rank_pad, counts_pad = _route(h2, w_router, b_router)
    idx = idx_pad[:, :TOP_K]
    counts = counts_pad[0, :e]

    padded = ((counts + tm - 1) // tm) * tm
    ends = jnp.cumsum(padded)
    starts = ends - padded
    pos = starts[idx] + rank_pad[:, :TOP_K]
    n_slots = t * TOP_K + e * tm
    n_tiles = n_slots // tm
    tile_start = jnp.arange(n_tiles, dtype=jnp.int32) * tm
    tile_expert = jnp.sum((tile_start[:, None] >= ends[None, :]).astype(jnp.int32), axis=1)
    tile_expert = jnp.minimum(tile_expert, e - 1)
    n_valid = (ends[-1:] // tm).astype(jnp.int32)
    tile_first = jnp.concatenate([jnp.ones((1,), jnp.int32),
                                  (tile_expert[1:] != tile_expert[:-1]).astype(jnp.int32)])

    n_pairs = t * TOP_K
    slot_pair = jnp.full((n_slots,), -1, jnp.int32).at[pos.reshape(-1)].set(
        jnp.arange(n_pairs, dtype=jnp.int32), unique_indices=True)
    slot_id = jnp.arange(n_slots, dtype=jnp.int32)
    spare = n_pairs + ((slot_id // tm) % 2) * tm + slot_id % tm
    used = slot_pair >= 0
    pair_token, pair_k = slot_pair // TOP_K, slot_pair % TOP_K
    slot_token = (jnp.where(used, pair_token, 0) * SUBLANES).reshape(n_tiles, 1, tm)
    slot_dst = (jnp.where(used, pair_k * t + pair_token, spare) * SUBLANES).reshape(n_tiles, 1, tm)

    y = _moe_experts(h2, slot_token, slot_dst, n_pairs + 2 * tm, tile_expert, tile_first, n_valid, layer,
                     w_gate, b_gate, w_up, b_up, w_down, b_down, tm)
    return _moe_combine(x, gate, y, wgt_pad)


def kernel(x, c, mod_w, mod_b, mix_norm_g, ffn_norm_g, ab_w_in, ab_w_out, moba_q_gain, moba_k_gain,
           fox_w_in, fox_b_f, fox_w_out, fox_q_gain, fox_k_gain, router_w, router_b,
           exp_w_gate, exp_b_gate, exp_w_up, exp_b_up, exp_w_down, exp_b_down):
    b, s, d = x.shape
    depth = mod_w.shape[0]
    n_heads = d // HEAD_DIM
    n_moba = n_heads // 2
    n_sb = n_heads - n_moba
    mod = _adaln_mod(c, mod_w, mod_b)

    for layer in range(depth):
        sh1, sc1, g1, sh2, sc2, g2 = [mod[layer, :, i * d:(i + 1) * d] for i in range(N_MOD)]
        j = layer // 2
        h = _norm_mod(x, mix_norm_g[layer], sc1, sh1)
        if layer % 2 == 0:
            wa = n_moba * HEAD_DIM
            wb = n_sb * HEAD_DIM
            col_scale = jnp.concatenate([
                jnp.tile(moba_q_gain[j] * Q_PRESCALE, n_moba), jnp.tile(moba_k_gain[j], n_moba),
                jnp.ones((wa,), F32), jnp.full((wb,), Q_PRESCALE, F32), jnp.ones((2 * wb,), F32)])
            n_cols = col_scale.shape[0]
            proj = _in_proj(h.reshape(b * s, d), ab_w_in[j].astype(BF16), col_scale.reshape(1, n_cols), 2 * wa)
            proj = proj.reshape(b, s, n_cols)
            o_a = _moba_attention(proj, n_moba, 0)
            o_b = _sb_attention(proj, n_sb, 3 * n_moba)
            w_out = ab_w_out[j].astype(BF16)
            x = _out_proj_residual(x, g1, [(o_a, w_out[:wa]), (o_b, w_out[wa:])])
        else:
            w = n_heads * HEAD_DIM
            col_scale = jnp.concatenate([jnp.tile(fox_q_gain[j] * Q_PRESCALE, n_heads),
                                         jnp.tile(fox_k_gain[j], n_heads), jnp.ones((w,), F32)])
            proj = _in_proj(h.reshape(b * s, d), fox_w_in[j, :, :3 * w].astype(BF16),
                            col_scale.reshape(1, 3 * w), 2 * w)
            cum_f = _forget_cumsum(h, fox_w_in[j, :, 3 * w:], fox_b_f[j])
            o = _fox_attention(proj.reshape(b, s, 3 * w), cum_f, n_heads)
            x = _out_proj_residual(x, g1, [(o, fox_w_out[j].astype(BF16))])
        h = _norm_mod(x, ffn_norm_g[layer], sc2, sh2, packed=True)
        x = _moe_ffn(x, h, g2, layer, router_w[layer], router_b[layer], exp_w_gate, exp_b_gate,
                     exp_w_up, exp_b_up, exp_w_down, exp_b_down)
    return x
```

```python
import functools
import math

import jax
import jax.numpy as jnp
from jax import lax
from jax.experimental import pallas as pl
from jax.experimental.pallas import tpu as pltpu

F32 = jnp.float32
BF16 = jnp.bfloat16

HEAD_DIM = 128
MOBA_BLOCK = 256
MOBA_TOPK = 3
TOP_K = 4
SWIGLU_LIMIT = 7.0
SWIGLU_ALPHA = 1.702
RMS_EPS = 1e-5
N_MOD = 6
LOG2E = math.log2(math.e)
Q_PRESCALE = HEAD_DIM ** -0.5 * LOG2E
LANES = 128
SUBLANES = 8
BF16_ROWS = 16
MASKED = -1e30
SB_EXIT_BITS = 160.0
ROW_DMA_UNROLL = 16
ROW_DMA_PRIORITY = 1
VMEM_LIMIT_BYTES = 56 * 1024 * 1024


def _params(*semantics):
    return pltpu.CompilerParams(dimension_semantics=semantics, vmem_limit_bytes=VMEM_LIMIT_BYTES)


def _dot(a, b):
    return jnp.dot(a, b, preferred_element_type=F32)


def _dot_nt(a, b):
    return lax.dot_general(a, b, (((1,), (1,)), ((), ())), preferred_element_type=F32)


def _split_bf16(x):
    hi = x.astype(BF16)
    lo = (x - hi.astype(F32)).astype(BF16)
    return hi, lo


def _kv_block(ref, j, tk):
    return ref[pl.ds(pl.multiple_of(j * tk, tk), tk), :]


def _pack_bf16_pair(lo, hi):
    lo_bits = lax.bitcast_convert_type(lo.astype(BF16).astype(F32), jnp.uint32)
    hi_bits = lax.bitcast_convert_type(hi.astype(BF16).astype(F32), jnp.uint32)
    return lax.shift_right_logical(lo_bits, jnp.uint32(16)) | (hi_bits & jnp.uint32(0xFFFF0000))


def _unpack_bf16_pair(word):
    lo = lax.bitcast_convert_type(lax.shift_left(word, jnp.uint32(16)), F32)
    hi = lax.bitcast_convert_type(word & jnp.uint32(0xFFFF0000), F32)
    return lo, hi


TOKEN_DIM = 2 * SUBLANES * LANES


def _store_token_tiles(ref, y, lead=()):
    n = y.shape[0]
    half = TOKEN_DIM // 2
    for s in range(SUBLANES):
        lo = y[:, s * LANES:(s + 1) * LANES]
        hi = y[:, half + s * LANES:half + (s + 1) * LANES]
        ref[lead + (pl.ds(s, n, stride=SUBLANES), slice(None))] = _pack_bf16_pair(lo, hi)


def _load_token_tiles(ref, n, lead=()):
    los, his = [], []
    for s in range(SUBLANES):
        lo, hi = _unpack_bf16_pair(ref[lead + (pl.ds(s, n, stride=SUBLANES), slice(None))])
        los.append(lo)
        his.append(hi)
    return los + his


def _load_token_tiles_bf16(ref, n, lead=()):
    return jnp.concatenate([c.astype(BF16) for c in _load_token_tiles(ref, n, lead)], axis=1)


def _mod_kernel(c_ref, w_ref, b_ref, o_ref):
    c = c_ref[...]
    c_act = c / (1.0 + jnp.exp(-c))
    o_ref[...] = _dot(c_act, w_ref[...]) + b_ref[...]


def _adaln_mod(c, mod_w, mod_b):
    depth, d, n = mod_w.shape
    b = c.shape[0]
    bp = -(-b // SUBLANES) * SUBLANES
    c_pad = jnp.zeros((bp, d), F32).at[:b].set(c)
    tn = min(n, 1024)
    out = pl.pallas_call(
        _mod_kernel,
        grid=(depth, n // tn),
        in_specs=[pl.BlockSpec((bp, d), lambda l, j: (0, 0)),
                  pl.BlockSpec((None, d, tn), lambda l, j: (l, 0, j)),
                  pl.BlockSpec((None, 1, tn), lambda l, j: (l, 0, j))],
        out_specs=pl.BlockSpec((None, bp, tn), lambda l, j: (l, 0, j)),
        out_shape=jax.ShapeDtypeStruct((depth, bp, n), F32),
        compiler_params=_params("arbitrary", "arbitrary"),
        name="adaln_mod",
    )(c_pad, mod_w, mod_b.reshape(depth, 1, n))
    return out[:, :b]


def _norm_kernel(x_ref, g_ref, sc_ref, sh_ref, o_ref, *, packed):
    x = x_ref[...]
    ms = jnp.mean(x * x, axis=-1, keepdims=True)
    y = x * lax.rsqrt(ms + RMS_EPS)
    y = (y * g_ref[...]) * (1.0 + sc_ref[...]) + sh_ref[...]
    if packed:
        _store_token_tiles(o_ref, y)
    else:
        o_ref[...] = y.astype(o_ref.dtype)


def _norm_mod(x, gain, scale, shift, packed=False):
    b, s, d = x.shape
    ts = min(s, 512)
    if packed:
        assert d == TOKEN_DIM
        out_block, out_shape = (None, ts * SUBLANES, LANES), (b, s * SUBLANES, LANES)
    else:
        out_block, out_shape = (None, ts, d), (b, s, d)
    return pl.pallas_call(
        functools.partial(_norm_kernel, packed=packed),
        grid=(b, s // ts),
        in_specs=[pl.BlockSpec((None, ts, d), lambda i, j: (i, j, 0)),
                  pl.BlockSpec((1, d), lambda i, j: (0, 0)),
                  pl.BlockSpec((None, 1, d), lambda i, j: (i, 0, 0)),
                  pl.BlockSpec((None, 1, d), lambda i, j: (i, 0, 0))],
        out_specs=pl.BlockSpec(out_block, lambda i, j: (i, j, 0)),
        out_shape=jax.ShapeDtypeStruct(out_shape, jnp.uint32 if packed else BF16),
        compiler_params=_params("arbitrary", "arbitrary"),
        name="norm_mod",
    )(x, gain.reshape(1, d), scale.reshape(b, 1, d), shift.reshape(b, 1, d))


def _inproj_kernel(h_ref, w_ref, gain_ref, o_ref, *, n_norm_tiles):
    acc = _dot(h_ref[...], w_ref[...])
    j = pl.program_id(0)

    @pl.when(j < n_norm_tiles)
    def _():
        for g in range(acc.shape[1] // HEAD_DIM):
            sl = slice(g * HEAD_DIM, (g + 1) * HEAD_DIM)
            blk = acc[:, sl]
            ms = jnp.mean(blk * blk, axis=-1, keepdims=True)
            o_ref[:, sl] = (blk * lax.rsqrt(ms + RMS_EPS) * gain_ref[:, sl]).astype(o_ref.dtype)

    @pl.when(j >= n_norm_tiles)
    def _():
        o_ref[...] = (acc * gain_ref[...]).astype(o_ref.dtype)


def _in_proj(h, w, col_scale, n_norm_cols):
    t, d = h.shape
    n = w.shape[1]
    tm = min(t, 1024)
    tn = min(n, 1024)
    assert n_norm_cols % tn == 0 and n % tn == 0 and t % tm == 0
    return pl.pallas_call(
        functools.partial(_inproj_kernel, n_norm_tiles=n_norm_cols // tn),
        grid=(n // tn, t // tm),
        in_specs=[pl.BlockSpec((tm, d), lambda j, i: (i, 0)),
                  pl.BlockSpec((d, tn), lambda j, i: (0, j)),
                  pl.BlockSpec((1, tn), lambda j, i: (0, j))],
        out_specs=pl.BlockSpec((tm, tn), lambda j, i: (i, j)),
        out_shape=jax.ShapeDtypeStruct((t, n), BF16),
        compiler_params=_params("arbitrary", "arbitrary"),
        name="in_proj",
    )(h, w, col_scale)


def _fgate_kernel(h_ref, wf_ref, bf_ref, o_ref, carry_ref):
    @pl.when(pl.program_id(1) == 0)
    def _():
        carry_ref[...] = jnp.zeros_like(carry_ref)

    ts = h_ref.shape[0]
    logit = _dot(h_ref[...], wf_ref[...]) + bf_ref[...]
    log_f = jnp.minimum(logit, 0.0) - jnp.log(1.0 + jnp.exp(-jnp.abs(logit)))
    row = lax.broadcasted_iota(jnp.int32, (ts, ts), 0)
    col = lax.broadcasted_iota(jnp.int32, (ts, ts), 1)
    lower = jnp.where(col <= row, 1.0, 0.0).astype(BF16)
    hi, lo = _split_bf16(log_f)
    cum = _dot(lower, hi) + _dot(lower, lo) + carry_ref[0:1, :]
    o_ref[...] = cum
    carry_ref[...] = jnp.broadcast_to(cum[ts - 1:ts, :], carry_ref.shape)


def _forget_cumsum(h, w_f, b_f):
    b, s, d = h.shape
    nh = w_f.shape[1]
    assert nh <= LANES
    ts = min(s, 512)
    w_pad = jnp.zeros((d, LANES), BF16).at[:, :nh].set(w_f.astype(BF16))
    b_pad = jnp.zeros((1, LANES), F32).at[0, :nh].set(b_f)
    return pl.pallas_call(
        _fgate_kernel,
        grid=(b, s // ts),
        in_specs=[pl.BlockSpec((None, ts, d), lambda i, j: (i, j, 0)),
                  pl.BlockSpec((d, LANES), lambda i, j: (0, 0)),
                  pl.BlockSpec((1, LANES), lambda i, j: (0, 0))],
        out_specs=pl.BlockSpec((None, ts, LANES), lambda i, j: (i, j, 0)),
        out_shape=jax.ShapeDtypeStruct((b, s, LANES), F32),
        scratch_shapes=[pltpu.VMEM((SUBLANES, LANES), F32)],
        compiler_params=_params("arbitrary", "arbitrary"),
        name="forget_cumsum",
    )(h, w_pad, b_pad)


VT_ROWS = HEAD_DIM + BF16_ROWS


def _store_v_transposed(v_ref, vt_ref, c, tile):
    vb = _kv_block(v_ref, c, tile).astype(F32)
    vt_ref[c, 0:HEAD_DIM, :] = vb.T.astype(BF16)
    vt_ref[c, HEAD_DIM:VT_ROWS, :] = jnp.ones((BF16_ROWS, tile), BF16)


def _softmax_accumulate(s, vt_blk, m_ref, acc_ref):
    m_old = m_ref[...]
    m_new = jnp.maximum(m_old, jnp.max(s, axis=0, keepdims=True))
    alpha = jnp.exp2(m_old - m_new)
    p = jnp.exp2((s - m_new).astype(BF16))
    m_ref[...] = m_new
    acc_ref[...] = alpha * acc_ref[...] + _dot(vt_blk, p)


def _softmax_finish(acc_ref, o_ref):
    acc = acc_ref[...]
    out_t = acc[0:HEAD_DIM, :] / acc[HEAD_DIM:HEAD_DIM + 1, :]
    o_ref[...] = out_t.T.astype(o_ref.dtype)


def _flash_sweep(qi, n_past, produce, s0_ref, s1_ref, vt_ref, m_ref, acc_ref, tile):
    m_ref[...] = jnp.full(m_ref.shape, MASKED, F32)
    acc_ref[...] = jnp.zeros_like(acc_ref)
    key = lax.broadcasted_iota(jnp.int32, (tile, tile), 0)
    qry = lax.broadcasted_iota(jnp.int32, (tile, tile), 1)
    s0_ref[...] = jnp.where(key <= qry, produce(qi), MASKED)
    n_pos = n_past + 1
    farthest = qi - n_past

    def pair(i):
        blk = qi - 2 * i
        s1_ref[...] = produce(blk - 1)
        _softmax_accumulate(s0_ref[...], vt_ref[blk], m_ref, acc_ref)
        s0_ref[...] = produce(jnp.maximum(blk - 2, farthest))
        _softmax_accumulate(s1_ref[...], vt_ref[blk - 1], m_ref, acc_ref)

    def two_pairs(i, carry):
        pair(2 * i)
        pair(2 * i + 1)
        return carry

    n_pairs = n_pos // 2
    lax.fori_loop(0, n_pairs // 2, two_pairs, 0)

    @pl.when(lax.rem(n_pairs, 2) == 1)
    def _():
        pair(n_pairs - 1)

    @pl.when(lax.rem(n_pos, 2) == 1)
    def _():
        _softmax_accumulate(s0_ref[...], vt_ref[farthest], m_ref, acc_ref)


SKIP_BITS = 162.0


def _logit_spread_bound(q, kmax_sq_ref):
    q32 = q.astype(F32)
    qmax_sq = jnp.max(jnp.sum(q32 * q32, axis=-1, keepdims=True), axis=0, keepdims=True)
    return 2.0 * jnp.sqrt(qmax_sq * kmax_sq_ref[0:1, 0:1])


def _update_kmax_sq(kmax_sq_ref, kb):
    blk_max = jnp.max(jnp.sum(kb * kb, axis=-1, keepdims=True), axis=0, keepdims=True)
    kmax_sq_ref[...] = jnp.maximum(kmax_sq_ref[...], jnp.broadcast_to(blk_max, kmax_sq_ref.shape))


def _fox_kernel(q_ref, k_ref, v_ref, f_ref, o_ref, vt_ref, frep_ref, fend_ref, kmax_sq_ref, s0_ref, s1_ref, m_ref,
                acc_ref, *, tile, n_kv):
    head = pl.program_id(1)
    qi = pl.program_id(2)

    @pl.when(qi == 0)
    def _():
        lane = lax.broadcasted_iota(jnp.int32, (tile, LANES), 1)
        fend_ref[...] = jnp.zeros_like(fend_ref)
        kmax_sq_ref[...] = jnp.zeros_like(kmax_sq_ref)

        def prep(c, carry):
            _store_v_transposed(v_ref, vt_ref, c, tile)
            _update_kmax_sq(kmax_sq_ref, _kv_block(k_ref, c, tile).astype(F32))
            f_blk = _kv_block(f_ref, c, tile)
            f_col = jnp.sum(jnp.where(lane == head, f_blk, 0.0), axis=-1, keepdims=True)
            f_rep = jnp.broadcast_to(f_col * LOG2E, (tile, LANES))
            frep_ref[pl.ds(pl.multiple_of(c * tile, tile), tile), :] = f_rep
            fend_ref[pl.ds(c, 1), :] = f_rep[tile - 1:tile, :]
            return carry

        lax.fori_loop(0, n_kv, prep, 0)

    q = q_ref[...]
    f_first = frep_ref[pl.ds(pl.multiple_of(qi * tile, tile), 1), :]
    gap = fend_ref[...] - f_first
    block_id = lax.broadcasted_iota(jnp.int32, fend_ref.shape, 0)
    needed = jnp.logical_and(block_id < qi, gap <= _logit_spread_bound(q, kmax_sq_ref) + SKIP_BITS)
    n_past = jnp.sum(jnp.where(needed, 1.0, 0.0)[:, 0:1]).astype(jnp.int32)

    def produce(j):
        f_rep = _kv_block(frep_ref, j, tile)
        return _dot_nt(_kv_block(k_ref, j, tile), q) - jnp.concatenate([f_rep] * (tile // LANES), axis=1)

    _flash_sweep(qi, n_past, produce, s0_ref, s1_ref, vt_ref, m_ref, acc_ref, tile)
    _softmax_finish(acc_ref, o_ref)


def _fox_attention(proj, cum_f, n_heads):
    b, s, _ = proj.shape
    tile = min(s, 512)
    n_kv = s // tile
    return pl.pallas_call(
        functools.partial(_fox_kernel, tile=tile, n_kv=n_kv),
        grid=(b, n_heads, s // tile),
        in_specs=[pl.BlockSpec((None, tile, HEAD_DIM), lambda i, h, t: (i, t, h)),
                  pl.BlockSpec((None, s, HEAD_DIM), lambda i, h, t: (i, 0, n_heads + h)),
                  pl.BlockSpec((None, s, HEAD_DIM), lambda i, h, t: (i, 0, 2 * n_heads + h)),
                  pl.BlockSpec((None, s, LANES), lambda i, h, t: (i, 0, 0))],
        out_specs=pl.BlockSpec((None, tile, HEAD_DIM), lambda i, h, t: (i, t, h)),
        out_shape=jax.ShapeDtypeStruct((b, s, n_heads * HEAD_DIM), BF16),
        scratch_shapes=[pltpu.VMEM((n_kv, VT_ROWS, tile), BF16), pltpu.VMEM((s, LANES), F32),
                        pltpu.VMEM((-(-n_kv // SUBLANES) * SUBLANES, LANES), F32),
                        pltpu.VMEM((SUBLANES, LANES), F32),
                        pltpu.VMEM((tile, tile), F32), pltpu.VMEM((tile, tile), F32),
                        pltpu.VMEM((1, tile), F32),
                        pltpu.VMEM((VT_ROWS, tile), F32)],
        compiler_params=_params("arbitrary", "arbitrary", "arbitrary"),
        name="fox_attention",
    )(proj, proj, proj, cum_f)


def _moba_kernel(q_ref, k_ref, v_ref, slope_ref, o_ref, kmean_ref, kmax_sq_ref, vt_ref, sel_ref, s0_ref, s1_ref,
                 m_ref, acc_ref, *, tile, n_blocks):
    blk = MOBA_BLOCK
    per_tile = tile // blk
    qi = pl.program_id(2)

    @pl.when(qi == 0)
    def _():
        kmean_ref[...] = jnp.zeros_like(kmean_ref)
        kmax_sq_ref[...] = jnp.zeros_like(kmax_sq_ref)

        def block_mean(n, carry):
            kb = _kv_block(k_ref, n, blk).astype(F32)
            kmean_ref[pl.ds(n, 1), :] = jnp.mean(kb, axis=0, keepdims=True)
            _update_kmax_sq(kmax_sq_ref, kb)
            return carry

        def transpose_v(c, carry):
            _store_v_transposed(v_ref, vt_ref, c, tile)
            return carry

        lax.fori_loop(0, n_blocks, block_mean, 0)
        lax.fori_loop(0, n_blocks // per_tile, transpose_v, 0)

    q = q_ref[...]
    block_id = lax.broadcasted_iota(jnp.int32, (LANES, tile), 0).astype(F32)
    qry = lax.broadcasted_iota(jnp.int32, (LANES, tile), 1)
    own = (qi * per_tile).astype(F32) + jnp.floor(qry.astype(F32) * (1.0 / blk))
    gate = _dot_nt(kmean_ref[...].astype(BF16), q)
    gate = jnp.where(block_id < own, gate, -jnp.inf)
    sel = jnp.where(block_id == own, 1.0, 0.0)
    for slot in range(MOBA_TOPK):
        best = jnp.max(gate, axis=0, keepdims=True)
        idx = jnp.min(jnp.where(gate == best, block_id, float(LANES)), axis=0, keepdims=True)
        hit = block_id == idx
        slot_ok = jnp.where(float(slot) < own, 1.0, 0.0)
        sel = jnp.maximum(sel, jnp.where(hit, slot_ok, 0.0))
        gate = jnp.where(hit, -jnp.inf, gate)
    sel_ref[...] = jnp.where(sel > 0.5, 0.0, MASKED)

    slope = slope_ref[...] * LOG2E
    key_in_tile = lax.broadcasted_iota(jnp.int32, (tile, tile), 0).astype(F32)
    alibi = slope * key_in_tile

    def produce(j):
        shift = slope * ((j - qi) * tile).astype(F32)
        per_query = jnp.concatenate(
            [jnp.broadcast_to(sel_ref[pl.ds(j * per_tile + r, 1), :] + shift, (blk, tile))
             for r in range(per_tile)], axis=0)
        return (_dot_nt(_kv_block(k_ref, j, tile), q) + alibi) + per_query

    reach = (_logit_spread_bound(q, kmax_sq_ref) + SKIP_BITS) / slope
    tiles_back = jnp.floor((reach - 1.0) * (1.0 / tile)) + 2.0
    n_past = jnp.minimum(jnp.minimum(jnp.max(tiles_back), float(n_blocks)).astype(jnp.int32), qi)

    _flash_sweep(qi, n_past, produce, s0_ref, s1_ref, vt_ref, m_ref, acc_ref, tile)
    _softmax_finish(acc_ref, o_ref)


def _moba_attention(proj, n_heads, col_block):
    b, s, _ = proj.shape
    blk = MOBA_BLOCK
    n_blocks = s // blk
    tile = min(s, 2 * blk)
    assert s % tile == 0 and tile % blk == 0 and n_blocks <= LANES
    slopes = jnp.asarray([2.0 ** (-8.0 * (i + 1) / n_heads) for i in range(n_heads)], F32)
    slope_rows = jnp.broadcast_to(slopes[:, None, None], (n_heads, 1, tile))
    return pl.pallas_call(
        functools.partial(_moba_kernel, tile=tile, n_blocks=n_blocks),
        grid=(b, n_heads, s // tile),
        in_specs=[pl.BlockSpec((None, tile, HEAD_DIM), lambda i, h, t: (i, t, col_block + h)),
                  pl.BlockSpec((None, s, HEAD_DIM), lambda i, h, t: (i, 0, col_block + n_heads + h)),
                  pl.BlockSpec((None, s, HEAD_DIM), lambda i, h, t: (i, 0, col_block + 2 * n_heads + h)),
                  pl.BlockSpec((None, 1, tile), lambda i, h, t: (h, 0, 0))],
        out_specs=pl.BlockSpec((None, tile, HEAD_DIM), lambda i, h, t: (i, t, h)),
        out_shape=jax.ShapeDtypeStruct((b, s, n_heads * HEAD_DIM), BF16),
        scratch_shapes=[pltpu.VMEM((LANES, HEAD_DIM), F32), pltpu.VMEM((SUBLANES, LANES), F32),
                        pltpu.VMEM((s // tile, VT_ROWS, tile), BF16),
                        pltpu.VMEM((LANES, tile), F32), pltpu.VMEM((tile, tile), F32),
                        pltpu.VMEM((tile, tile), F32), pltpu.VMEM((1, tile), F32),
                        pltpu.VMEM((VT_ROWS, tile), F32)],
        compiler_params=_params("arbitrary", "arbitrary", "arbitrary"),
        name="moba_attention",
    )(proj, proj, proj, slope_rows)


def _sb_kernel(q_ref, k_ref, v_ref, o_ref, decay_ref, acc_ref, *, tile):
    qi = pl.program_id(2)
    q = q_ref[...]
    row = lax.broadcasted_iota(jnp.int32, (tile, tile), 0)
    col = lax.broadcasted_iota(jnp.int32, (tile, tile), 1)
    after = jnp.where(row > col, 1.0, 0.0).astype(BF16)

    def logits_and_drop(j, causal):
        z = _dot_nt(q, _kv_block(k_ref, j, tile))
        drop = jnp.maximum(z, 0.0) + jnp.log2(1.0 + jnp.exp2(-jnp.abs(z)))
        if causal is not None:
            drop = jnp.where(causal, drop, 0.0)
        return z, drop

    def weighted_values(j, z, drop, decay_right, causal):
        hi, lo = _split_bf16(drop)
        decay = _dot(hi, after) + _dot(lo, after) + decay_right
        a = jnp.exp2(z - drop - decay)
        if causal is not None:
            a = jnp.where(causal, a, 0.0)
        return _dot(a.astype(BF16), _kv_block(v_ref, j, tile))

    causal = col < row

    @pl.when(qi == 0)
    def _():
        z, drop = logits_and_drop(qi, causal)
        acc_ref[...] = weighted_values(qi, z, drop, 0.0, causal)
        decay_ref[...] = jnp.sum(drop, axis=-1, keepdims=True)

    @pl.when(qi > 0)
    def _():
        z_d, drop_d = logits_and_drop(qi, causal)
        z_p, drop_p = logits_and_drop(qi - 1, None)
        decay_d = jnp.sum(drop_d, axis=-1, keepdims=True)
        acc_ref[...] = (weighted_values(qi, z_d, drop_d, 0.0, causal)
                        + weighted_values(qi - 1, z_p, drop_p, decay_d, None))
        decay_ref[...] = decay_d + jnp.sum(drop_p, axis=-1, keepdims=True)

    def cond(carry):
        jj, least = carry
        return jnp.logical_and(jj < qi, least < SB_EXIT_BITS)

    def body(carry):
        jj, _ = carry
        j = qi - 1 - jj
        z, drop = logits_and_drop(j, None)
        acc_ref[...] += weighted_values(j, z, drop, decay_ref[...], None)
        decay_ref[...] += jnp.sum(drop, axis=-1, keepdims=True)
        return jj + 1, jnp.min(decay_ref[...])

    lax.while_loop(cond, body, (jnp.int32(1), jnp.min(decay_ref[...])))
    o_ref[...] = acc_ref[...].astype(o_ref.dtype)


def _sb_attention(proj, n_heads, col_block):
    b, s, _ = proj.shape
    tile = min(s, 256)
    return pl.pallas_call(
        functools.partial(_sb_kernel, tile=tile),
        grid=(b, n_heads, s // tile),
        in_specs=[pl.BlockSpec((None, tile, HEAD_DIM), lambda i, h, t: (i, t, col_block + h)),
                  pl.BlockSpec((None, s, HEAD_DIM), lambda i, h, t: (i, 0, col_block + n_heads + h)),
                  pl.BlockSpec((None, s, HEAD_DIM), lambda i, h, t: (i, 0, col_block + 2 * n_heads + h))],
        out_specs=pl.BlockSpec((None, tile, HEAD_DIM), lambda i, h, t: (i, t, h)),
        out_shape=jax.ShapeDtypeStruct((b, s, n_heads * HEAD_DIM), BF16),
        scratch_shapes=[pltpu.VMEM((tile, 1), F32), pltpu.VMEM((tile, HEAD_DIM), F32)],
        compiler_params=_params("arbitrary", "arbitrary", "arbitrary"),
        name="sb_attention",
    )(proj, proj, proj)


def _outproj_kernel(*refs, n_parts):
    x_ref, gate_ref = refs[0], refs[1]
    parts = refs[2:2 + 2 * n_parts]
    o_ref = refs[2 + 2 * n_parts]
    y = _dot(parts[0][...], parts[1][...])
    for p in range(1, n_parts):
        y += _dot(parts[2 * p][...], parts[2 * p + 1][...])
    o_ref[...] = x_ref[...] + gate_ref[...] * y


def _out_proj_residual(x, gate, parts):
    b, s, d = x.shape
    ts = min(s, 512)
    in_specs = [pl.BlockSpec((None, ts, d), lambda i, j: (i, j, 0)),
                pl.BlockSpec((None, 1, d), lambda i, j: (i, 0, 0))]
    args = [x, gate.reshape(b, 1, d)]
    for a, w in parts:
        kp = a.shape[-1]
        in_specs += [pl.BlockSpec((None, ts, kp), lambda i, j: (i, j, 0)),
                     pl.BlockSpec((kp, d), lambda i, j: (0, 0))]
        args += [a, w]
    return pl.pallas_call(
        functools.partial(_outproj_kernel, n_parts=len(parts)),
        grid=(b, s // ts),
        in_specs=in_specs,
        out_specs=pl.BlockSpec((None, ts, d), lambda i, j: (i, j, 0)),
        out_shape=jax.ShapeDtypeStruct((b, s, d), F32),
        compiler_params=_params("arbitrary", "arbitrary"),
        name="out_proj",
    )(*args)


def _router_kernel(h_ref, w_ref, b_ref, idx_ref, wgt_ref, rank_ref, cnt_ref, carry_ref):
    @pl.when(pl.program_id(0) == 0)
    def _():
        carry_ref[...] = jnp.zeros_like(carry_ref)

    tm = h_ref.shape[0] // SUBLANES
    logits = _dot(_load_token_tiles_bf16(h_ref, tm), w_ref[...]) + b_ref[...]
    lane = lax.broadcasted_iota(jnp.int32, (tm, LANES), 1).astype(F32)
    vals, idxs = [], []
    onehot = jnp.zeros((tm, LANES), F32)
    for _ in range(TOP_K):
        best = jnp.max(logits, axis=-1, keepdims=True)
        idx = jnp.min(jnp.where(logits == best, lane, float(LANES)), axis=-1, keepdims=True)
        hit = lane == idx
        onehot = jnp.where(hit, 1.0, onehot)
        logits = jnp.where(hit, -jnp.inf, logits)
        vals.append(best)
        idxs.append(idx)
    exps = [jnp.exp(v - vals[0]) for v in vals]
    denom = exps[0]
    for e in exps[1:]:
        denom = denom + e

    row = lax.broadcasted_iota(jnp.int32, (tm, tm), 0)
    col = lax.broadcasted_iota(jnp.int32, (tm, tm), 1)
    before = jnp.where(col < row, 1.0, 0.0).astype(BF16)
    prior = _dot(before, onehot.astype(BF16)) + carry_ref[0:1, :]

    idx_out = jnp.zeros((tm, LANES), F32)
    wgt_out = jnp.zeros((tm, LANES), F32)
    rank_out = jnp.zeros((tm, LANES), F32)
    for k in range(TOP_K):
        rank_k = jnp.sum(jnp.where(lane == idxs[k], prior, 0.0), axis=-1, keepdims=True)
        slot = lane == float(k)
        idx_out = jnp.where(slot, idxs[k], idx_out)
        wgt_out = jnp.where(slot, exps[k] / denom, wgt_out)
        rank_out = jnp.where(slot, rank_k, rank_out)
    idx_ref[...] = idx_out.astype(jnp.int32)
    wgt_ref[...] = wgt_out
    rank_ref[...] = rank_out.astype(jnp.int32)
    counts = carry_ref[...] + jnp.sum(onehot, axis=0, keepdims=True)
    carry_ref[...] = counts
    cnt_ref[...] = counts.astype(jnp.int32)


def _route(h, w_router, b_router):
    t = h.shape[0] // SUBLANES
    d = TOKEN_DIM
    e = w_router.shape[1]
    assert e <= LANES
    tm = min(t, 512)
    w_pad = jnp.zeros((d, LANES), BF16).at[:, :e].set(w_router.astype(BF16))
    b_pad = jnp.full((1, LANES), MASKED, F32).at[0, :e].set(b_router)
    tok_spec = pl.BlockSpec((tm, LANES), lambda i: (i, 0))
    return pl.pallas_call(
        _router_kernel,
        grid=(t // tm,),
        in_specs=[pl.BlockSpec((tm * SUBLANES, LANES), lambda i: (i, 0)),
                  pl.BlockSpec((d, LANES), lambda i: (0, 0)),
                  pl.BlockSpec((1, LANES), lambda i: (0, 0))],
        out_specs=[tok_spec, tok_spec, tok_spec, pl.BlockSpec((SUBLANES, LANES), lambda i: (0, 0))],
        out_shape=[jax.ShapeDtypeStruct((t, LANES), jnp.int32), jax.ShapeDtypeStruct((t, LANES), F32),
                   jax.ShapeDtypeStruct((t, LANES), jnp.int32),
                   jax.ShapeDtypeStruct((SUBLANES, LANES), jnp.int32)],
        scratch_shapes=[pltpu.VMEM((SUBLANES, LANES), F32)],
        compiler_params=_params("arbitrary"),
        name="moe_router",
    )(h, w_pad, b_pad)


def _for_each_row(n_rows, fn):
    def body(r, carry):
        fn(r)
        return carry

    lax.fori_loop(0, n_rows, body, 0, unroll=ROW_DMA_UNROLL)


def _moe_up_kernel(te_ref, first_ref, nvalid_ref, tok_ref, tok_next_ref, h_hbm, wg_ref, bg_ref, wu_ref, bu_ref,
                   o_ref, xbuf, sem, wg_bf, wu_bf):
    i = pl.program_id(0)
    slot = lax.rem(i, 2)
    n_valid = nvalid_ref[0]
    tm = xbuf.shape[1] // SUBLANES

    def row_copy(tok_smem, buf, r):
        src = h_hbm.at[pl.ds(pl.multiple_of(tok_smem[0, r], SUBLANES), SUBLANES)]
        dst = xbuf.at[buf, pl.ds(pl.multiple_of(r * SUBLANES, SUBLANES), SUBLANES)]
        return pltpu.make_async_copy(src, dst, sem.at[buf])

    @pl.when(jnp.logical_and(i == 0, n_valid > 0))
    def _():
        _for_each_row(tm, lambda r: row_copy(tok_ref, 0, r).start(priority=ROW_DMA_PRIORITY))

    @pl.when(i + 1 < n_valid)
    def _():
        _for_each_row(tm, lambda r: row_copy(tok_next_ref, 1 - slot, r).start(priority=ROW_DMA_PRIORITY))

    @pl.when(first_ref[i] == 1)
    def _():
        wg_bf[...] = wg_ref[...].astype(BF16)
        wu_bf[...] = wu_ref[...].astype(BF16)

    @pl.when(i < n_valid)
    def _():
        _for_each_row(tm, lambda r: row_copy(tok_ref, slot, r).wait())
        x = _load_token_tiles_bf16(xbuf, tm, lead=(slot,))
        g = jnp.minimum(_dot(x, wg_bf[...]) + bg_ref[...], SWIGLU_LIMIT)
        u = jnp.clip(_dot(x, wu_bf[...]) + bu_ref[...], -SWIGLU_LIMIT, SWIGLU_LIMIT)
        act = (u + 1.0) * (g / (1.0 + jnp.exp(-SWIGLU_ALPHA * g)))
        o_ref[...] = act.astype(o_ref.dtype)

    @pl.when(i >= n_valid)
    def _():
        o_ref[...] = jnp.zeros_like(o_ref)


def _moe_down_kernel(te_ref, first_ref, nvalid_ref, dst_ref, dst_prev_ref, a_ref, wd_ref, bd_ref, y_hbm,
                     obuf, sem, wd_bf, *, n_tiles, spare_row0):
    i = pl.program_id(0)
    slot = lax.rem(i, 2)
    n_valid = nvalid_ref[0]
    tm = obuf.shape[1] // SUBLANES

    def row_copy(dst_smem, buf, r):
        src = obuf.at[buf, pl.ds(pl.multiple_of(r * SUBLANES, SUBLANES), SUBLANES)]
        dst = y_hbm.at[pl.ds(pl.multiple_of(dst_smem[0, r], SUBLANES), SUBLANES)]
        return pltpu.make_async_copy(src, dst, sem.at[buf])

    @pl.when(i == 0)
    def _():
        obuf[...] = jnp.zeros_like(obuf)
        rows = tm * SUBLANES
        fills = [pltpu.make_async_copy(obuf.at[half], y_hbm.at[pl.ds(spare_row0 + half * rows, rows)], sem.at[half])
                 for half in range(2)]
        for fill in fills:
            fill.start()
        for fill in fills:
            fill.wait()

    @pl.when(first_ref[i] == 1)
    def _():
        wd_bf[...] = wd_ref[...].astype(BF16)

    @pl.when(i < n_valid)
    def _():
        _store_token_tiles(obuf, _dot(a_ref[...], wd_bf[...]) + bd_ref[...], lead=(slot,))
        _for_each_row(tm, lambda r: row_copy(dst_ref, slot, r).start(priority=ROW_DMA_PRIORITY))

    @pl.when(jnp.logical_and(i >= 1, i - 1 < n_valid))
    def _():
        _for_each_row(tm, lambda r: row_copy(dst_prev_ref, 1 - slot, r).wait())

    @pl.when(jnp.logical_and(i == n_tiles - 1, i < n_valid))
    def _():
        _for_each_row(tm, lambda r: row_copy(dst_ref, slot, r).wait())


def _expert_spec(layer, shape):
    return pl.BlockSpec((None, None) + shape, lambda i, te, first, nvalid: (layer, te[i], 0, 0))


def _moe_experts(h_packed, slot_token, slot_dst, n_out_rows, tile_expert, tile_first, n_valid, layer,
                 w_gate, b_gate, w_up, b_up, w_down, b_down, tm):
    d = TOKEN_DIM
    buf_shape = (2, tm * SUBLANES, LANES)
    depth, e, _, f = w_gate.shape
    n_tiles = slot_token.shape[0]
    last = n_tiles - 1
    smem_rows = lambda index: pl.BlockSpec((None, 1, tm), index, memory_space=pltpu.SMEM)
    row_spec = lambda width: pl.BlockSpec((tm, width), lambda i, te, first, nvalid: (i, 0))
    act = pl.pallas_call(
        _moe_up_kernel,
        grid_spec=pltpu.PrefetchScalarGridSpec(
            num_scalar_prefetch=3, grid=(n_tiles,),
            in_specs=[smem_rows(lambda i, te, first, nvalid: (i, 0, 0)),
                      smem_rows(lambda i, te, first, nvalid: (jnp.minimum(i + 1, last), 0, 0)),
                      pl.BlockSpec(memory_space=pl.ANY),
                      _expert_spec(layer, (d, f)), _expert_spec(layer, (1, f)),
                      _expert_spec(layer, (d, f)), _expert_spec(layer, (1, f))],
            out_specs=row_spec(f),
            scratch_shapes=[pltpu.VMEM(buf_shape, jnp.uint32), pltpu.SemaphoreType.DMA((2,)),
                            pltpu.VMEM((d, f), BF16), pltpu.VMEM((d, f), BF16)]),
        out_shape=jax.ShapeDtypeStruct((n_tiles * tm, f), BF16),
        compiler_params=_params("arbitrary"),
        name="moe_up",
    )(tile_expert, tile_first, n_valid, slot_token, slot_token, h_packed,
      w_gate, b_gate.reshape(depth, e, 1, f), w_up, b_up.reshape(depth, e, 1, f))
    return pl.pallas_call(
        functools.partial(_moe_down_kernel, n_tiles=n_tiles, spare_row0=(n_out_rows - 2 * tm) * SUBLANES),
        grid_spec=pltpu.PrefetchScalarGridSpec(
            num_scalar_prefetch=3, grid=(n_tiles,),
            in_specs=[smem_rows(lambda i, te, first, nvalid: (i, 0, 0)),
                      smem_rows(lambda i, te, first, nvalid: (jnp.maximum(i - 1, 0), 0, 0)),
                      row_spec(f), _expert_spec(layer, (f, d)), _expert_spec(layer, (1, d))],
            out_specs=pl.BlockSpec(memory_space=pl.ANY),
            scratch_shapes=[pltpu.VMEM(buf_shape, jnp.uint32), pltpu.SemaphoreType.DMA((2,)),
                            pltpu.VMEM((f, d), BF16)]),
        out_shape=jax.ShapeDtypeStruct((n_out_rows * SUBLANES, LANES), jnp.uint32),
        compiler_params=_params("arbitrary"),
        name="moe_down",
    )(tile_expert, tile_first, n_valid, slot_dst, slot_dst, act, w_down, b_down.reshape(depth, e, 1, d))


def _combine_kernel(*refs):
    x_ref, gate_ref, w_ref = refs[0], refs[1], refs[2]
    y_refs = refs[3:3 + TOP_K]
    o_ref = refs[3 + TOP_K]
    ts = x_ref.shape[0]
    w = w_ref[...]
    cols = None
    for k in range(TOP_K):
        chunks = [w[:, k:k + 1] * c for c in _load_token_tiles(y_refs[k], ts)]
        cols = chunks if cols is None else [a + c for a, c in zip(cols, chunks)]
    o_ref[...] = x_ref[...] + gate_ref[...] * jnp.concatenate(cols, axis=1)


def _moe_combine(x, gate, y, weights):
    b, s, d = x.shape
    ts = min(s, 256)
    per_b = s // ts
    tiles_per_k = b * per_b
    y_spec = lambda k: pl.BlockSpec((ts * SUBLANES, LANES), lambda i, j: (k * tiles_per_k + i * per_b + j, 0))
    return pl.pallas_call(
        _combine_kernel,
        grid=(b, per_b),
        in_specs=[pl.BlockSpec((None, ts, d), lambda i, j: (i, j, 0)),
                  pl.BlockSpec((None, 1, d), lambda i, j: (i, 0, 0)),
                  pl.BlockSpec((ts, LANES), lambda i, j: (i * per_b + j, 0))]
                 + [y_spec(k) for k in range(TOP_K)],
        out_specs=pl.BlockSpec((None, ts, d), lambda i, j: (i, j, 0)),
        out_shape=jax.ShapeDtypeStruct((b, s, d), F32),
        compiler_params=_params("arbitrary", "arbitrary"),
        name="moe_combine",
    )(x, gate.reshape(b, 1, d), weights, *([y] * TOP_K))


def _moe_ffn(x, h, gate, layer, w_router, b_router, w_gate, b_gate, w_up, b_up, w_down, b_down):
    b, s, d = x.shape
    t = b * s
    e = w_router.shape[1]
    tm = 512
    h2 = h.reshape(t * SUBLANES, LANES)
    idx_pad, wgt_pad, rank_pad, counts_pad = _route(h2, w_router, b_router)
    idx = idx_pad[:, :TOP_K]
    counts = counts_pad[0, :e]

    padded = ((counts + tm - 1) // tm) * tm
    ends = jnp.cumsum(padded)
    starts = ends - padded
    pos = starts[idx] + rank_pad[:, :TOP_K]
    n_slots = t * TOP_K + e * tm
    n_tiles = n_slots // tm
    tile_start = jnp.arange(n_tiles, dtype=jnp.int32) * tm
    tile_expert = jnp.sum((tile_start[:, None] >= ends[None, :]).astype(jnp.int32), axis=1)
    tile_expert = jnp.minimum(tile_expert, e - 1)
    n_valid = (ends[-1:] // tm).astype(jnp.int32)
    tile_first = jnp.concatenate([jnp.ones((1,), jnp.int32),
                                  (tile_expert[1:] != tile_expert[:-1]).astype(jnp.int32)])

    n_pairs = t * TOP_K
    slot_pair = jnp.full((n_slots,), -1, jnp.int32).at[pos.reshape(-1)].set(
        jnp.arange(n_pairs, dtype=jnp.int32), unique_indices=True)
    slot_id = jnp.arange(n_slots, dtype=jnp.int32)
    spare = n_pairs + ((slot_id // tm) % 2) * tm + slot_id % tm
    used = slot_pair >= 0
    pair_token, pair_k = slot_pair // TOP_K, slot_pair % TOP_K
    slot_token = (jnp.where(used, pair_token, 0) * SUBLANES).reshape(n_tiles, 1, tm)
    slot_dst = (jnp.where(used, pair_k * t + pair_token, spare) * SUBLANES).reshape(n_tiles, 1, tm)

    y = _moe_experts(h2, slot_token, slot_dst, n_pairs + 2 * tm, tile_expert, tile_first, n_valid, layer,
                     w_gate, b_gate, w_up, b_up, w_down, b_down, tm)
    return _moe_combine(x, gate, y, wgt_pad)


def kernel(x, c, mod_w, mod_b, mix_norm_g, ffn_norm_g, ab_w_in, ab_w_out, moba_q_gain, moba_k_gain,
           fox_w_in, fox_b_f, fox_w_out, fox_q_gain, fox_k_gain, router_w, router_b,
           exp_w_gate, exp_b_gate, exp_w_up, exp_b_up, exp_w_down, exp_b_down):
    b, s, d = x.shape
    depth = mod_w.shape[0]
    n_heads = d // HEAD_DIM
    n_moba = n_heads // 2
    n_sb = n_heads - n_moba
    mod = _adaln_mod(c, mod_w, mod_b)

    for layer in range(depth):
        sh1, sc1, g1, sh2, sc2, g2 = [mod[layer, :, i * d:(i + 1) * d] for i in range(N_MOD)]
        j = layer // 2
        h = _norm_mod(x, mix_norm_g[layer], sc1, sh1)
        if layer % 2 == 0:
            wa = n_moba * HEAD_DIM
            wb = n_sb * HEAD_DIM
            col_scale = jnp.concatenate([
                jnp.tile(moba_q_gain[j] * Q_PRESCALE, n_moba), jnp.tile(moba_k_gain[j], n_moba),
                jnp.ones((wa,), F32), jnp.full((wb,), Q_PRESCALE, F32), jnp.ones((2 * wb,), F32)])
            n_cols = col_scale.shape[0]
            proj = _in_proj(h.reshape(b * s, d), ab_w_in[j].astype(BF16), col_scale.reshape(1, n_cols), 2 * wa)
            proj = proj.reshape(b, s, n_cols)
            o_a = _moba_attention(proj, n_moba, 0)
            o_b = _sb_attention(proj, n_sb, 3 * n_moba)
            w_out = ab_w_out[j].astype(BF16)
            x = _out_proj_residual(x, g1, [(o_a, w_out[:wa]), (o_b, w_out[wa:])])
        else:
            w = n_heads * HEAD_DIM
            col_scale = jnp.concatenate([jnp.tile(fox_q_gain[j] * Q_PRESCALE, n_heads),
                                         jnp.tile(fox_k_gain[j], n_heads), jnp.ones((w,), F32)])
            proj = _in_proj(h.reshape(b * s, d), fox_w_in[j, :, :3 * w].astype(BF16),
                            col_scale.reshape(1, 3 * w), 2 * w)
            cum_f = _forget_cumsum(h, fox_w_in[j, :, 3 * w:], fox_b_f[j])
            o = _fox_attention(proj.reshape(b, s, 3 * w), cum_f, n_heads)
            x = _out_proj_residual(x, g1, [(o, fox_w_out[j].astype(BF16))])
        h = _norm_mod(x, ffn_norm_g[layer], sc2, sh2, packed=True)
        x = _moe_ffn(x, h, g2, layer, router_w[layer], router_b[layer], exp_w_gate, exp_b_gate,
                     exp_w_up, exp_b_up, exp_w_down, exp_b_down)
    return x
```

```python
import functools
import math

import jax
import jax.numpy as jnp
from jax import lax
from jax.experimental import pallas as pl
from jax.experimental.pallas import tpu as pltpu

F32 = jnp.float32
BF16 = jnp.bfloat16

HEAD_DIM = 128
MOBA_BLOCK = 256
MOBA_TOPK = 3
TOP_K = 4
SWIGLU_LIMIT = 7.0
SWIGLU_ALPHA = 1.702
RMS_EPS = 1e-5
N_MOD = 6
LOG2E = math.log2(math.e)
Q_PRESCALE = HEAD_DIM ** -0.5 * LOG2E
LANES = 128
SUBLANES = 8
BF16_ROWS = 16
MASKED = -1e30
SB_EXIT_BITS = 160.0
ROW_DMA_UNROLL = 16
VMEM_LIMIT_BYTES = 56 * 1024 * 1024


def _params(*semantics):
    return pltpu.CompilerParams(dimension_semantics=semantics, vmem_limit_bytes=VMEM_LIMIT_BYTES)


def _dot(a, b):
    return jnp.dot(a, b, preferred_element_type=F32)


def _dot_nt(a, b):
    return lax.dot_general(a, b, (((1,), (1,)), ((), ())), preferred_element_type=F32)


def _split_bf16(x):
    hi = x.astype(BF16)
    lo = (x - hi.astype(F32)).astype(BF16)
    return hi, lo


def _kv_block(ref, j, tk):
    return ref[pl.ds(pl.multiple_of(j * tk, tk), tk), :]


def _pack_bf16_pair(lo, hi):
    lo_bits = lax.bitcast_convert_type(lo.astype(BF16).astype(F32), jnp.uint32)
    hi_bits = lax.bitcast_convert_type(hi.astype(BF16).astype(F32), jnp.uint32)
    return lax.shift_right_logical(lo_bits, jnp.uint32(16)) | (hi_bits & jnp.uint32(0xFFFF0000))


def _unpack_bf16_pair(word):
    lo = lax.bitcast_convert_type(lax.shift_left(word, jnp.uint32(16)), F32)
    hi = lax.bitcast_convert_type(word & jnp.uint32(0xFFFF0000), F32)
    return lo, hi


TOKEN_DIM = 2 * SUBLANES * LANES


def _store_token_tiles(ref, y, lead=()):
    n = y.shape[0]
    half = TOKEN_DIM // 2
    for s in range(SUBLANES):
        lo = y[:, s * LANES:(s + 1) * LANES]
        hi = y[:, half + s * LANES:half + (s + 1) * LANES]
        ref[lead + (pl.ds(s, n, stride=SUBLANES), slice(None))] = _pack_bf16_pair(lo, hi)


def _load_token_tiles(ref, n, lead=()):
    los, his = [], []
    for s in range(SUBLANES):
        lo, hi = _unpack_bf16_pair(ref[lead + (pl.ds(s, n, stride=SUBLANES), slice(None))])
        los.append(lo)
        his.append(hi)
    return los + his


def _load_token_tiles_bf16(ref, n, lead=()):
    return jnp.concatenate([c.astype(BF16) for c in _load_token_tiles(ref, n, lead)], axis=1)


def _mod_kernel(c_ref, w_ref, b_ref, o_ref):
    c = c_ref[...]
    c_act = c / (1.0 + jnp.exp(-c))
    o_ref[...] = _dot(c_act, w_ref[...]) + b_ref[...]


def _adaln_mod(c, mod_w, mod_b):
    depth, d, n = mod_w.shape
    b = c.shape[0]
    bp = -(-b // SUBLANES) * SUBLANES
    c_pad = jnp.zeros((bp, d), F32).at[:b].set(c)
    tn = min(n, 1024)
    out = pl.pallas_call(
        _mod_kernel,
        grid=(depth, n // tn),
        in_specs=[pl.BlockSpec((bp, d), lambda l, j: (0, 0)),
                  pl.BlockSpec((None, d, tn), lambda l, j: (l, 0, j)),
                  pl.BlockSpec((None, 1, tn), lambda l, j: (l, 0, j))],
        out_specs=pl.BlockSpec((None, bp, tn), lambda l, j: (l, 0, j)),
        out_shape=jax.ShapeDtypeStruct((depth, bp, n), F32),
        compiler_params=_params("arbitrary", "arbitrary"),
        name="adaln_mod",
    )(c_pad, mod_w, mod_b.reshape(depth, 1, n))
    return out[:, :b]


def _norm_kernel(x_ref, g_ref, sc_ref, sh_ref, o_ref, *, packed):
    x = x_ref[...]
    ms = jnp.mean(x * x, axis=-1, keepdims=True)
    y = x * lax.rsqrt(ms + RMS_EPS)
    y = (y * g_ref[...]) * (1.0 + sc_ref[...]) + sh_ref[...]
    if packed:
        _store_token_tiles(o_ref, y)
    else:
        o_ref[...] = y.astype(o_ref.dtype)


def _norm_mod(x, gain, scale, shift, packed=False):
    b, s, d = x.shape
    ts = min(s, 512)
    if packed:
        assert d == TOKEN_DIM
        out_block, out_shape = (None, ts * SUBLANES, LANES), (b, s * SUBLANES, LANES)
    else:
        out_block, out_shape = (None, ts, d), (b, s, d)
    return pl.pallas_call(
        functools.partial(_norm_kernel, packed=packed),
        grid=(b, s // ts),
        in_specs=[pl.BlockSpec((None, ts, d), lambda i, j: (i, j, 0)),
                  pl.BlockSpec((1, d), lambda i, j: (0, 0)),
                  pl.BlockSpec((None, 1, d), lambda i, j: (i, 0, 0)),
                  pl.BlockSpec((None, 1, d), lambda i, j: (i, 0, 0))],
        out_specs=pl.BlockSpec(out_block, lambda i, j: (i, j, 0)),
        out_shape=jax.ShapeDtypeStruct(out_shape, jnp.uint32 if packed else BF16),
        compiler_params=_params("arbitrary", "arbitrary"),
        name="norm_mod",
    )(x, gain.reshape(1, d), scale.reshape(b, 1, d), shift.reshape(b, 1, d))


def _inproj_kernel(h_ref, w_ref, gain_ref, o_ref, *, n_norm_tiles):
    acc = _dot(h_ref[...], w_ref[...])
    j = pl.program_id(0)

    @pl.when(j < n_norm_tiles)
    def _():
        for g in range(acc.shape[1] // HEAD_DIM):
            sl = slice(g * HEAD_DIM, (g + 1) * HEAD_DIM)
            blk = acc[:, sl]
            ms = jnp.mean(blk * blk, axis=-1, keepdims=True)
            o_ref[:, sl] = (blk * lax.rsqrt(ms + RMS_EPS) * gain_ref[:, sl]).astype(o_ref.dtype)

    @pl.when(j >= n_norm_tiles)
    def _():
        o_ref[...] = (acc * gain_ref[...]).astype(o_ref.dtype)


def _in_proj(h, w, col_scale, n_norm_cols):
    t, d = h.shape
    n = w.shape[1]
    tm = min(t, 1024)
    tn = min(n, 1024)
    assert n_norm_cols % tn == 0 and n % tn == 0 and t % tm == 0
    return pl.pallas_call(
        functools.partial(_inproj_kernel, n_norm_tiles=n_norm_cols // tn),
        grid=(n // tn, t // tm),
        in_specs=[pl.BlockSpec((tm, d), lambda j, i: (i, 0)),
                  pl.BlockSpec((d, tn), lambda j, i: (0, j)),
                  pl.BlockSpec((1, tn), lambda j, i: (0, j))],
        out_specs=pl.BlockSpec((tm, tn), lambda j, i: (i, j)),
        out_shape=jax.ShapeDtypeStruct((t, n), BF16),
        compiler_params=_params("arbitrary", "arbitrary"),
        name="in_proj",
    )(h, w, col_scale)


def _fgate_kernel(h_ref, wf_ref, bf_ref, o_ref, carry_ref):
    @pl.when(pl.program_id(1) == 0)
    def _():
        carry_ref[...] = jnp.zeros_like(carry_ref)

    ts = h_ref.shape[0]
    logit = _dot(h_ref[...], wf_ref[...]) + bf_ref[...]
    log_f = jnp.minimum(logit, 0.0) - jnp.log(1.0 + jnp.exp(-jnp.abs(logit)))
    row = lax.broadcasted_iota(jnp.int32, (ts, ts), 0)
    col = lax.broadcasted_iota(jnp.int32, (ts, ts), 1)
    lower = jnp.where(col <= row, 1.0, 0.0).astype(BF16)
    hi, lo = _split_bf16(log_f)
    cum = _dot(lower, hi) + _dot(lower, lo) + carry_ref[0:1, :]
    o_ref[...] = cum
    carry_ref[...] = jnp.broadcast_to(cum[ts - 1:ts, :], carry_ref.shape)


def _forget_cumsum(h, w_f, b_f):
    b, s, d = h.shape
    nh = w_f.shape[1]
    assert nh <= LANES
    ts = min(s, 512)
    w_pad = jnp.zeros((d, LANES), BF16).at[:, :nh].set(w_f.astype(BF16))
    b_pad = jnp.zeros((1, LANES), F32).at[0, :nh].set(b_f)
    return pl.pallas_call(
        _fgate_kernel,
        grid=(b, s // ts),
        in_specs=[pl.BlockSpec((None, ts, d), lambda i, j: (i, j, 0)),
                  pl.BlockSpec((d, LANES), lambda i, j: (0, 0)),
                  pl.BlockSpec((1, LANES), lambda i, j: (0, 0))],
        out_specs=pl.BlockSpec((None, ts, LANES), lambda i, j: (i, j, 0)),
        out_shape=jax.ShapeDtypeStruct((b, s, LANES), F32),
        scratch_shapes=[pltpu.VMEM((SUBLANES, LANES), F32)],
        compiler_params=_params("arbitrary", "arbitrary"),
        name="forget_cumsum",
    )(h, w_pad, b_pad)


VT_ROWS = HEAD_DIM + BF16_ROWS


def _store_v_transposed(v_ref, vt_ref, c, tile):
    vb = _kv_block(v_ref, c, tile).astype(F32)
    vt_ref[c, 0:HEAD_DIM, :] = vb.T.astype(BF16)
    vt_ref[c, HEAD_DIM:VT_ROWS, :] = jnp.ones((BF16_ROWS, tile), BF16)


def _softmax_accumulate(s, vt_blk, m_ref, acc_ref):
    m_old = m_ref[...]
    m_new = jnp.maximum(m_old, jnp.max(s, axis=0, keepdims=True))
    alpha = jnp.exp2(m_old - m_new)
    p = jnp.exp2((s - m_new).astype(BF16))
    m_ref[...] = m_new
    acc_ref[...] = alpha * acc_ref[...] + _dot(vt_blk, p)


def _softmax_finish(acc_ref, o_ref, qi, tile):
    acc = acc_ref[...]
    out_t = acc[0:HEAD_DIM, :] / acc[HEAD_DIM:HEAD_DIM + 1, :]
    o_ref[pl.ds(pl.multiple_of(qi * tile, tile), tile), :] = out_t.T.astype(o_ref.dtype)


def _flash_sweep(qi, n_past, produce, s0_ref, s1_ref, vt_ref, m_ref, acc_ref, tile):
    m_ref[...] = jnp.full(m_ref.shape, MASKED, F32)
    acc_ref[...] = jnp.zeros_like(acc_ref)
    key = lax.broadcasted_iota(jnp.int32, (tile, tile), 0)
    qry = lax.broadcasted_iota(jnp.int32, (tile, tile), 1)
    s0_ref[...] = jnp.where(key <= qry, produce(qi), MASKED)
    n_pos = n_past + 1
    farthest = qi - n_past

    def pair(i):
        blk = qi - 2 * i
        s1_ref[...] = produce(blk - 1)
        _softmax_accumulate(s0_ref[...], vt_ref[blk], m_ref, acc_ref)
        s0_ref[...] = produce(jnp.maximum(blk - 2, farthest))
        _softmax_accumulate(s1_ref[...], vt_ref[blk - 1], m_ref, acc_ref)

    def two_pairs(i, carry):
        pair(2 * i)
        pair(2 * i + 1)
        return carry

    n_pairs = n_pos // 2
    lax.fori_loop(0, n_pairs // 2, two_pairs, 0)

    @pl.when(lax.rem(n_pairs, 2) == 1)
    def _():
        pair(n_pairs - 1)

    @pl.when(lax.rem(n_pos, 2) == 1)
    def _():
        _softmax_accumulate(s0_ref[...], vt_ref[farthest], m_ref, acc_ref)


SKIP_BITS = 162.0


def _logit_spread_bound(q, kmax_sq_ref):
    q32 = q.astype(F32)
    qmax_sq = jnp.max(jnp.sum(q32 * q32, axis=-1, keepdims=True), axis=0, keepdims=True)
    return 2.0 * jnp.sqrt(qmax_sq * kmax_sq_ref[0:1, 0:1])


def _update_kmax_sq(kmax_sq_ref, kb):
    blk_max = jnp.max(jnp.sum(kb * kb, axis=-1, keepdims=True), axis=0, keepdims=True)
    kmax_sq_ref[...] = jnp.maximum(kmax_sq_ref[...], jnp.broadcast_to(blk_max, kmax_sq_ref.shape))


def _fox_kernel(q_ref, k_ref, v_ref, f_ref, o_ref, vt_ref, frep_ref, fend_ref, kmax_sq_ref, s0_ref, s1_ref, m_ref,
                acc_ref, *, tile, n_kv):
    head = pl.program_id(1)
    lane = lax.broadcasted_iota(jnp.int32, (tile, LANES), 1)
    fend_ref[...] = jnp.zeros_like(fend_ref)
    kmax_sq_ref[...] = jnp.zeros_like(kmax_sq_ref)

    def prep(c, carry):
        _store_v_transposed(v_ref, vt_ref, c, tile)
        _update_kmax_sq(kmax_sq_ref, _kv_block(k_ref, c, tile).astype(F32))
        f_blk = _kv_block(f_ref, c, tile)
        f_col = jnp.sum(jnp.where(lane == head, f_blk, 0.0), axis=-1, keepdims=True)
        f_rep = jnp.broadcast_to(f_col * LOG2E, (tile, LANES))
        frep_ref[pl.ds(pl.multiple_of(c * tile, tile), tile), :] = f_rep
        fend_ref[pl.ds(c, 1), :] = f_rep[tile - 1:tile, :]
        return carry

    lax.fori_loop(0, n_kv, prep, 0)

    def query_tile(qi, carry):
        qi = jnp.asarray(qi, jnp.int32)
        q = _kv_block(q_ref, qi, tile)
        f_first = frep_ref[pl.ds(pl.multiple_of(qi * tile, tile), 1), :]
        gap = fend_ref[...] - f_first
        block_id = lax.broadcasted_iota(jnp.int32, fend_ref.shape, 0)
        needed = jnp.logical_and(block_id < qi, gap <= _logit_spread_bound(q, kmax_sq_ref) + SKIP_BITS)
        n_past = jnp.sum(jnp.where(needed, 1.0, 0.0)[:, 0:1]).astype(jnp.int32)

        def produce(j):
            f_rep = _kv_block(frep_ref, j, tile)
            return _dot_nt(_kv_block(k_ref, j, tile), q) - jnp.concatenate([f_rep] * (tile // LANES), axis=1)

        _flash_sweep(qi, n_past, produce, s0_ref, s1_ref, vt_ref, m_ref, acc_ref, tile)
        _softmax_finish(acc_ref, o_ref, qi, tile)
        return carry

    lax.fori_loop(0, n_kv, query_tile, 0)


def _fox_attention(proj, cum_f, n_heads):
    b, s, _ = proj.shape
    tile = min(s, 512)
    n_kv = s // tile
    return pl.pallas_call(
        functools.partial(_fox_kernel, tile=tile, n_kv=n_kv),
        grid=(b, n_heads),
        in_specs=[pl.BlockSpec((None, s, HEAD_DIM), lambda i, h: (i, 0, h)),
                  pl.BlockSpec((None, s, HEAD_DIM), lambda i, h: (i, 0, n_heads + h)),
                  pl.BlockSpec((None, s, HEAD_DIM), lambda i, h: (i, 0, 2 * n_heads + h)),
                  pl.BlockSpec((None, s, LANES), lambda i, h: (i, 0, 0))],
        out_specs=pl.BlockSpec((None, s, HEAD_DIM), lambda i, h: (i, 0, h)),
        out_shape=jax.ShapeDtypeStruct((b, s, n_heads * HEAD_DIM), BF16),
        scratch_shapes=[pltpu.VMEM((n_kv, VT_ROWS, tile), BF16), pltpu.VMEM((s, LANES), F32),
                        pltpu.VMEM((-(-n_kv // SUBLANES) * SUBLANES, LANES), F32),
                        pltpu.VMEM((SUBLANES, LANES), F32),
                        pltpu.VMEM((tile, tile), F32), pltpu.VMEM((tile, tile), F32),
                        pltpu.VMEM((1, tile), F32),
                        pltpu.VMEM((VT_ROWS, tile), F32)],
        compiler_params=_params("arbitrary", "arbitrary"),
        name="fox_attention",
    )(proj, proj, proj, cum_f)


def _moba_kernel(q_ref, k_ref, v_ref, slope_ref, o_ref, kmean_ref, kmax_sq_ref, vt_ref, sel_ref, alibi_ref,
                 s0_ref, s1_ref, m_ref, acc_ref, *, tile, n_blocks):
    blk = MOBA_BLOCK
    per_tile = tile // blk
    kmean_ref[...] = jnp.zeros_like(kmean_ref)
    kmax_sq_ref[...] = jnp.zeros_like(kmax_sq_ref)

    def block_mean(n, carry):
        kb = _kv_block(k_ref, n, blk).astype(F32)
        kmean_ref[pl.ds(n, 1), :] = jnp.mean(kb, axis=0, keepdims=True)
        _update_kmax_sq(kmax_sq_ref, kb)
        return carry

    def transpose_v(c, carry):
        _store_v_transposed(v_ref, vt_ref, c, tile)
        return carry

    lax.fori_loop(0, n_blocks, block_mean, 0)
    lax.fori_loop(0, n_blocks // per_tile, transpose_v, 0)

    slope = slope_ref[...] * LOG2E
    alibi_ref[...] = slope * lax.broadcasted_iota(jnp.int32, (tile, tile), 0).astype(F32)
    block_id = lax.broadcasted_iota(jnp.int32, (LANES, tile), 0).astype(F32)
    qry_block = jnp.floor(lax.broadcasted_iota(jnp.int32, (LANES, tile), 1).astype(F32) * (1.0 / blk))

    def query_tile(qi, carry):
        qi = jnp.asarray(qi, jnp.int32)
        q = _kv_block(q_ref, qi, tile)
        own = jnp.asarray(qi * per_tile, F32) + qry_block
        gate = _dot_nt(kmean_ref[...].astype(BF16), q)
        gate = jnp.where(block_id < own, gate, -jnp.inf)
        sel = jnp.where(block_id == own, 1.0, 0.0)
        for slot in range(MOBA_TOPK):
            best = jnp.max(gate, axis=0, keepdims=True)
            idx = jnp.min(jnp.where(gate == best, block_id, float(LANES)), axis=0, keepdims=True)
            hit = block_id == idx
            slot_ok = jnp.where(float(slot) < own, 1.0, 0.0)
            sel = jnp.maximum(sel, jnp.where(hit, slot_ok, 0.0))
            gate = jnp.where(hit, -jnp.inf, gate)
        sel_ref[...] = jnp.where(sel > 0.5, 0.0, MASKED)

        def produce(j):
            shift = slope * jnp.asarray((j - qi) * tile, F32)
            per_query = jnp.concatenate(
                [jnp.broadcast_to(sel_ref[pl.ds(j * per_tile + r, 1), :] + shift, (blk, tile))
                 for r in range(per_tile)], axis=0)
            return (_dot_nt(_kv_block(k_ref, j, tile), q) + alibi_ref[...]) + per_query

        reach = (_logit_spread_bound(q, kmax_sq_ref) + SKIP_BITS) / slope
        tiles_back = jnp.floor((reach - 1.0) * (1.0 / tile)) + 2.0
        n_past = jnp.minimum(jnp.minimum(jnp.max(tiles_back), float(n_blocks)).astype(jnp.int32), qi)

        _flash_sweep(qi, n_past, produce, s0_ref, s1_ref, vt_ref, m_ref, acc_ref, tile)
        _softmax_finish(acc_ref, o_ref, qi, tile)
        return carry

    lax.fori_loop(0, n_blocks // per_tile, query_tile, 0)


def _moba_attention(proj, n_heads, col_block):
    b, s, _ = proj.shape
    blk = MOBA_BLOCK
    n_blocks = s // blk
    tile = min(s, 2 * blk)
    assert s % tile == 0 and tile % blk == 0 and n_blocks <= LANES
    slopes = jnp.asarray([2.0 ** (-8.0 * (i + 1) / n_heads) for i in range(n_heads)], F32)
    slope_rows = jnp.broadcast_to(slopes[:, None, None], (n_heads, 1, tile))
    return pl.pallas_call(
        functools.partial(_moba_kernel, tile=tile, n_blocks=n_blocks),
        grid=(b, n_heads),
        in_specs=[pl.BlockSpec((None, s, HEAD_DIM), lambda i, h: (i, 0, col_block + h)),
                  pl.BlockSpec((None, s, HEAD_DIM), lambda i, h: (i, 0, col_block + n_heads + h)),
                  pl.BlockSpec((None, s, HEAD_DIM), lambda i, h: (i, 0, col_block + 2 * n_heads + h)),
                  pl.BlockSpec((None, 1, tile), lambda i, h: (h, 0, 0))],
        out_specs=pl.BlockSpec((None, s, HEAD_DIM), lambda i, h: (i, 0, h)),
        out_shape=jax.ShapeDtypeStruct((b, s, n_heads * HEAD_DIM), BF16),
        scratch_shapes=[pltpu.VMEM((LANES, HEAD_DIM), F32), pltpu.VMEM((SUBLANES, LANES), F32),
                        pltpu.VMEM((s // tile, VT_ROWS, tile), BF16),
                        pltpu.VMEM((LANES, tile), F32), pltpu.VMEM((tile, tile), F32),
                        pltpu.VMEM((tile, tile), F32), pltpu.VMEM((tile, tile), F32),
                        pltpu.VMEM((1, tile), F32), pltpu.VMEM((VT_ROWS, tile), F32)],
        compiler_params=_params("arbitrary", "arbitrary"),
        name="moba_attention",
    )(proj, proj, proj, slope_rows)


def _sb_kernel(q_ref, k_ref, v_ref, o_ref, after_ref, decay_ref, acc_ref, *, tile, n_q):
    row = lax.broadcasted_iota(jnp.int32, (tile, tile), 0)
    col = lax.broadcasted_iota(jnp.int32, (tile, tile), 1)
    after_ref[...] = jnp.where(row > col, 1.0, 0.0).astype(BF16)

    def query_tile(qi, carry):
        qi = jnp.asarray(qi, jnp.int32)
        q = _kv_block(q_ref, qi, tile)
        causal = col < row

        def logits_and_drop(j, masked):
            z = _dot_nt(q, _kv_block(k_ref, j, tile))
            drop = jnp.maximum(z, 0.0) + jnp.log2(1.0 + jnp.exp2(-jnp.abs(z)))
            if masked:
                drop = jnp.where(causal, drop, 0.0)
            return z, drop

        def weighted_values(j, z, drop, decay_right, masked):
            hi, lo = _split_bf16(drop)
            after = after_ref[...]
            decay = _dot(hi, after) + _dot(lo, after) + decay_right
            a = jnp.exp2(z - drop - decay)
            if masked:
                a = jnp.where(causal, a, 0.0)
            return _dot(a.astype(BF16), _kv_block(v_ref, j, tile))

        @pl.when(qi == 0)
        def _():
            z, drop = logits_and_drop(qi, True)
            acc_ref[...] = weighted_values(qi, z, drop, 0.0, True)
            decay_ref[...] = jnp.sum(drop, axis=-1, keepdims=True)

        @pl.when(qi > 0)
        def _():
            z_d, drop_d = logits_and_drop(qi, True)
            z_p, drop_p = logits_and_drop(qi - 1, False)
            decay_d = jnp.sum(drop_d, axis=-1, keepdims=True)
            acc_ref[...] = (weighted_values(qi, z_d, drop_d, 0.0, True)
                            + weighted_values(qi - 1, z_p, drop_p, decay_d, False))
            decay_ref[...] = decay_d + jnp.sum(drop_p, axis=-1, keepdims=True)

        def cond(state):
            jj, least = state
            return jnp.logical_and(jj < qi, least < SB_EXIT_BITS)

        def body(state):
            jj, _ = state
            j = qi - 1 - jj
            z, drop = logits_and_drop(j, False)
            acc_ref[...] += weighted_values(j, z, drop, decay_ref[...], False)
            decay_ref[...] += jnp.sum(drop, axis=-1, keepdims=True)
            return jj + 1, jnp.min(decay_ref[...])

        lax.while_loop(cond, body, (jnp.int32(1), jnp.min(decay_ref[...])))
        o_ref[pl.ds(pl.multiple_of(qi * tile, tile), tile), :] = acc_ref[...].astype(o_ref.dtype)
        return carry

    lax.fori_loop(0, n_q, query_tile, 0)


def _sb_attention(proj, n_heads, col_block):
    b, s, _ = proj.shape
    tile = min(s, 256)
    return pl.pallas_call(
        functools.partial(_sb_kernel, tile=tile, n_q=s // tile),
        grid=(b, n_heads),
        in_specs=[pl.BlockSpec((None, s, HEAD_DIM), lambda i, h: (i, 0, col_block + h)),
                  pl.BlockSpec((None, s, HEAD_DIM), lambda i, h: (i, 0, col_block + n_heads + h)),
                  pl.BlockSpec((None, s, HEAD_DIM), lambda i, h: (i, 0, col_block + 2 * n_heads + h))],
        out_specs=pl.BlockSpec((None, s, HEAD_DIM), lambda i, h: (i, 0, h)),
        out_shape=jax.ShapeDtypeStruct((b, s, n_heads * HEAD_DIM), BF16),
        scratch_shapes=[pltpu.VMEM((tile, tile), BF16), pltpu.VMEM((tile, 1), F32),
                        pltpu.VMEM((tile, HEAD_DIM), F32)],
        compiler_params=_params("arbitrary", "arbitrary"),
        name="sb_attention",
    )(proj, proj, proj)


def _outproj_kernel(*refs, n_parts):
    x_ref, gate_ref = refs[0], refs[1]
    parts = refs[2:2 + 2 * n_parts]
    o_ref = refs[2 + 2 * n_parts]
    y = _dot(parts[0][...], parts[1][...])
    for p in range(1, n_parts):
        y += _dot(parts[2 * p][...], parts[2 * p + 1][...])
    o_ref[...] = x_ref[...] + gate_ref[...] * y


def _out_proj_residual(x, gate, parts):
    b, s, d = x.shape
    ts = min(s, 512)
    in_specs = [pl.BlockSpec((None, ts, d), lambda i, j: (i, j, 0)),
                pl.BlockSpec((None, 1, d), lambda i, j: (i, 0, 0))]
    args = [x, gate.reshape(b, 1, d)]
    for a, w in parts:
        kp = a.shape[-1]
        in_specs += [pl.BlockSpec((None, ts, kp), lambda i, j: (i, j, 0)),
                     pl.BlockSpec((kp, d), lambda i, j: (0, 0))]
        args += [a, w]
    return pl.pallas_call(
        functools.partial(_outproj_kernel, n_parts=len(parts)),
        grid=(b, s // ts),
        in_specs=in_specs,
        out_specs=pl.BlockSpec((None, ts, d), lambda i, j: (i, j, 0)),
        out_shape=jax.ShapeDtypeStruct((b, s, d), F32),
        compiler_params=_params("arbitrary", "arbitrary"),
        name="out_proj",
    )(*args)


def _router_kernel(h_ref, w_ref, b_ref, idx_ref, wgt_ref, rank_ref, cnt_ref, carry_ref):
    @pl.when(pl.program_id(0) == 0)
    def _():
        carry_ref[...] = jnp.zeros_like(carry_ref)

    tm = h_ref.shape[0] // SUBLANES
    logits = _dot(_load_token_tiles_bf16(h_ref, tm), w_ref[...]) + b_ref[...]
    lane = lax.broadcasted_iota(jnp.int32, (tm, LANES), 1).astype(F32)
    vals, idxs = [], []
    onehot = jnp.zeros((tm, LANES), F32)
    for _ in range(TOP_K):
        best = jnp.max(logits, axis=-1, keepdims=True)
        idx = jnp.min(jnp.where(logits == best, lane, float(LANES)), axis=-1, keepdims=True)
        hit = lane == idx
        onehot = jnp.where(hit, 1.0, onehot)
        logits = jnp.where(hit, -jnp.inf, logits)
        vals.append(best)
        idxs.append(idx)
    exps = [jnp.exp(v - vals[0]) for v in vals]
    denom = exps[0]
    for e in exps[1:]:
        denom = denom + e

    row = lax.broadcasted_iota(jnp.int32, (tm, tm), 0)
    col = lax.broadcasted_iota(jnp.int32, (tm, tm), 1)
    before = jnp.where(col < row, 1.0, 0.0).astype(BF16)
    prior = _dot(before, onehot.astype(BF16)) + carry_ref[0:1, :]

    idx_out = jnp.zeros((tm, LANES), F32)
    wgt_out = jnp.zeros((tm, LANES), F32)
    rank_out = jnp.zeros((tm, LANES), F32)
    for k in range(TOP_K):
        rank_k = jnp.sum(jnp.where(lane == idxs[k], prior, 0.0), axis=-1, keepdims=True)
        slot = lane == float(k)
        idx_out = jnp.where(slot, idxs[k], idx_out)
        wgt_out = jnp.where(slot, exps[k] / denom, wgt_out)
        rank_out = jnp.where(slot, rank_k, rank_out)
    idx_ref[...] = idx_out.astype(jnp.int32)
    wgt_ref[...] = wgt_out
    rank_ref[...] = rank_out.astype(jnp.int32)
    counts = carry_ref[...] + jnp.sum(onehot, axis=0, keepdims=True)
    carry_ref[...] = counts
    cnt_ref[...] = counts.astype(jnp.int32)


def _route(h, w_router, b_router):
    t = h.shape[0] // SUBLANES
    d = TOKEN_DIM
    e = w_router.shape[1]
    assert e <= LANES
    tm = min(t, 512)
    w_pad = jnp.zeros((d, LANES), BF16).at[:, :e].set(w_router.astype(BF16))
    b_pad = jnp.full((1, LANES), MASKED, F32).at[0, :e].set(b_router)
    tok_spec = pl.BlockSpec((tm, LANES), lambda i: (i, 0))
    return pl.pallas_call(
        _router_kernel,
        grid=(t // tm,),
        in_specs=[pl.BlockSpec((tm * SUBLANES, LANES), lambda i: (i, 0)),
                  pl.BlockSpec((d, LANES), lambda i: (0, 0)),
                  pl.BlockSpec((1, LANES), lambda i: (0, 0))],
        out_specs=[tok_spec, tok_spec, tok_spec, pl.BlockSpec((SUBLANES, LANES), lambda i: (0, 0))],
        out_shape=[jax.ShapeDtypeStruct((t, LANES), jnp.int32), jax.ShapeDtypeStruct((t, LANES), F32),
                   jax.ShapeDtypeStruct((t, LANES), jnp.int32),
                   jax.ShapeDtypeStruct((SUBLANES, LANES), jnp.int32)],
        scratch_shapes=[pltpu.VMEM((SUBLANES, LANES), F32)],
        compiler_params=_params("arbitrary"),
        name="moe_router",
    )(h, w_pad, b_pad)


def _start_rows(n_rows, copy_of_row):
    def body(p, carry):
        copy_of_row(2 * p).start(priority=0)
        copy_of_row(2 * p + 1).start(priority=1)
        return carry

    lax.fori_loop(0, n_rows // 2, body, 0, unroll=ROW_DMA_UNROLL // 2)


def _wait_rows(n_rows, copy_of_row):
    def body(r, carry):
        copy_of_row(r).wait()
        return carry

    lax.fori_loop(0, n_rows, body, 0, unroll=ROW_DMA_UNROLL)


def _moe_up_kernel(te_ref, first_ref, nvalid_ref, tok_ref, tok_next_ref, h_hbm, wg_ref, bg_ref, wu_ref, bu_ref,
                   o_ref, xbuf, sem, wg_bf, wu_bf):
    i = pl.program_id(0)
    slot = lax.rem(i, 2)
    n_valid = nvalid_ref[0]
    tm = xbuf.shape[1] // SUBLANES

    def row_copy(tok_smem, buf, r):
        src = h_hbm.at[pl.ds(pl.multiple_of(tok_smem[0, r], SUBLANES), SUBLANES)]
        dst = xbuf.at[buf, pl.ds(pl.multiple_of(r * SUBLANES, SUBLANES), SUBLANES)]
        return pltpu.make_async_copy(src, dst, sem.at[buf])

    @pl.when(jnp.logical_and(i == 0, n_valid > 0))
    def _():
        _start_rows(tm, lambda r: row_copy(tok_ref, 0, r))

    @pl.when(i + 1 < n_valid)
    def _():
        _start_rows(tm, lambda r: row_copy(tok_next_ref, 1 - slot, r))

    @pl.when(first_ref[i] == 1)
    def _():
        wg_bf[...] = wg_ref[...].astype(BF16)
        wu_bf[...] = wu_ref[...].astype(BF16)

    @pl.when(i < n_valid)
    def _():
        _wait_rows(tm, lambda r: row_copy(tok_ref, slot, r))
        x = _load_token_tiles_bf16(xbuf, tm, lead=(slot,))
        g = jnp.minimum(_dot(x, wg_bf[...]) + bg_ref[...], SWIGLU_LIMIT)
        u = jnp.clip(_dot(x, wu_bf[...]) + bu_ref[...], -SWIGLU_LIMIT, SWIGLU_LIMIT)
        act = (u + 1.0) * (g / (1.0 + jnp.exp(-SWIGLU_ALPHA * g)))
        o_ref[...] = act.astype(o_ref.dtype)

    @pl.when(i >= n_valid)
    def _():
        o_ref[...] = jnp.zeros_like(o_ref)


def _moe_down_kernel(te_ref, first_ref, nvalid_ref, dst_ref, dst_prev_ref, a_ref, wd_ref, bd_ref, y_hbm,
                     obuf, sem, wd_bf, *, n_tiles, spare_row0):
    i = pl.program_id(0)
    slot = lax.rem(i, 2)
    n_valid = nvalid_ref[0]
    tm = obuf.shape[1] // SUBLANES

    def row_copy(dst_smem, buf, r):
        src = obuf.at[buf, pl.ds(pl.multiple_of(r * SUBLANES, SUBLANES), SUBLANES)]
        dst = y_hbm.at[pl.ds(pl.multiple_of(dst_smem[0, r], SUBLANES), SUBLANES)]
        return pltpu.make_async_copy(src, dst, sem.at[buf])

    @pl.when(i == 0)
    def _():
        obuf[...] = jnp.zeros_like(obuf)
        rows = tm * SUBLANES
        fills = [pltpu.make_async_copy(obuf.at[half], y_hbm.at[pl.ds(spare_row0 + half * rows, rows)], sem.at[half])
                 for half in range(2)]
        for fill in fills:
            fill.start()
        for fill in fills:
            fill.wait()

    @pl.when(first_ref[i] == 1)
    def _():
        wd_bf[...] = wd_ref[...].astype(BF16)

    @pl.when(i < n_valid)
    def _():
        _store_token_tiles(obuf, _dot(a_ref[...], wd_bf[...]) + bd_ref[...], lead=(slot,))
        _start_rows(tm, lambda r: row_copy(dst_ref, slot, r))

    @pl.when(jnp.logical_and(i >= 1, i - 1 < n_valid))
    def _():
        _wait_rows(tm, lambda r: row_copy(dst_prev_ref, 1 - slot, r))

    @pl.when(jnp.logical_and(i == n_tiles - 1, i < n_valid))
    def _():
        _wait_rows(tm, lambda r: row_copy(dst_ref, slot, r))


def _expert_spec(layer, shape):
    return pl.BlockSpec((None, None) + shape, lambda i, te, first, nvalid: (layer, te[i], 0, 0))


def _moe_experts(h_packed, slot_token, slot_dst, n_out_rows, tile_expert, tile_first, n_valid, layer,
                 w_gate, b_gate, w_up, b_up, w_down, b_down, tm):
    d = TOKEN_DIM
    buf_shape = (2, tm * SUBLANES, LANES)
    depth, e, _, f = w_gate.shape
    n_tiles = slot_token.shape[0]
    last = n_tiles - 1
    smem_rows = lambda index: pl.BlockSpec((None, 1, tm), index, memory_space=pltpu.SMEM)
    row_spec = lambda width: pl.BlockSpec((tm, width), lambda i, te, first, nvalid: (i, 0))
    act = pl.pallas_call(
        _moe_up_kernel,
        grid_spec=pltpu.PrefetchScalarGridSpec(
            num_scalar_prefetch=3, grid=(n_tiles,),
            in_specs=[smem_rows(lambda i, te, first, nvalid: (i, 0, 0)),
                      smem_rows(lambda i, te, first, nvalid: (jnp.minimum(i + 1, last), 0, 0)),
                      pl.BlockSpec(memory_space=pl.ANY),
                      _expert_spec(layer, (d, f)), _expert_spec(layer, (1, f)),
                      _expert_spec(layer, (d, f)), _expert_spec(layer, (1, f))],
            out_specs=row_spec(f),
            scratch_shapes=[pltpu.VMEM(buf_shape, jnp.uint32), pltpu.SemaphoreType.DMA((2,)),
                            pltpu.VMEM((d, f), BF16), pltpu.VMEM((d, f), BF16)]),
        out_shape=jax.ShapeDtypeStruct((n_tiles * tm, f), BF16),
        compiler_params=_params("arbitrary"),
        name="moe_up",
    )(tile_expert, tile_first, n_valid, slot_token, slot_token, h_packed,
      w_gate, b_gate.reshape(depth, e, 1, f), w_up, b_up.reshape(depth, e, 1, f))
    return pl.pallas_call(
        functools.partial(_moe_down_kernel, n_tiles=n_tiles, spare_row0=(n_out_rows - 2 * tm) * SUBLANES),
        grid_spec=pltpu.PrefetchScalarGridSpec(
            num_scalar_prefetch=3, grid=(n_tiles,),
            in_specs=[smem_rows(lambda i, te, first, nvalid: (i, 0, 0)),
                      smem_rows(lambda i, te, first, nvalid: (jnp.maximum(i - 1, 0), 0, 0)),
                      row_spec(f), _expert_spec(layer, (f, d)), _expert_spec(layer, (1, d))],
            out_specs=pl.BlockSpec(memory_space=pl.ANY),
            scratch_shapes=[pltpu.VMEM(buf_shape, jnp.uint32), pltpu.SemaphoreType.DMA((2,)),
                            pltpu.VMEM((f, d), BF16)]),
        out_shape=jax.ShapeDtypeStruct((n_out_rows * SUBLANES, LANES), jnp.uint32),
        compiler_params=_params("arbitrary"),
        name="moe_down",
    )(tile_expert, tile_first, n_valid, slot_dst, slot_dst, act, w_down, b_down.reshape(depth, e, 1, d))


def _combine_kernel(*refs):
    x_ref, gate_ref, w_ref = refs[0], refs[1], refs[2]
    y_refs = refs[3:3 + TOP_K]
    o_ref = refs[3 + TOP_K]
    ts = x_ref.shape[0]
    w = w_ref[...]
    cols = None
    for k in range(TOP_K):
        chunks = [w[:, k:k + 1] * c for c in _load_token_tiles(y_refs[k], ts)]
        cols = chunks if cols is None else [a + c for a, c in zip(cols, chunks)]
    o_ref[...] = x_ref[...] + gate_ref[...] * jnp.concatenate(cols, axis=1)


def _moe_combine(x, gate, y, weights):
    b, s, d = x.shape
    ts = min(s, 256)
    per_b = s // ts
    tiles_per_k = b * per_b
    y_spec = lambda k: pl.BlockSpec((ts * SUBLANES, LANES), lambda i, j: (k * tiles_per_k + i * per_b + j, 0))
    return pl.pallas_call(
        _combine_kernel,
        grid=(b, per_b),
        in_specs=[pl.BlockSpec((None, ts, d), lambda i, j: (i, j, 0)),
                  pl.BlockSpec((None, 1, d), lambda i, j: (i, 0, 0)),
                  pl.BlockSpec((ts, LANES), lambda i, j: (i * per_b + j, 0))]
                 + [y_spec(k) for k in range(TOP_K)],
        out_specs=pl.BlockSpec((None, ts, d), lambda i, j: (i, j, 0)),
        out_shape=jax.ShapeDtypeStruct((b, s, d), F32),
        compiler_params=_params("arbitrary", "arbitrary"),
        name="moe_combine",
    )(x, gate.reshape(b, 1, d), weights, *([y] * TOP_K))


def _moe_ffn(x, h, gate, layer, w_router, b_router, w_gate, b_gate, w_up, b_up, w_down, b_down):
    b, s, d = x.shape
    t = b * s
    e = w_router.shape[1]
    tm = 512
    h2 = h.reshape(t * SUBLANES, LANES)
    idx_pad, wgt_pad, rank_pad, counts_pad = _route(h2, w_router, b_router)
    idx = idx_pad[:, :TOP_K]
    counts = counts_pad[0, :e]

    padded = ((counts + tm - 1) // tm) * tm
    ends = jnp.cumsum(padded)
    starts = ends - padded
    pos = starts[idx] + rank_pad[:, :TOP_K]
    n_slots = t * TOP_K + e * tm
    n_tiles = n_slots // tm
    tile_start = jnp.arange(n_tiles, dtype=jnp.int32) * tm
    tile_expert = jnp.sum((tile_start[:, None] >= ends[None, :]).astype(jnp.int32), axis=1)
    tile_expert = jnp.minimum(tile_expert, e - 1)
    n_valid = (ends[-1:] // tm).astype(jnp.int32)
    tile_first = jnp.concatenate([jnp.ones((1,), jnp.int32),
                                  (tile_expert[1:] != tile_expert[:-1]).astype(jnp.int32)])

    n_pairs = t * TOP_K
    slot_pair = jnp.full((n_slots,), -1, jnp.int32).at[pos.reshape(-1)].set(
        jnp.arange(n_pairs, dtype=jnp.int32), unique_indices=True)
    slot_id = jnp.arange(n_slots, dtype=jnp.int32)
    spare = n_pairs + ((slot_id // tm) % 2) * tm + slot_id % tm
    used = slot_pair >= 0
    pair_token, pair_k = slot_pair // TOP_K, slot_pair % TOP_K
    slot_token = (jnp.where(used, pair_token, 0) * SUBLANES).reshape(n_tiles, 1, tm)
    slot_dst = (jnp.where(used, pair_k * t + pair_token, spare) * SUBLANES).reshape(n_tiles, 1, tm)

    y = _moe_experts(h2, slot_token, slot_dst, n_pairs + 2 * tm, tile_expert, tile_first, n_valid, layer,
                     w_gate, b_gate, w_up, b_up, w_down, b_down, tm)
    return _moe_combine(x, gate, y, wgt_pad)


def kernel(x, c, mod_w, mod_b, mix_norm_g, ffn_norm_g, ab_w_in, ab_w_out, moba_q_gain, moba_k_gain,
           fox_w_in, fox_b_f, fox_w_out, fox_q_gain, fox_k_gain, router_w, router_b,
           exp_w_gate, exp_b_gate, exp_w_up, exp_b_up, exp_w_down, exp_b_down):
    b, s, d = x.shape
    depth = mod_w.shape[0]
    n_heads = d // HEAD_DIM
    n_moba = n_heads // 2
    n_sb = n_heads - n_moba
    mod = _adaln_mod(c, mod_w, mod_b)

    for layer in range(depth):
        sh1, sc1, g1, sh2, sc2, g2 = [mod[layer, :, i * d:(i + 1) * d] for i in range(N_MOD)]
        j = layer // 2
        h = _norm_mod(x, mix_norm_g[layer], sc1, sh1)
        if layer % 2 == 0:
            wa = n_moba * HEAD_DIM
            wb = n_sb * HEAD_DIM
            col_scale = jnp.concatenate([
                jnp.tile(moba_q_gain[j] * Q_PRESCALE, n_moba), jnp.tile(moba_k_gain[j], n_moba),
                jnp.ones((wa,), F32), jnp.full((wb,), Q_PRESCALE, F32), jnp.ones((2 * wb,), F32)])
            n_cols = col_scale.shape[0]
            proj = _in_proj(h.reshape(b * s, d), ab_w_in[j].astype(BF16), col_scale.reshape(1, n_cols), 2 * wa)
            proj = proj.reshape(b, s, n_cols)
            o_a = _moba_attention(proj, n_moba, 0)
            o_b = _sb_attention(proj, n_sb, 3 * n_moba)
            w_out = ab_w_out[j].astype(BF16)
            x = _out_proj_residual(x, g1, [(o_a, w_out[:wa]), (o_b, w_out[wa:])])
        else:
            w = n_heads * HEAD_DIM
            col_scale = jnp.concatenate([jnp.tile(fox_q_gain[j] * Q_PRESCALE, n_heads),
                                         jnp.tile(fox_k_gain[j], n_heads), jnp.ones((w,), F32)])
            proj = _in_proj(h.reshape(b * s, d), fox_w_in[j, :, :3 * w].astype(BF16),
                            col_scale.reshape(1, 3 * w), 2 * w)
            cum_f = _forget_cumsum(h, fox_w_in[j, :, 3 * w:], fox_b_f[j])
            o = _fox_attention(proj.reshape(b, s, 3 * w), cum_f, n_heads)
            x = _out_proj_residual(x, g1, [(o, fox_w_out[j].astype(BF16))])
        h = _norm_mod(x, ffn_norm_g[layer], sc2, sh2, packed=True)
        x = _moe_ffn(x, h, g2, layer, router_w[layer], router_b[layer], exp_w_gate, exp_b_gate,
                     exp_w_up, exp_b_up, exp_w_down, exp_b_down)
    return x
```

```python
import functools
import math

import jax
import jax.numpy as jnp
from jax import lax
from jax.experimental import pallas as pl
from jax.experimental.pallas import tpu as pltpu

F32 = jnp.float32
BF16 = jnp.bfloat16

HEAD_DIM = 128
MOBA_BLOCK = 256
MOBA_TOPK = 3
TOP_K = 4
SWIGLU_LIMIT = 7.0
SWIGLU_ALPHA = 1.702
RMS_EPS = 1e-5
N_MOD = 6
LOG2E = math.log2(math.e)
Q_PRESCALE = HEAD_DIM ** -0.5 * LOG2E
LANES = 128
SUBLANES = 8
BF16_ROWS = 16
MASKED = -1e30
SB_EXIT_BITS = 160.0
ROW_DMA_UNROLL = 16
MOE_UP_SPLIT = 2
VMEM_LIMIT_BYTES = 56 * 1024 * 1024


def _params(*semantics):
    return pltpu.CompilerParams(dimension_semantics=semantics, vmem_limit_bytes=VMEM_LIMIT_BYTES)


def _dot(a, b):
    return jnp.dot(a, b, preferred_element_type=F32)


def _dot_nt(a, b):
    return lax.dot_general(a, b, (((1,), (1,)), ((), ())), preferred_element_type=F32)


def _split_bf16(x):
    hi = x.astype(BF16)
    lo = (x - hi.astype(F32)).astype(BF16)
    return hi, lo


def _kv_block(ref, j, tk):
    return ref[pl.ds(pl.multiple_of(j * tk, tk), tk), :]


def _pack_bf16_pair(lo, hi):
    lo_bits = lax.bitcast_convert_type(lo.astype(BF16).astype(F32), jnp.uint32)
    hi_bits = lax.bitcast_convert_type(hi.astype(BF16).astype(F32), jnp.uint32)
    return lax.shift_right_logical(lo_bits, jnp.uint32(16)) | (hi_bits & jnp.uint32(0xFFFF0000))


def _unpack_bf16_pair(word):
    lo = lax.bitcast_convert_type(lax.shift_left(word, jnp.uint32(16)), F32)
    hi = lax.bitcast_convert_type(word & jnp.uint32(0xFFFF0000), F32)
    return lo, hi


TOKEN_DIM = 2 * SUBLANES * LANES


def _store_token_tiles(ref, y, lead=()):
    n = y.shape[0]
    half = TOKEN_DIM // 2
    for s in range(SUBLANES):
        lo = y[:, s * LANES:(s + 1) * LANES]
        hi = y[:, half + s * LANES:half + (s + 1) * LANES]
        ref[lead + (pl.ds(s, n, stride=SUBLANES), slice(None))] = _pack_bf16_pair(lo, hi)


def _load_token_tiles(ref, n, lead=()):
    los, his = [], []
    for s in range(SUBLANES):
        lo, hi = _unpack_bf16_pair(ref[lead + (pl.ds(s, n, stride=SUBLANES), slice(None))])
        los.append(lo)
        his.append(hi)
    return los + his


def _load_token_tiles_bf16(ref, n, lead=()):
    return jnp.concatenate([c.astype(BF16) for c in _load_token_tiles(ref, n, lead)], axis=1)


def _mod_kernel(c_ref, w_ref, b_ref, o_ref):
    c = c_ref[...]
    c_act = c / (1.0 + jnp.exp(-c))
    o_ref[...] = _dot(c_act, w_ref[...]) + b_ref[...]


def _adaln_mod(c, mod_w, mod_b):
    depth, d, n = mod_w.shape
    b = c.shape[0]
    bp = -(-b // SUBLANES) * SUBLANES
    c_pad = jnp.zeros((bp, d), F32).at[:b].set(c)
    tn = min(n, 1024)
    out = pl.pallas_call(
        _mod_kernel,
        grid=(depth, n // tn),
        in_specs=[pl.BlockSpec((bp, d), lambda l, j: (0, 0)),
                  pl.BlockSpec((None, d, tn), lambda l, j: (l, 0, j)),
                  pl.BlockSpec((None, 1, tn), lambda l, j: (l, 0, j))],
        out_specs=pl.BlockSpec((None, bp, tn), lambda l, j: (l, 0, j)),
        out_shape=jax.ShapeDtypeStruct((depth, bp, n), F32),
        compiler_params=_params("arbitrary", "arbitrary"),
        name="adaln_mod",
    )(c_pad, mod_w, mod_b.reshape(depth, 1, n))
    return out[:, :b]


def _norm_kernel(x_ref, g_ref, sc_ref, sh_ref, o_ref, *, packed):
    x = x_ref[...]
    ms = jnp.mean(x * x, axis=-1, keepdims=True)
    y = x * lax.rsqrt(ms + RMS_EPS)
    y = (y * g_ref[...]) * (1.0 + sc_ref[...]) + sh_ref[...]
    if packed:
        _store_token_tiles(o_ref, y)
    else:
        o_ref[...] = y.astype(o_ref.dtype)


def _norm_mod(x, gain, scale, shift, packed=False):
    b, s, d = x.shape
    ts = min(s, 512)
    if packed:
        assert d == TOKEN_DIM
        out_block, out_shape = (None, ts * SUBLANES, LANES), (b, s * SUBLANES, LANES)
    else:
        out_block, out_shape = (None, ts, d), (b, s, d)
    return pl.pallas_call(
        functools.partial(_norm_kernel, packed=packed),
        grid=(b, s // ts),
        in_specs=[pl.BlockSpec((None, ts, d), lambda i, j: (i, j, 0)),
                  pl.BlockSpec((1, d), lambda i, j: (0, 0)),
                  pl.BlockSpec((None, 1, d), lambda i, j: (i, 0, 0)),
                  pl.BlockSpec((None, 1, d), lambda i, j: (i, 0, 0))],
        out_specs=pl.BlockSpec(out_block, lambda i, j: (i, j, 0)),
        out_shape=jax.ShapeDtypeStruct(out_shape, jnp.uint32 if packed else BF16),
        compiler_params=_params("arbitrary", "arbitrary"),
        name="norm_mod",
    )(x, gain.reshape(1, d), scale.reshape(b, 1, d), shift.reshape(b, 1, d))


def _inproj_kernel(h_ref, w_ref, gain_ref, o_ref, *, n_norm_tiles):
    acc = _dot(h_ref[...], w_ref[...])
    j = pl.program_id(0)

    @pl.when(j < n_norm_tiles)
    def _():
        for g in range(acc.shape[1] // HEAD_DIM):
            sl = slice(g * HEAD_DIM, (g + 1) * HEAD_DIM)
            blk = acc[:, sl]
            ms = jnp.mean(blk * blk, axis=-1, keepdims=True)
            o_ref[:, sl] = (blk * lax.rsqrt(ms + RMS_EPS) * gain_ref[:, sl]).astype(o_ref.dtype)

    @pl.when(j >= n_norm_tiles)
    def _():
        o_ref[...] = (acc * gain_ref[...]).astype(o_ref.dtype)


def _in_proj(h, w, col_scale, n_norm_cols):
    t, d = h.shape
    n = w.shape[1]
    tm = min(t, 1024)
    tn = min(n, 1024)
    assert n_norm_cols % tn == 0 and n % tn == 0 and t % tm == 0
    return pl.pallas_call(
        functools.partial(_inproj_kernel, n_norm_tiles=n_norm_cols // tn),
        grid=(n // tn, t // tm),
        in_specs=[pl.BlockSpec((tm, d), lambda j, i: (i, 0)),
                  pl.BlockSpec((d, tn), lambda j, i: (0, j)),
                  pl.BlockSpec((1, tn), lambda j, i: (0, j))],
        out_specs=pl.BlockSpec((tm, tn), lambda j, i: (i, j)),
        out_shape=jax.ShapeDtypeStruct((t, n), BF16),
        compiler_params=_params("arbitrary", "arbitrary"),
        name="in_proj",
    )(h, w, col_scale)


def _fgate_kernel(h_ref, wf_ref, bf_ref, o_ref, carry_ref):
    @pl.when(pl.program_id(1) == 0)
    def _():
        carry_ref[...] = jnp.zeros_like(carry_ref)

    ts = h_ref.shape[0]
    logit = _dot(h_ref[...], wf_ref[...]) + bf_ref[...]
    log_f = jnp.minimum(logit, 0.0) - jnp.log(1.0 + jnp.exp(-jnp.abs(logit)))
    row = lax.broadcasted_iota(jnp.int32, (ts, ts), 0)
    col = lax.broadcasted_iota(jnp.int32, (ts, ts), 1)
    lower = jnp.where(col <= row, 1.0, 0.0).astype(BF16)
    hi, lo = _split_bf16(log_f)
    cum = _dot(lower, hi) + _dot(lower, lo) + carry_ref[0:1, :]
    o_ref[...] = cum
    carry_ref[...] = jnp.broadcast_to(cum[ts - 1:ts, :], carry_ref.shape)


def _forget_cumsum(h, w_f, b_f):
    b, s, d = h.shape
    nh = w_f.shape[1]
    assert nh <= LANES
    ts = min(s, 512)
    w_pad = jnp.zeros((d, LANES), BF16).at[:, :nh].set(w_f.astype(BF16))
    b_pad = jnp.zeros((1, LANES), F32).at[0, :nh].set(b_f)
    return pl.pallas_call(
        _fgate_kernel,
        grid=(b, s // ts),
        in_specs=[pl.BlockSpec((None, ts, d), lambda i, j: (i, j, 0)),
                  pl.BlockSpec((d, LANES), lambda i, j: (0, 0)),
                  pl.BlockSpec((1, LANES), lambda i, j: (0, 0))],
        out_specs=pl.BlockSpec((None, ts, LANES), lambda i, j: (i, j, 0)),
        out_shape=jax.ShapeDtypeStruct((b, s, LANES), F32),
        scratch_shapes=[pltpu.VMEM((SUBLANES, LANES), F32)],
        compiler_params=_params("arbitrary", "arbitrary"),
        name="forget_cumsum",
    )(h, w_pad, b_pad)


VT_ROWS = HEAD_DIM + BF16_ROWS


def _store_v_transposed(v_ref, vt_ref, c, tile):
    vb = _kv_block(v_ref, c, tile).astype(F32)
    vt_ref[c, 0:HEAD_DIM, :] = vb.T.astype(BF16)
    vt_ref[c, HEAD_DIM:VT_ROWS, :] = jnp.ones((BF16_ROWS, tile), BF16)


def _softmax_accumulate(s, vt_blk, m_ref, acc_ref):
    m_old = m_ref[...]
    m_new = jnp.maximum(m_old, jnp.max(s, axis=0, keepdims=True))
    alpha = jnp.exp2(m_old - m_new)
    p = jnp.exp2((s - m_new).astype(BF16))
    m_ref[...] = m_new
    acc_ref[...] = alpha * acc_ref[...] + _dot(vt_blk, p)


def _softmax_finish(acc_ref, o_ref, qi, tile):
    acc = acc_ref[...]
    out_t = acc[0:HEAD_DIM, :] / acc[HEAD_DIM:HEAD_DIM + 1, :]
    o_ref[pl.ds(pl.multiple_of(qi * tile, tile), tile), :] = out_t.T.astype(o_ref.dtype)


def _flash_sweep(qi, n_past, produce, s0_ref, s1_ref, vt_ref, m_ref, acc_ref, tile):
    m_ref[...] = jnp.full(m_ref.shape, MASKED, F32)
    acc_ref[...] = jnp.zeros_like(acc_ref)
    key = lax.broadcasted_iota(jnp.int32, (tile, tile), 0)
    qry = lax.broadcasted_iota(jnp.int32, (tile, tile), 1)
    s0_ref[...] = jnp.where(key <= qry, produce(qi), MASKED)
    n_pos = n_past + 1
    farthest = qi - n_past

    def pair(i):
        blk = qi - 2 * i
        s1_ref[...] = produce(blk - 1)
        _softmax_accumulate(s0_ref[...], vt_ref[blk], m_ref, acc_ref)
        s0_ref[...] = produce(jnp.maximum(blk - 2, farthest))
        _softmax_accumulate(s1_ref[...], vt_ref[blk - 1], m_ref, acc_ref)

    def two_pairs(i, carry):
        pair(2 * i)
        pair(2 * i + 1)
        return carry

    n_pairs = n_pos // 2
    lax.fori_loop(0, n_pairs // 2, two_pairs, 0)

    @pl.when(lax.rem(n_pairs, 2) == 1)
    def _():
        pair(n_pairs - 1)

    @pl.when(lax.rem(n_pos, 2) == 1)
    def _():
        _softmax_accumulate(s0_ref[...], vt_ref[farthest], m_ref, acc_ref)


SKIP_BITS = 162.0


def _logit_spread_bound(q, kmax_sq_ref):
    q32 = q.astype(F32)
    qmax_sq = jnp.max(jnp.sum(q32 * q32, axis=-1, keepdims=True), axis=0, keepdims=True)
    return 2.0 * jnp.sqrt(qmax_sq * kmax_sq_ref[0:1, 0:1])


def _update_kmax_sq(kmax_sq_ref, kb):
    blk_max = jnp.max(jnp.sum(kb * kb, axis=-1, keepdims=True), axis=0, keepdims=True)
    kmax_sq_ref[...] = jnp.maximum(kmax_sq_ref[...], jnp.broadcast_to(blk_max, kmax_sq_ref.shape))


def _fox_kernel(q_ref, k_ref, v_ref, f_ref, o_ref, vt_ref, frep_ref, fend_ref, kmax_sq_ref, s0_ref, s1_ref, m_ref,
                acc_ref, *, tile, n_kv):
    head = pl.program_id(1)
    lane = lax.broadcasted_iota(jnp.int32, (tile, LANES), 1)
    fend_ref[...] = jnp.zeros_like(fend_ref)
    kmax_sq_ref[...] = jnp.zeros_like(kmax_sq_ref)

    def prep(c, carry):
        _store_v_transposed(v_ref, vt_ref, c, tile)
        _update_kmax_sq(kmax_sq_ref, _kv_block(k_ref, c, tile).astype(F32))
        f_blk = _kv_block(f_ref, c, tile)
        f_col = jnp.sum(jnp.where(lane == head, f_blk, 0.0), axis=-1, keepdims=True)
        f_rep = jnp.broadcast_to(f_col * LOG2E, (tile, LANES))
        frep_ref[pl.ds(pl.multiple_of(c * tile, tile), tile), :] = f_rep
        fend_ref[pl.ds(c, 1), :] = f_rep[tile - 1:tile, :]
        return carry

    lax.fori_loop(0, n_kv, prep, 0)

    def query_tile(qi, carry):
        qi = jnp.asarray(qi, jnp.int32)
        q = _kv_block(q_ref, qi, tile)
        f_first = frep_ref[pl.ds(pl.multiple_of(qi * tile, tile), 1), :]
        gap = fend_ref[...] - f_first
        block_id = lax.broadcasted_iota(jnp.int32, fend_ref.shape, 0)
        needed = jnp.logical_and(block_id < qi, gap <= _logit_spread_bound(q, kmax_sq_ref) + SKIP_BITS)
        n_past = jnp.sum(jnp.where(needed, 1.0, 0.0)[:, 0:1]).astype(jnp.int32)

        def produce(j):
            f_rep = _kv_block(frep_ref, j, tile)
            return _dot_nt(_kv_block(k_ref, j, tile), q) - jnp.concatenate([f_rep] * (tile // LANES), axis=1)

        _flash_sweep(qi, n_past, produce, s0_ref, s1_ref, vt_ref, m_ref, acc_ref, tile)
        _softmax_finish(acc_ref, o_ref, qi, tile)
        return carry

    lax.fori_loop(0, n_kv, query_tile, 0)


def _fox_attention(proj, cum_f, n_heads):
    b, s, _ = proj.shape
    tile = min(s, 512)
    n_kv = s // tile
    return pl.pallas_call(
        functools.partial(_fox_kernel, tile=tile, n_kv=n_kv),
        grid=(b, n_heads),
        in_specs=[pl.BlockSpec((None, s, HEAD_DIM), lambda i, h: (i, 0, h)),
                  pl.BlockSpec((None, s, HEAD_DIM), lambda i, h: (i, 0, n_heads + h)),
                  pl.BlockSpec((None, s, HEAD_DIM), lambda i, h: (i, 0, 2 * n_heads + h)),
                  pl.BlockSpec((None, s, LANES), lambda i, h: (i, 0, 0))],
        out_specs=pl.BlockSpec((None, s, HEAD_DIM), lambda i, h: (i, 0, h)),
        out_shape=jax.ShapeDtypeStruct((b, s, n_heads * HEAD_DIM), BF16),
        scratch_shapes=[pltpu.VMEM((n_kv, VT_ROWS, tile), BF16), pltpu.VMEM((s, LANES), F32),
                        pltpu.VMEM((-(-n_kv // SUBLANES) * SUBLANES, LANES), F32),
                        pltpu.VMEM((SUBLANES, LANES), F32),
                        pltpu.VMEM((tile, tile), F32), pltpu.VMEM((tile, tile), F32),
                        pltpu.VMEM((1, tile), F32),
                        pltpu.VMEM((VT_ROWS, tile), F32)],
        compiler_params=_params("arbitrary", "arbitrary"),
        name="fox_attention",
    )(proj, proj, proj, cum_f)


def _moba_kernel(q_ref, k_ref, v_ref, slope_ref, o_ref, kmean_ref, kmax_sq_ref, vt_ref, sel_ref, alibi_ref,
                 s0_ref, s1_ref, m_ref, acc_ref, *, tile, n_blocks):
    blk = MOBA_BLOCK
    per_tile = tile // blk
    kmean_ref[...] = jnp.zeros_like(kmean_ref)
    kmax_sq_ref[...] = jnp.zeros_like(kmax_sq_ref)

    def block_mean(n, carry):
        kb = _kv_block(k_ref, n, blk).astype(F32)
        kmean_ref[pl.ds(n, 1), :] = jnp.mean(kb, axis=0, keepdims=True)
        _update_kmax_sq(kmax_sq_ref, kb)
        return carry

    def transpose_v(c, carry):
        _store_v_transposed(v_ref, vt_ref, c, tile)
        return carry

    lax.fori_loop(0, n_blocks, block_mean, 0)
    lax.fori_loop(0, n_blocks // per_tile, transpose_v, 0)

    slope = slope_ref[...] * LOG2E
    alibi_ref[...] = slope * lax.broadcasted_iota(jnp.int32, (tile, tile), 0).astype(F32)
    block_id = lax.broadcasted_iota(jnp.int32, (LANES, tile), 0).astype(F32)
    qry_block = jnp.floor(lax.broadcasted_iota(jnp.int32, (LANES, tile), 1).astype(F32) * (1.0 / blk))

    def query_tile(qi, carry):
        qi = jnp.asarray(qi, jnp.int32)
        q = _kv_block(q_ref, qi, tile)
        own = jnp.asarray(qi * per_tile, F32) + qry_block
        gate = _dot_nt(kmean_ref[...].astype(BF16), q)
        gate = jnp.where(block_id < own, gate, -jnp.inf)
        sel = jnp.where(block_id == own, 1.0, 0.0)
        for slot in range(MOBA_TOPK):
            best = jnp.max(gate, axis=0, keepdims=True)
            idx = jnp.min(jnp.where(gate == best, block_id, float(LANES)), axis=0, keepdims=True)
            hit = block_id == idx
            slot_ok = jnp.where(float(slot) < own, 1.0, 0.0)
            sel = jnp.maximum(sel, jnp.where(hit, slot_ok, 0.0))
            gate = jnp.where(hit, -jnp.inf, gate)
        sel_ref[...] = jnp.where(sel > 0.5, 0.0, MASKED)

        def produce(j):
            shift = slope * jnp.asarray((j - qi) * tile, F32)
            per_query = jnp.concatenate(
                [jnp.broadcast_to(sel_ref[pl.ds(j * per_tile + r, 1), :] + shift, (blk, tile))
                 for r in range(per_tile)], axis=0)
            return (_dot_nt(_kv_block(k_ref, j, tile), q) + alibi_ref[...]) + per_query

        reach = (_logit_spread_bound(q, kmax_sq_ref) + SKIP_BITS) / slope
        tiles_back = jnp.floor((reach - 1.0) * (1.0 / tile)) + 2.0
        n_past = jnp.minimum(jnp.minimum(jnp.max(tiles_back), float(n_blocks)).astype(jnp.int32), qi)

        _flash_sweep(qi, n_past, produce, s0_ref, s1_ref, vt_ref, m_ref, acc_ref, tile)
        _softmax_finish(acc_ref, o_ref, qi, tile)
        return carry

    lax.fori_loop(0, n_blocks // per_tile, query_tile, 0)


def _moba_attention(proj, n_heads, col_block):
    b, s, _ = proj.shape
    blk = MOBA_BLOCK
    n_blocks = s // blk
    tile = min(s, 2 * blk)
    assert s % tile == 0 and tile % blk == 0 and n_blocks <= LANES
    slopes = jnp.asarray([2.0 ** (-8.0 * (i + 1) / n_heads) for i in range(n_heads)], F32)
    slope_rows = jnp.broadcast_to(slopes[:, None, None], (n_heads, 1, tile))
    return pl.pallas_call(
        functools.partial(_moba_kernel, tile=tile, n_blocks=n_blocks),
        grid=(b, n_heads),
        in_specs=[pl.BlockSpec((None, s, HEAD_DIM), lambda i, h: (i, 0, col_block + h)),
                  pl.BlockSpec((None, s, HEAD_DIM), lambda i, h: (i, 0, col_block + n_heads + h)),
                  pl.BlockSpec((None, s, HEAD_DIM), lambda i, h: (i, 0, col_block + 2 * n_heads + h)),
                  pl.BlockSpec((None, 1, tile), lambda i, h: (h, 0, 0))],
        out_specs=pl.BlockSpec((None, s, HEAD_DIM), lambda i, h: (i, 0, h)),
        out_shape=jax.ShapeDtypeStruct((b, s, n_heads * HEAD_DIM), BF16),
        scratch_shapes=[pltpu.VMEM((LANES, HEAD_DIM), F32), pltpu.VMEM((SUBLANES, LANES), F32),
                        pltpu.VMEM((s // tile, VT_ROWS, tile), BF16),
                        pltpu.VMEM((LANES, tile), F32), pltpu.VMEM((tile, tile), F32),
                        pltpu.VMEM((tile, tile), F32), pltpu.VMEM((tile, tile), F32),
                        pltpu.VMEM((1, tile), F32), pltpu.VMEM((VT_ROWS, tile), F32)],
        compiler_params=_params("arbitrary", "arbitrary"),
        name="moba_attention",
    )(proj, proj, proj, slope_rows)


def _sb_kernel(q_ref, k_ref, v_ref, o_ref, after_ref, decay_ref, acc_ref, *, tile, n_q):
    row = lax.broadcasted_iota(jnp.int32, (tile, tile), 0)
    col = lax.broadcasted_iota(jnp.int32, (tile, tile), 1)
    after_ref[...] = jnp.where(row > col, 1.0, 0.0).astype(BF16)

    def query_tile(qi, carry):
        qi = jnp.asarray(qi, jnp.int32)
        q = _kv_block(q_ref, qi, tile)
        causal = col < row

        def logits_and_drop(j, masked):
            z = _dot_nt(q, _kv_block(k_ref, j, tile))
            drop = jnp.maximum(z, 0.0) + jnp.log2(1.0 + jnp.exp2(-jnp.abs(z)))
            if masked:
                drop = jnp.where(causal, drop, 0.0)
            return z, drop

        def weighted_values(j, z, drop, decay_right, masked):
            hi, lo = _split_bf16(drop)
            after = after_ref[...]
            decay = _dot(hi, after) + _dot(lo, after) + decay_right
            a = jnp.exp2(z - drop - decay)
            if masked:
                a = jnp.where(causal, a, 0.0)
            return _dot(a.astype(BF16), _kv_block(v_ref, j, tile))

        @pl.when(qi == 0)
        def _():
            z, drop = logits_and_drop(qi, True)
            acc_ref[...] = weighted_values(qi, z, drop, 0.0, True)
            decay_ref[...] = jnp.sum(drop, axis=-1, keepdims=True)

        @pl.when(qi > 0)
        def _():
            z_d, drop_d = logits_and_drop(qi, True)
            z_p, drop_p = logits_and_drop(qi - 1, False)
            decay_d = jnp.sum(drop_d, axis=-1, keepdims=True)
            acc_ref[...] = (weighted_values(qi, z_d, drop_d, 0.0, True)
                            + weighted_values(qi - 1, z_p, drop_p, decay_d, False))
            decay_ref[...] = decay_d + jnp.sum(drop_p, axis=-1, keepdims=True)

        def cond(state):
            jj, least = state
            return jnp.logical_and(jj < qi, least < SB_EXIT_BITS)

        def body(state):
            jj, _ = state
            j = qi - 1 - jj
            z, drop = logits_and_drop(j, False)
            acc_ref[...] += weighted_values(j, z, drop, decay_ref[...], False)
            decay_ref[...] += jnp.sum(drop, axis=-1, keepdims=True)
            return jj + 1, jnp.min(decay_ref[...])

        lax.while_loop(cond, body, (jnp.int32(1), jnp.min(decay_ref[...])))
        o_ref[pl.ds(pl.multiple_of(qi * tile, tile), tile), :] = acc_ref[...].astype(o_ref.dtype)
        return carry

    lax.fori_loop(0, n_q, query_tile, 0)


def _sb_attention(proj, n_heads, col_block):
    b, s, _ = proj.shape
    tile = min(s, 256)
    return pl.pallas_call(
        functools.partial(_sb_kernel, tile=tile, n_q=s // tile),
        grid=(b, n_heads),
        in_specs=[pl.BlockSpec((None, s, HEAD_DIM), lambda i, h: (i, 0, col_block + h)),
                  pl.BlockSpec((None, s, HEAD_DIM), lambda i, h: (i, 0, col_block + n_heads + h)),
                  pl.BlockSpec((None, s, HEAD_DIM), lambda i, h: (i, 0, col_block + 2 * n_heads + h))],
        out_specs=pl.BlockSpec((None, s, HEAD_DIM), lambda i, h: (i, 0, h)),
        out_shape=jax.ShapeDtypeStruct((b, s, n_heads * HEAD_DIM), BF16),
        scratch_shapes=[pltpu.VMEM((tile, tile), BF16), pltpu.VMEM((tile, 1), F32),
                        pltpu.VMEM((tile, HEAD_DIM), F32)],
        compiler_params=_params("arbitrary", "arbitrary"),
        name="sb_attention",
    )(proj, proj, proj)


def _outproj_kernel(*refs, n_parts):
    x_ref, gate_ref = refs[0], refs[1]
    parts = refs[2:2 + 2 * n_parts]
    o_ref = refs[2 + 2 * n_parts]
    y = _dot(parts[0][...], parts[1][...])
    for p in range(1, n_parts):
        y += _dot(parts[2 * p][...], parts[2 * p + 1][...])
    o_ref[...] = x_ref[...] + gate_ref[...] * y


def _out_proj_residual(x, gate, parts):
    b, s, d = x.shape
    ts = min(s, 512)
    in_specs = [pl.BlockSpec((None, ts, d), lambda i, j: (i, j, 0)),
                pl.BlockSpec((None, 1, d), lambda i, j: (i, 0, 0))]
    args = [x, gate.reshape(b, 1, d)]
    for a, w in parts:
        kp = a.shape[-1]
        in_specs += [pl.BlockSpec((None, ts, kp), lambda i, j: (i, j, 0)),
                     pl.BlockSpec((kp, d), lambda i, j: (0, 0))]
        args += [a, w]
    return pl.pallas_call(
        functools.partial(_outproj_kernel, n_parts=len(parts)),
        grid=(b, s // ts),
        in_specs=in_specs,
        out_specs=pl.BlockSpec((None, ts, d), lambda i, j: (i, j, 0)),
        out_shape=jax.ShapeDtypeStruct((b, s, d), F32),
        compiler_params=_params("arbitrary", "arbitrary"),
        name="out_proj",
    )(*args)


def _router_kernel(h_ref, w_ref, b_ref, idx_ref, wgt_ref, rank_ref, cnt_ref, carry_ref):
    @pl.when(pl.program_id(0) == 0)
    def _():
        carry_ref[...] = jnp.zeros_like(carry_ref)

    tm = h_ref.shape[0] // SUBLANES
    logits = _dot(_load_token_tiles_bf16(h_ref, tm), w_ref[...]) + b_ref[...]
    lane = lax.broadcasted_iota(jnp.int32, (tm, LANES), 1).astype(F32)
    vals, idxs = [], []
    onehot = jnp.zeros((tm, LANES), F32)
    for _ in range(TOP_K):
        best = jnp.max(logits, axis=-1, keepdims=True)
        idx = jnp.min(jnp.where(logits == best, lane, float(LANES)), axis=-1, keepdims=True)
        hit = lane == idx
        onehot = jnp.where(hit, 1.0, onehot)
        logits = jnp.where(hit, -jnp.inf, logits)
        vals.append(best)
        idxs.append(idx)
    exps = [jnp.exp(v - vals[0]) for v in vals]
    denom = exps[0]
    for e in exps[1:]:
        denom = denom + e

    row = lax.broadcasted_iota(jnp.int32, (tm, tm), 0)
    col = lax.broadcasted_iota(jnp.int32, (tm, tm), 1)
    before = jnp.where(col < row, 1.0, 0.0).astype(BF16)
    prior = _dot(before, onehot.astype(BF16)) + carry_ref[0:1, :]

    idx_out = jnp.zeros((tm, LANES), F32)
    wgt_out = jnp.zeros((tm, LANES), F32)
    rank_out = jnp.zeros((tm, LANES), F32)
    for k in range(TOP_K):
        rank_k = jnp.sum(jnp.where(lane == idxs[k], prior, 0.0), axis=-1, keepdims=True)
        slot = lane == float(k)
        idx_out = jnp.where(slot, idxs[k], idx_out)
        wgt_out = jnp.where(slot, exps[k] / denom, wgt_out)
        rank_out = jnp.where(slot, rank_k, rank_out)
    idx_ref[...] = idx_out.astype(jnp.int32)
    wgt_ref[...] = wgt_out
    rank_ref[...] = rank_out.astype(jnp.int32)
    counts = carry_ref[...] + jnp.sum(onehot, axis=0, keepdims=True)
    carry_ref[...] = counts
    cnt_ref[...] = counts.astype(jnp.int32)


def _route(h, w_router, b_router):
    t = h.shape[0] // SUBLANES
    d = TOKEN_DIM
    e = w_router.shape[1]
    assert e <= LANES
    tm = min(t, 512)
    w_pad = jnp.zeros((d, LANES), BF16).at[:, :e].set(w_router.astype(BF16))
    b_pad = jnp.full((1, LANES), MASKED, F32).at[0, :e].set(b_router)
    tok_spec = pl.BlockSpec((tm, LANES), lambda i: (i, 0))
    return pl.pallas_call(
        _router_kernel,
        grid=(t // tm,),
        in_specs=[pl.BlockSpec((tm * SUBLANES, LANES), lambda i: (i, 0)),
                  pl.BlockSpec((d, LANES), lambda i: (0, 0)),
                  pl.BlockSpec((1, LANES), lambda i: (0, 0))],
        out_specs=[tok_spec, tok_spec, tok_spec, pl.BlockSpec((SUBLANES, LANES), lambda i: (0, 0))],
        out_shape=[jax.ShapeDtypeStruct((t, LANES), jnp.int32), jax.ShapeDtypeStruct((t, LANES), F32),
                   jax.ShapeDtypeStruct((t, LANES), jnp.int32),
                   jax.ShapeDtypeStruct((SUBLANES, LANES), jnp.int32)],
        scratch_shapes=[pltpu.VMEM((SUBLANES, LANES), F32)],
        compiler_params=_params("arbitrary"),
        name="moe_router",
    )(h, w_pad, b_pad)


def _start_rows(n_rows, copy_of_row):
    def body(p, carry):
        copy_of_row(2 * p).start(priority=0)
        copy_of_row(2 * p + 1).start(priority=1)
        return carry

    lax.fori_loop(0, n_rows // 2, body, 0, unroll=ROW_DMA_UNROLL // 2)


def _wait_rows(n_rows, copy_of_row):
    def body(r, carry):
        copy_of_row(r).wait()
        return carry

    lax.fori_loop(0, n_rows, body, 0, unroll=ROW_DMA_UNROLL)


def _moe_up_kernel(te_ref, first_ref, nvalid_ref, tok_ref, tok_next_ref, h_hbm, wg_ref, bg_ref, wu_ref, bu_ref,
                   o_ref, xbuf, sem, wg_bf, wu_bf):
    i = pl.program_id(0)
    slot = lax.rem(i, 2)
    n_valid = nvalid_ref[0]
    tm = xbuf.shape[1] // SUBLANES

    def row_copy(tok_smem, buf, r):
        src = h_hbm.at[pl.ds(pl.multiple_of(tok_smem[0, r], SUBLANES), SUBLANES)]
        dst = xbuf.at[buf, pl.ds(pl.multiple_of(r * SUBLANES, SUBLANES), SUBLANES)]
        return pltpu.make_async_copy(src, dst, sem.at[buf])

    @pl.when(jnp.logical_and(i == 0, n_valid > 0))
    def _():
        _start_rows(tm, lambda r: row_copy(tok_ref, 0, r))

    @pl.when(i + 1 < n_valid)
    def _():
        _start_rows(tm, lambda r: row_copy(tok_next_ref, 1 - slot, r))

    @pl.when(first_ref[i] == 1)
    def _():
        wg_bf[...] = wg_ref[...].astype(BF16)
        wu_bf[...] = wu_ref[...].astype(BF16)

    @pl.when(i < n_valid)
    def _():
        _wait_rows(tm, lambda r: row_copy(tok_ref, slot, r))
        x = _load_token_tiles_bf16(xbuf, tm, lead=(slot,))
        g = jnp.minimum(_dot(x, wg_bf[...]) + bg_ref[...], SWIGLU_LIMIT)
        u = jnp.clip(_dot(x, wu_bf[...]) + bu_ref[...], -SWIGLU_LIMIT, SWIGLU_LIMIT)
        act = (u + 1.0) * (g / (1.0 + jnp.exp(-SWIGLU_ALPHA * g)))
        o_ref[...] = act.astype(o_ref.dtype)

    @pl.when(i >= n_valid)
    def _():
        o_ref[...] = jnp.zeros_like(o_ref)


def _moe_down_kernel(te_ref, first_ref, nvalid_ref, dst_ref, dst_prev_ref, a_ref, wd_ref, bd_ref, y_hbm,
                     obuf, sem, wd_bf, *, n_tiles, spare_row0):
    i = pl.program_id(0)
    slot = lax.rem(i, 2)
    n_valid = nvalid_ref[0]
    tm = obuf.shape[1] // SUBLANES

    def row_copy(dst_smem, buf, r):
        src = obuf.at[buf, pl.ds(pl.multiple_of(r * SUBLANES, SUBLANES), SUBLANES)]
        dst = y_hbm.at[pl.ds(pl.multiple_of(dst_smem[0, r], SUBLANES), SUBLANES)]
        return pltpu.make_async_copy(src, dst, sem.at[buf])

    @pl.when(i == 0)
    def _():
        obuf[...] = jnp.zeros_like(obuf)
        rows = tm * SUBLANES
        fills = [pltpu.make_async_copy(obuf.at[half], y_hbm.at[pl.ds(spare_row0 + half * rows, rows)], sem.at[half])
                 for half in range(2)]
        for fill in fills:
            fill.start()
        for fill in fills:
            fill.wait()

    @pl.when(first_ref[i] == 1)
    def _():
        wd_bf[...] = wd_ref[...].astype(BF16)

    @pl.when(i < n_valid)
    def _():
        _store_token_tiles(obuf, _dot(a_ref[...], wd_bf[...]) + bd_ref[...], lead=(slot,))
        _start_rows(tm, lambda r: row_copy(dst_ref, slot, r))

    @pl.when(jnp.logical_and(i >= 1, i - 1 < n_valid))
    def _():
        _wait_rows(tm, lambda r: row_copy(dst_prev_ref, 1 - slot, r))

    @pl.when(jnp.logical_and(i == n_tiles - 1, i < n_valid))
    def _():
        _wait_rows(tm, lambda r: row_copy(dst_ref, slot, r))


def _expert_spec(layer, shape):
    return pl.BlockSpec((None, None) + shape, lambda i, te, first, nvalid: (layer, te[i], 0, 0))


def _moe_experts(h_packed, slot_token, slot_dst, n_out_rows, tile_expert, tile_first, n_valid, layer,
                 w_gate, b_gate, w_up, b_up, w_down, b_down, tm):
    d = TOKEN_DIM
    depth, e, _, f = w_gate.shape
    n_tiles = slot_token.shape[0]
    smem_rows = lambda rows, index: pl.BlockSpec((None, 1, rows), index, memory_space=pltpu.SMEM)
    row_spec = lambda rows, width: pl.BlockSpec((rows, width), lambda i, te, first, nvalid: (i, 0))

    tm_up = tm // MOE_UP_SPLIT
    n_up = n_tiles * MOE_UP_SPLIT
    last_up = n_up - 1
    head_of_tile = (jnp.arange(n_up, dtype=jnp.int32) % MOE_UP_SPLIT == 0).astype(jnp.int32)
    act = pl.pallas_call(
        _moe_up_kernel,
        grid_spec=pltpu.PrefetchScalarGridSpec(
            num_scalar_prefetch=3, grid=(n_up,),
            in_specs=[smem_rows(tm_up, lambda i, te, first, nvalid: (i, 0, 0)),
                      smem_rows(tm_up, lambda i, te, first, nvalid: (jnp.minimum(i + 1, last_up), 0, 0)),
                      pl.BlockSpec(memory_space=pl.ANY),
                      _expert_spec(layer, (d, f)), _expert_spec(layer, (1, f)),
                      _expert_spec(layer, (d, f)), _expert_spec(layer, (1, f))],
            out_specs=row_spec(tm_up, f),
            scratch_shapes=[pltpu.VMEM((2, tm_up * SUBLANES, LANES), jnp.uint32), pltpu.SemaphoreType.DMA((2,)),
                            pltpu.VMEM((d, f), BF16), pltpu.VMEM((d, f), BF16)]),
        out_shape=jax.ShapeDtypeStruct((n_tiles * tm, f), BF16),
        compiler_params=_params("arbitrary"),
        name="moe_up",
    )(jnp.repeat(tile_expert, MOE_UP_SPLIT), jnp.repeat(tile_first, MOE_UP_SPLIT) * head_of_tile,
      n_valid * MOE_UP_SPLIT, slot_token.reshape(n_up, 1, tm_up), slot_token.reshape(n_up, 1, tm_up), h_packed,
      w_gate, b_gate.reshape(depth, e, 1, f), w_up, b_up.reshape(depth, e, 1, f))
    buf_shape = (2, tm * SUBLANES, LANES)
    smem_rows = functools.partial(smem_rows, tm)
    row_spec = functools.partial(row_spec, tm)
    return pl.pallas_call(
        functools.partial(_moe_down_kernel, n_tiles=n_tiles, spare_row0=(n_out_rows - 2 * tm) * SUBLANES),
        grid_spec=pltpu.PrefetchScalarGridSpec(
            num_scalar_prefetch=3, grid=(n_tiles,),
            in_specs=[smem_rows(lambda i, te, first, nvalid: (i, 0, 0)),
                      smem_rows(lambda i, te, first, nvalid: (jnp.maximum(i - 1, 0), 0, 0)),
                      row_spec(f), _expert_spec(layer, (f, d)), _expert_spec(layer, (1, d))],
            out_specs=pl.BlockSpec(memory_space=pl.ANY),
            scratch_shapes=[pltpu.VMEM(buf_shape, jnp.uint32), pltpu.SemaphoreType.DMA((2,)),
                            pltpu.VMEM((f, d), BF16)]),
        out_shape=jax.ShapeDtypeStruct((n_out_rows * SUBLANES, LANES), jnp.uint32),
        compiler_params=_params("arbitrary"),
        name="moe_down",
    )(tile_expert, tile_first, n_valid, slot_dst, slot_dst, act, w_down, b_down.reshape(depth, e, 1, d))


def _combine_kernel(*refs):
    x_ref, gate_ref, w_ref = refs[0], refs[1], refs[2]
    y_refs = refs[3:3 + TOP_K]
    o_ref = refs[3 + TOP_K]
    ts = x_ref.shape[0]
    w = w_ref[...]
    cols = None
    for k in range(TOP_K):
        chunks = [w[:, k:k + 1] * c for c in _load_token_tiles(y_refs[k], ts)]
        cols = chunks if cols is None else [a + c for a, c in zip(cols, chunks)]
    o_ref[...] = x_ref[...] + gate_ref[...] * jnp.concatenate(cols, axis=1)


def _moe_combine(x, gate, y, weights):
    b, s, d = x.shape
    ts = min(s, 256)
    per_b = s // ts
    tiles_per_k = b * per_b
    y_spec = lambda k: pl.BlockSpec((ts * SUBLANES, LANES), lambda i, j: (k * tiles_per_k + i * per_b + j, 0))
    return pl.pallas_call(
        _combine_kernel,
        grid=(b, per_b),
        in_specs=[pl.BlockSpec((None, ts, d), lambda i, j: (i, j, 0)),
                  pl.BlockSpec((None, 1, d), lambda i, j: (i, 0, 0)),
                  pl.BlockSpec((ts, LANES), lambda i, j: (i * per_b + j, 0))]
                 + [y_spec(k) for k in range(TOP_K)],
        out_specs=pl.BlockSpec((None, ts, d), lambda i, j: (i, j, 0)),
        out_shape=jax.ShapeDtypeStruct((b, s, d), F32),
        compiler_params=_params("arbitrary", "arbitrary"),
        name="moe_combine",
    )(x, gate.reshape(b, 1, d), weights, *([y] * TOP_K))


def _moe_ffn(x, h, gate, layer, w_router, b_router, w_gate, b_gate, w_up, b_up, w_down, b_down):
    b, s, d = x.shape
    t = b * s
    e = w_router.shape[1]
    tm = 512
    h2 = h.reshape(t * SUBLANES, LANES)
    idx_pad, wgt_pad, rank_pad, counts_pad = _route(h2, w_router, b_router)
    idx = idx_pad[:, :TOP_K]
    counts = counts_pad[0, :e]

    padded = ((counts + tm - 1) // tm) * tm
    ends = jnp.cumsum(padded)
    starts = ends - padded
    pos = starts[idx] + rank_pad[:, :TOP_K]
    n_slots = t * TOP_K + e * tm
    n_tiles = n_slots // tm
    tile_start = jnp.arange(n_tiles, dtype=jnp.int32) * tm
    tile_expert = jnp.sum((tile_start[:, None] >= ends[None, :]).astype(jnp.int32), axis=1)
    tile_expert = jnp.minimum(tile_expert, e - 1)
    n_valid = (ends[-1:] // tm).astype(jnp.int32)
    tile_first = jnp.concatenate([jnp.ones((1,), jnp.int32),
                                  (tile_expert[1:] != tile_expert[:-1]).astype(jnp.int32)])

    n_pairs = t * TOP_K
    slot_pair = jnp.full((n_slots,), -1, jnp.int32).at[pos.reshape(-1)].set(
        jnp.arange(n_pairs, dtype=jnp.int32), unique_indices=True)
    slot_id = jnp.arange(n_slots, dtype=jnp.int32)
    spare = n_pairs + ((slot_id // tm) % 2) * tm + slot_id % tm
    used = slot_pair >= 0
    pair_token, pair_k = slot_pair // TOP_K, slot_pair % TOP_K
    slot_token = (jnp.where(used, pair_token, 0) * SUBLANES).reshape(n_tiles, 1, tm)
    slot_dst = (jnp.where(used, pair_k * t + pair_token, spare) * SUBLANES).reshape(n_tiles, 1, tm)

    y = _moe_experts(h2, slot_token, slot_dst, n_pairs + 2 * tm, tile_expert, tile_first, n_valid, layer,
                     w_gate, b_gate, w_up, b_up, w_down, b_down, tm)
    return _moe_combine(x, gate, y, wgt_pad)


def kernel(x, c, mod_w, mod_b, mix_norm_g, ffn_norm_g, ab_w_in, ab_w_out, moba_q_gain, moba_k_gain,
           fox_w_in, fox_b_f, fox_w_out, fox_q_gain, fox_k_gain, router_w, router_b,
           exp_w_gate, exp_b_gate, exp_w_up, exp_b_up, exp_w_down, exp_b_down):
    b, s, d = x.shape
    depth = mod_w.shape[0]
    n_heads = d // HEAD_DIM
    n_moba = n_heads // 2
    n_sb = n_heads - n_moba
    mod = _adaln_mod(c, mod_w, mod_b)

    for layer in range(depth):
        sh1, sc1, g1, sh2, sc2, g2 = [mod[layer, :, i * d:(i + 1) * d] for i in range(N_MOD)]
        j = layer // 2
        h = _norm_mod(x, mix_norm_g[layer], sc1, sh1)
        if layer % 2 == 0:
            wa = n_moba * HEAD_DIM
            wb = n_sb * HEAD_DIM
            col_scale = jnp.concatenate([
                jnp.tile(moba_q_gain[j] * Q_PRESCALE, n_moba), jnp.tile(moba_k_gain[j], n_moba),
                jnp.ones((wa,), F32), jnp.full((wb,), Q_PRESCALE, F32), jnp.ones((2 * wb,), F32)])
            n_cols = col_scale.shape[0]
            proj = _in_proj(h.reshape(b * s, d), ab_w_in[j].astype(BF16), col_scale.reshape(1, n_cols), 2 * wa)
            proj = proj.reshape(b, s, n_cols)
            o_a = _moba_attention(proj, n_moba, 0)
            o_b = _sb_attention(proj, n_sb, 3 * n_moba)
            w_out = ab_w_out[j].astype(BF16)
            x = _out_proj_residual(x, g1, [(o_a, w_out[:wa]), (o_b, w_out[wa:])])
        else:
            w = n_heads * HEAD_DIM
            col_scale = jnp.concatenate([jnp.tile(fox_q_gain[j] * Q_PRESCALE, n_heads),
                                         jnp.tile(fox_k_gain[j], n_heads), jnp.ones((w,), F32)])
            proj = _in_proj(h.reshape(b * s, d), fox_w_in[j, :, :3 * w].astype(BF16),
                            col_scale.reshape(1, 3 * w), 2 * w)
            cum_f = _forget_cumsum(h, fox_w_in[j, :, 3 * w:], fox_b_f[j])
            o = _fox_attention(proj.reshape(b, s, 3 * w), cum_f, n_heads)
            x = _out_proj_residual(x, g1, [(o, fox_w_out[j].astype(BF16))])
        h = _norm_mod(x, ffn_norm_g[layer], sc2, sh2, packed=True)
        x = _moe_ffn(x, h, g2, layer, router_w[layer], router_b[layer], exp_w_gate, exp_b_gate,
                     exp_w_up, exp_b_up, exp_w_down, exp_b_down)
    return x
```

```python
import functools
import math

import jax
import jax.numpy as jnp
from jax import lax
from jax.experimental import pallas as pl
from jax.experimental.pallas import tpu as pltpu

F32 = jnp.float32
BF16 = jnp.bfloat16

HEAD_DIM = 128
MOBA_BLOCK = 256
MOBA_TOPK = 3
TOP_K = 4
SWIGLU_LIMIT = 7.0
SWIGLU_ALPHA = 1.702
RMS_EPS = 1e-5
N_MOD = 6
LOG2E = math.log2(math.e)
Q_PRESCALE = HEAD_DIM ** -0.5 * LOG2E
LANES = 128
SUBLANES = 8
BF16_ROWS = 16
MASKED = -1e30
SB_EXIT_BITS = 160.0
ROW_DMA_UNROLL = 16
VMEM_LIMIT_BYTES = 56 * 1024 * 1024


def _params(*semantics):
    return pltpu.CompilerParams(dimension_semantics=semantics, vmem_limit_bytes=VMEM_LIMIT_BYTES)


def _dot(a, b):
    return jnp.dot(a, b, preferred_element_type=F32)


def _dot_nt(a, b):
    return lax.dot_general(a, b, (((1,), (1,)), ((), ())), preferred_element_type=F32)


def _split_bf16(x):
    hi = x.astype(BF16)
    lo = (x - hi.astype(F32)).astype(BF16)
    return hi, lo


def _kv_block(ref, j, tk):
    return ref[pl.ds(pl.multiple_of(j * tk, tk), tk), :]


def _pack_bf16_pair(lo, hi):
    lo_bits = lax.bitcast_convert_type(lo.astype(BF16).astype(F32), jnp.uint32)
    hi_bits = lax.bitcast_convert_type(hi.astype(BF16).astype(F32), jnp.uint32)
    return lax.shift_right_logical(lo_bits, jnp.uint32(16)) | (hi_bits & jnp.uint32(0xFFFF0000))


def _unpack_bf16_pair(word):
    lo = lax.bitcast_convert_type(lax.shift_left(word, jnp.uint32(16)), F32)
    hi = lax.bitcast_convert_type(word & jnp.uint32(0xFFFF0000), F32)
    return lo, hi


TOKEN_DIM = 2 * SUBLANES * LANES


def _store_token_tiles(ref, y, lead=()):
    n = y.shape[0]
    half = TOKEN_DIM // 2
    for s in range(SUBLANES):
        lo = y[:, s * LANES:(s + 1) * LANES]
        hi = y[:, half + s * LANES:half + (s + 1) * LANES]
        ref[lead + (pl.ds(s, n, stride=SUBLANES), slice(None))] = _pack_bf16_pair(lo, hi)


def _load_token_tiles(ref, n, lead=()):
    los, his = [], []
    for s in range(SUBLANES):
        lo, hi = _unpack_bf16_pair(ref[lead + (pl.ds(s, n, stride=SUBLANES), slice(None))])
        los.append(lo)
        his.append(hi)
    return los + his


def _load_token_tiles_bf16(ref, n, lead=()):
    return jnp.concatenate([c.astype(BF16) for c in _load_token_tiles(ref, n, lead)], axis=1)


def _mod_kernel(c_ref, w_ref, b_ref, o_ref):
    c = c_ref[...]
    c_act = c / (1.0 + jnp.exp(-c))
    o_ref[...] = _dot(c_act, w_ref[...]) + b_ref[...]


def _adaln_mod(c, mod_w, mod_b):
    depth, d, n = mod_w.shape
    b = c.shape[0]
    bp = -(-b // SUBLANES) * SUBLANES
    c_pad = jnp.zeros((bp, d), F32).at[:b].set(c)
    tn = min(n, 1024)
    out = pl.pallas_call(
        _mod_kernel,
        grid=(depth, n // tn),
        in_specs=[pl.BlockSpec((bp, d), lambda l, j: (0, 0)),
                  pl.BlockSpec((None, d, tn), lambda l, j: (l, 0, j)),
                  pl.BlockSpec((None, 1, tn), lambda l, j: (l, 0, j))],
        out_specs=pl.BlockSpec((None, bp, tn), lambda l, j: (l, 0, j)),
        out_shape=jax.ShapeDtypeStruct((depth, bp, n), F32),
        compiler_params=_params("arbitrary", "arbitrary"),
        name="adaln_mod",
    )(c_pad, mod_w, mod_b.reshape(depth, 1, n))
    return out[:, :b]


def _norm_kernel(x_ref, g_ref, sc_ref, sh_ref, o_ref, *, packed):
    x = x_ref[...]
    ms = jnp.mean(x * x, axis=-1, keepdims=True)
    y = x * lax.rsqrt(ms + RMS_EPS)
    y = (y * g_ref[...]) * (1.0 + sc_ref[...]) + sh_ref[...]
    if packed:
        _store_token_tiles(o_ref, y)
    else:
        o_ref[...] = y.astype(o_ref.dtype)


def _norm_mod(x, gain, scale, shift, packed=False):
    b, s, d = x.shape
    ts = min(s, 512)
    if packed:
        assert d == TOKEN_DIM
        out_block, out_shape = (None, ts * SUBLANES, LANES), (b, s * SUBLANES, LANES)
    else:
        out_block, out_shape = (None, ts, d), (b, s, d)
    return pl.pallas_call(
        functools.partial(_norm_kernel, packed=packed),
        grid=(b, s // ts),
        in_specs=[pl.BlockSpec((None, ts, d), lambda i, j: (i, j, 0)),
                  pl.BlockSpec((1, d), lambda i, j: (0, 0)),
                  pl.BlockSpec((None, 1, d), lambda i, j: (i, 0, 0)),
                  pl.BlockSpec((None, 1, d), lambda i, j: (i, 0, 0))],
        out_specs=pl.BlockSpec(out_block, lambda i, j: (i, j, 0)),
        out_shape=jax.ShapeDtypeStruct(out_shape, jnp.uint32 if packed else BF16),
        compiler_params=_params("arbitrary", "arbitrary"),
        name="norm_mod",
    )(x, gain.reshape(1, d), scale.reshape(b, 1, d), shift.reshape(b, 1, d))


def _inproj_kernel(h_ref, w_ref, gain_ref, o_ref, *, n_norm_tiles):
    acc = _dot(h_ref[...], w_ref[...])
    j = pl.program_id(0)

    @pl.when(j < n_norm_tiles)
    def _():
        for g in range(acc.shape[1] // HEAD_DIM):
            sl = slice(g * HEAD_DIM, (g + 1) * HEAD_DIM)
            blk = acc[:, sl]
            ms = jnp.mean(blk * blk, axis=-1, keepdims=True)
            o_ref[:, sl] = (blk * lax.rsqrt(ms + RMS_EPS) * gain_ref[:, sl]).astype(o_ref.dtype)

    @pl.when(j >= n_norm_tiles)
    def _():
        o_ref[...] = (acc * gain_ref[...]).astype(o_ref.dtype)


def _in_proj(h, w, col_scale, n_norm_cols):
    t, d = h.shape
    n = w.shape[1]
    tm = min(t, 1024)
    tn = min(n, 1024)
    assert n_norm_cols % tn == 0 and n % tn == 0 and t % tm == 0
    return pl.pallas_call(
        functools.partial(_inproj_kernel, n_norm_tiles=n_norm_cols // tn),
        grid=(n // tn, t // tm),
        in_specs=[pl.BlockSpec((tm, d), lambda j, i: (i, 0)),
                  pl.BlockSpec((d, tn), lambda j, i: (0, j)),
                  pl.BlockSpec((1, tn), lambda j, i: (0, j))],
        out_specs=pl.BlockSpec((tm, tn), lambda j, i: (i, j)),
        out_shape=jax.ShapeDtypeStruct((t, n), BF16),
        compiler_params=_params("arbitrary", "arbitrary"),
        name="in_proj",
    )(h, w, col_scale)


def _fgate_kernel(h_ref, wf_ref, bf_ref, o_ref, carry_ref):
    @pl.when(pl.program_id(1) == 0)
    def _():
        carry_ref[...] = jnp.zeros_like(carry_ref)

    ts = h_ref.shape[0]
    logit = _dot(h_ref[...], wf_ref[...]) + bf_ref[...]
    log_f = jnp.minimum(logit, 0.0) - jnp.log(1.0 + jnp.exp(-jnp.abs(logit)))
    row = lax.broadcasted_iota(jnp.int32, (ts, ts), 0)
    col = lax.broadcasted_iota(jnp.int32, (ts, ts), 1)
    lower = jnp.where(col <= row, 1.0, 0.0).astype(BF16)
    hi, lo = _split_bf16(log_f)
    cum = _dot(lower, hi) + _dot(lower, lo) + carry_ref[0:1, :]
    o_ref[...] = cum
    carry_ref[...] = jnp.broadcast_to(cum[ts - 1:ts, :], carry_ref.shape)


def _forget_cumsum(h, w_f, b_f):
    b, s, d = h.shape
    nh = w_f.shape[1]
    assert nh <= LANES
    ts = min(s, 512)
    w_pad = jnp.zeros((d, LANES), BF16).at[:, :nh].set(w_f.astype(BF16))
    b_pad = jnp.zeros((1, LANES), F32).at[0, :nh].set(b_f)
    return pl.pallas_call(
        _fgate_kernel,
        grid=(b, s // ts),
        in_specs=[pl.BlockSpec((None, ts, d), lambda i, j: (i, j, 0)),
                  pl.BlockSpec((d, LANES), lambda i, j: (0, 0)),
                  pl.BlockSpec((1, LANES), lambda i, j: (0, 0))],
        out_specs=pl.BlockSpec((None, ts, LANES), lambda i, j: (i, j, 0)),
        out_shape=jax.ShapeDtypeStruct((b, s, LANES), F32),
        scratch_shapes=[pltpu.VMEM((SUBLANES, LANES), F32)],
        compiler_params=_params("arbitrary", "arbitrary"),
        name="forget_cumsum",
    )(h, w_pad, b_pad)


VT_ROWS = HEAD_DIM + BF16_ROWS


def _store_v_transposed(v_ref, vt_ref, c, tile):
    vb = _kv_block(v_ref, c, tile).astype(F32)
    vt_ref[c, 0:HEAD_DIM, :] = vb.T.astype(BF16)
    vt_ref[c, HEAD_DIM:VT_ROWS, :] = jnp.ones((BF16_ROWS, tile), BF16)


def _softmax_accumulate(s, vt_blk, m_ref, acc_ref):
    m_old = m_ref[...]
    m_new = jnp.maximum(m_old, jnp.max(s, axis=0, keepdims=True))
    alpha = jnp.exp2(m_old - m_new)
    p = jnp.exp2((s - m_new).astype(BF16))
    m_ref[...] = m_new
    acc_ref[...] = alpha * acc_ref[...] + _dot(vt_blk, p)


def _softmax_finish(acc_ref, o_ref, qi, tile):
    acc = acc_ref[...]
    out_t = acc[0:HEAD_DIM, :] / acc[HEAD_DIM:HEAD_DIM + 1, :]
    o_ref[pl.ds(pl.multiple_of(qi * tile, tile), tile), :] = out_t.T.astype(o_ref.dtype)


def _flash_sweep(qi, n_past, produce, s0_ref, s1_ref, vt_ref, m_ref, acc_ref, tile):
    m_ref[...] = jnp.full(m_ref.shape, MASKED, F32)
    acc_ref[...] = jnp.zeros_like(acc_ref)
    key = lax.broadcasted_iota(jnp.int32, (tile, tile), 0)
    qry = lax.broadcasted_iota(jnp.int32, (tile, tile), 1)
    s0_ref[...] = jnp.where(key <= qry, produce(qi), MASKED)
    n_pos = n_past + 1
    farthest = qi - n_past

    def pair(i):
        blk = qi - 2 * i
        s1_ref[...] = produce(blk - 1)
        _softmax_accumulate(s0_ref[...], vt_ref[blk], m_ref, acc_ref)
        s0_ref[...] = produce(jnp.maximum(blk - 2, farthest))
        _softmax_accumulate(s1_ref[...], vt_ref[blk - 1], m_ref, acc_ref)

    def two_pairs(i, carry):
        pair(2 * i)
        pair(2 * i + 1)
        return carry

    n_pairs = n_pos // 2
    lax.fori_loop(0, n_pairs // 2, two_pairs, 0)

    @pl.when(lax.rem(n_pairs, 2) == 1)
    def _():
        pair(n_pairs - 1)

    @pl.when(lax.rem(n_pos, 2) == 1)
    def _():
        _softmax_accumulate(s0_ref[...], vt_ref[farthest], m_ref, acc_ref)


SKIP_BITS = 162.0


def _logit_spread_bound(q, kmax_sq_ref):
    q32 = q.astype(F32)
    qmax_sq = jnp.max(jnp.sum(q32 * q32, axis=-1, keepdims=True), axis=0, keepdims=True)
    return 2.0 * jnp.sqrt(qmax_sq * kmax_sq_ref[0:1, 0:1])


def _update_kmax_sq(kmax_sq_ref, kb):
    blk_max = jnp.max(jnp.sum(kb * kb, axis=-1, keepdims=True), axis=0, keepdims=True)
    kmax_sq_ref[...] = jnp.maximum(kmax_sq_ref[...], jnp.broadcast_to(blk_max, kmax_sq_ref.shape))


def _fox_kernel(q_ref, k_ref, v_ref, f_ref, o_ref, vt_ref, frep_ref, fend_ref, kmax_sq_ref, s0_ref, s1_ref, m_ref,
                acc_ref, *, tile, n_kv):
    head = pl.program_id(1)
    lane = lax.broadcasted_iota(jnp.int32, (tile, LANES), 1)
    fend_ref[...] = jnp.zeros_like(fend_ref)
    kmax_sq_ref[...] = jnp.zeros_like(kmax_sq_ref)

    def prep(c, carry):
        _store_v_transposed(v_ref, vt_ref, c, tile)
        _update_kmax_sq(kmax_sq_ref, _kv_block(k_ref, c, tile).astype(F32))
        f_blk = _kv_block(f_ref, c, tile)
        f_col = jnp.sum(jnp.where(lane == head, f_blk, 0.0), axis=-1, keepdims=True)
        f_rep = jnp.broadcast_to(f_col * LOG2E, (tile, LANES))
        frep_ref[pl.ds(pl.multiple_of(c * tile, tile), tile), :] = f_rep
        fend_ref[pl.ds(c, 1), :] = f_rep[tile - 1:tile, :]
        return carry

    lax.fori_loop(0, n_kv, prep, 0)

    def query_tile(qi, carry):
        qi = jnp.asarray(qi, jnp.int32)
        q = _kv_block(q_ref, qi, tile)
        f_first = frep_ref[pl.ds(pl.multiple_of(qi * tile, tile), 1), :]
        gap = fend_ref[...] - f_first
        block_id = lax.broadcasted_iota(jnp.int32, fend_ref.shape, 0)
        needed = jnp.logical_and(block_id < qi, gap <= _logit_spread_bound(q, kmax_sq_ref) + SKIP_BITS)
        n_past = jnp.sum(jnp.where(needed, 1.0, 0.0)[:, 0:1]).astype(jnp.int32)

        def produce(j):
            f_rep = _kv_block(frep_ref, j, tile)
            return _dot_nt(_kv_block(k_ref, j, tile), q) - jnp.concatenate([f_rep] * (tile // LANES), axis=1)

        _flash_sweep(qi, n_past, produce, s0_ref, s1_ref, vt_ref, m_ref, acc_ref, tile)
        _softmax_finish(acc_ref, o_ref, qi, tile)
        return carry

    lax.fori_loop(0, n_kv, query_tile, 0)


def _fox_attention(proj, cum_f, n_heads):
    b, s, _ = proj.shape
    tile = min(s, 512)
    n_kv = s // tile
    return pl.pallas_call(
        functools.partial(_fox_kernel, tile=tile, n_kv=n_kv),
        grid=(b, n_heads),
        in_specs=[pl.BlockSpec((None, s, HEAD_DIM), lambda i, h: (i, 0, h)),
                  pl.BlockSpec((None, s, HEAD_DIM), lambda i, h: (i, 0, n_heads + h)),
                  pl.BlockSpec((None, s, HEAD_DIM), lambda i, h: (i, 0, 2 * n_heads + h)),
                  pl.BlockSpec((None, s, LANES), lambda i, h: (i, 0, 0))],
        out_specs=pl.BlockSpec((None, s, HEAD_DIM), lambda i, h: (i, 0, h)),
        out_shape=jax.ShapeDtypeStruct((b, s, n_heads * HEAD_DIM), BF16),
        scratch_shapes=[pltpu.VMEM((n_kv, VT_ROWS, tile), BF16), pltpu.VMEM((s, LANES), F32),
                        pltpu.VMEM((-(-n_kv // SUBLANES) * SUBLANES, LANES), F32),
                        pltpu.VMEM((SUBLANES, LANES), F32),
                        pltpu.VMEM((tile, tile), F32), pltpu.VMEM((tile, tile), F32),
                        pltpu.VMEM((1, tile), F32),
                        pltpu.VMEM((VT_ROWS, tile), F32)],
        compiler_params=_params("arbitrary", "arbitrary"),
        name="fox_attention",
    )(proj, proj, proj, cum_f)


def _moba_kernel(q_ref, k_ref, v_ref, slope_ref, o_ref, kmean_ref, kmax_sq_ref, vt_ref, sel_ref, alibi_ref,
                 s0_ref, s1_ref, m_ref, acc_ref, *, tile, n_blocks):
    blk = MOBA_BLOCK
    per_tile = tile // blk
    kmean_ref[...] = jnp.zeros_like(kmean_ref)
    kmax_sq_ref[...] = jnp.zeros_like(kmax_sq_ref)

    def block_mean(n, carry):
        kb = _kv_block(k_ref, n, blk).astype(F32)
        kmean_ref[pl.ds(n, 1), :] = jnp.mean(kb, axis=0, keepdims=True)
        _update_kmax_sq(kmax_sq_ref, kb)
        return carry

    def transpose_v(c, carry):
        _store_v_transposed(v_ref, vt_ref, c, tile)
        return carry

    lax.fori_loop(0, n_blocks, block_mean, 0)
    lax.fori_loop(0, n_blocks // per_tile, transpose_v, 0)

    slope = slope_ref[...] * LOG2E
    alibi_ref[...] = slope * lax.broadcasted_iota(jnp.int32, (tile, tile), 0).astype(F32)
    block_id = lax.broadcasted_iota(jnp.int32, (LANES, tile), 0).astype(F32)
    qry_block = jnp.floor(lax.broadcasted_iota(jnp.int32, (LANES, tile), 1).astype(F32) * (1.0 / blk))

    def query_tile(qi, carry):
        qi = jnp.asarray(qi, jnp.int32)
        q = _kv_block(q_ref, qi, tile)
        own = jnp.asarray(qi * per_tile, F32) + qry_block
        gate = _dot_nt(kmean_ref[...].astype(BF16), q)
        gate = jnp.where(block_id < own, gate, -jnp.inf)
        sel = jnp.where(block_id == own, 1.0, 0.0)
        for slot in range(MOBA_TOPK):
            best = jnp.max(gate, axis=0, keepdims=True)
            idx = jnp.min(jnp.where(gate == best, block_id, float(LANES)), axis=0, keepdims=True)
            hit = block_id == idx
            slot_ok = jnp.where(float(slot) < own, 1.0, 0.0)
            sel = jnp.maximum(sel, jnp.where(hit, slot_ok, 0.0))
            gate = jnp.where(hit, -jnp.inf, gate)
        sel_ref[...] = jnp.where(sel > 0.5, 0.0, MASKED)

        def produce(j):
            shift = slope * jnp.asarray((j - qi) * tile, F32)
            per_query = jnp.concatenate(
                [jnp.broadcast_to(sel_ref[pl.ds(j * per_tile + r, 1), :] + shift, (blk, tile))
                 for r in range(per_tile)], axis=0)
            return (_dot_nt(_kv_block(k_ref, j, tile), q) + alibi_ref[...]) + per_query

        reach = (_logit_spread_bound(q, kmax_sq_ref) + SKIP_BITS) / slope
        tiles_back = jnp.floor((reach - 1.0) * (1.0 / tile)) + 2.0
        n_past = jnp.minimum(jnp.minimum(jnp.max(tiles_back), float(n_blocks)).astype(jnp.int32), qi)

        _flash_sweep(qi, n_past, produce, s0_ref, s1_ref, vt_ref, m_ref, acc_ref, tile)
        _softmax_finish(acc_ref, o_ref, qi, tile)
        return carry

    lax.fori_loop(0, n_blocks // per_tile, query_tile, 0)


def _moba_attention(proj, n_heads, col_block):
    b, s, _ = proj.shape
    blk = MOBA_BLOCK
    n_blocks = s // blk
    tile = min(s, 2 * blk)
    assert s % tile == 0 and tile % blk == 0 and n_blocks <= LANES
    slopes = jnp.asarray([2.0 ** (-8.0 * (i + 1) / n_heads) for i in range(n_heads)], F32)
    slope_rows = jnp.broadcast_to(slopes[:, None, None], (n_heads, 1, tile))
    return pl.pallas_call(
        functools.partial(_moba_kernel, tile=tile, n_blocks=n_blocks),
        grid=(b, n_heads),
        in_specs=[pl.BlockSpec((None, s, HEAD_DIM), lambda i, h: (i, 0, col_block + h)),
                  pl.BlockSpec((None, s, HEAD_DIM), lambda i, h: (i, 0, col_block + n_heads + h)),
                  pl.BlockSpec((None, s, HEAD_DIM), lambda i, h: (i, 0, col_block + 2 * n_heads + h)),
                  pl.BlockSpec((None, 1, tile), lambda i, h: (h, 0, 0))],
        out_specs=pl.BlockSpec((None, s, HEAD_DIM), lambda i, h: (i, 0, h)),
        out_shape=jax.ShapeDtypeStruct((b, s, n_heads * HEAD_DIM), BF16),
        scratch_shapes=[pltpu.VMEM((LANES, HEAD_DIM), F32), pltpu.VMEM((SUBLANES, LANES), F32),
                        pltpu.VMEM((s // tile, VT_ROWS, tile), BF16),
                        pltpu.VMEM((LANES, tile), F32), pltpu.VMEM((tile, tile), F32),
                        pltpu.VMEM((tile, tile), F32), pltpu.VMEM((tile, tile), F32),
                        pltpu.VMEM((1, tile), F32), pltpu.VMEM((VT_ROWS, tile), F32)],
        compiler_params=_params("arbitrary", "arbitrary"),
        name="moba_attention",
    )(proj, proj, proj, slope_rows)


def _sb_kernel(q_ref, k_ref, v_ref, o_ref, after_ref, decay_ref, acc_ref, *, tile, n_q):
    row = lax.broadcasted_iota(jnp.int32, (tile, tile), 0)
    col = lax.broadcasted_iota(jnp.int32, (tile, tile), 1)
    after_ref[...] = jnp.where(row > col, 1.0, 0.0).astype(BF16)

    def query_tile(qi, carry):
        qi = jnp.asarray(qi, jnp.int32)
        q = _kv_block(q_ref, qi, tile)
        causal = col < row

        def logits_and_drop(j, masked):
            z = _dot_nt(q, _kv_block(k_ref, j, tile))
            drop = jnp.maximum(z, 0.0) + jnp.log2(1.0 + jnp.exp2(-jnp.abs(z)))
            if masked:
                drop = jnp.where(causal, drop, 0.0)
            return z, drop

        def weighted_values(j, z, drop, decay_right, masked):
            hi, lo = _split_bf16(drop)
            after = after_ref[...]
            decay = _dot(hi, after) + _dot(lo, after) + decay_right
            a = jnp.exp2(z - drop - decay)
            if masked:
                a = jnp.where(causal, a, 0.0)
            return _dot(a.astype(BF16), _kv_block(v_ref, j, tile))

        @pl.when(qi == 0)
        def _():
            z, drop = logits_and_drop(qi, True)
            acc_ref[...] = weighted_values(qi, z, drop, 0.0, True)
            decay_ref[...] = jnp.sum(drop, axis=-1, keepdims=True)

        @pl.when(qi > 0)
        def _():
            z_d, drop_d = logits_and_drop(qi, True)
            z_p, drop_p = logits_and_drop(qi - 1, False)
            decay_d = jnp.sum(drop_d, axis=-1, keepdims=True)
            acc_ref[...] = (weighted_values(qi, z_d, drop_d, 0.0, True)
                            + weighted_values(qi - 1, z_p, drop_p, decay_d, False))
            decay_ref[...] = decay_d + jnp.sum(drop_p, axis=-1, keepdims=True)

        def cond(state):
            jj, least = state
            return jnp.logical_and(jj < qi, least < SB_EXIT_BITS)

        def body(state):
            jj, _ = state
            j = qi - 1 - jj
            z, drop = logits_and_drop(j, False)
            acc_ref[...] += weighted_values(j, z, drop, decay_ref[...], False)
            decay_ref[...] += jnp.sum(drop, axis=-1, keepdims=True)
            return jj + 1, jnp.min(decay_ref[...])

        lax.while_loop(cond, body, (jnp.int32(1), jnp.min(decay_ref[...])))
        o_ref[pl.ds(pl.multiple_of(qi * tile, tile), tile), :] = acc_ref[...].astype(o_ref.dtype)
        return carry

    lax.fori_loop(0, n_q, query_tile, 0)


def _sb_attention(proj, n_heads, col_block):
    b, s, _ = proj.shape
    tile = min(s, 256)
    return pl.pallas_call(
        functools.partial(_sb_kernel, tile=tile, n_q=s // tile),
        grid=(b, n_heads),
        in_specs=[pl.BlockSpec((None, s, HEAD_DIM), lambda i, h: (i, 0, col_block + h)),
                  pl.BlockSpec((None, s, HEAD_DIM), lambda i, h: (i, 0, col_block + n_heads + h)),
                  pl.BlockSpec((None, s, HEAD_DIM), lambda i, h: (i, 0, col_block + 2 * n_heads + h))],
        out_specs=pl.BlockSpec((None, s, HEAD_DIM), lambda i, h: (i, 0, h)),
        out_shape=jax.ShapeDtypeStruct((b, s, n_heads * HEAD_DIM), BF16),
        scratch_shapes=[pltpu.VMEM((tile, tile), BF16), pltpu.VMEM((tile, 1), F32),
                        pltpu.VMEM((tile, HEAD_DIM), F32)],
        compiler_params=_params("arbitrary", "arbitrary"),
        name="sb_attention",
    )(proj, proj, proj)


def _outproj_kernel(*refs, n_parts):
    x_ref, gate_ref = refs[0], refs[1]
    parts = refs[2:2 + 2 * n_parts]
    o_ref = refs[2 + 2 * n_parts]
    y = _dot(parts[0][...], parts[1][...])
    for p in range(1, n_parts):
        y += _dot(parts[2 * p][...], parts[2 * p + 1][...])
    o_ref[...] = x_ref[...] + gate_ref[...] * y


def _out_proj_residual(x, gate, parts):
    b, s, d = x.shape
    ts = min(s, 512)
    in_specs = [pl.BlockSpec((None, ts, d), lambda i, j: (i, j, 0)),
                pl.BlockSpec((None, 1, d), lambda i, j: (i, 0, 0))]
    args = [x, gate.reshape(b, 1, d)]
    for a, w in parts:
        kp = a.shape[-1]
        in_specs += [pl.BlockSpec((None, ts, kp), lambda i, j: (i, j, 0)),
                     pl.BlockSpec((kp, d), lambda i, j: (0, 0))]
        args += [a, w]
    return pl.pallas_call(
        functools.partial(_outproj_kernel, n_parts=len(parts)),
        grid=(b, s // ts),
        in_specs=in_specs,
        out_specs=pl.BlockSpec((None, ts, d), lambda i, j: (i, j, 0)),
        out_shape=jax.ShapeDtypeStruct((b, s, d), F32),
        compiler_params=_params("arbitrary", "arbitrary"),
        name="out_proj",
    )(*args)


def _router_kernel(h_ref, w_ref, b_ref, idx_ref, wgt_ref, rank_ref, cnt_ref, carry_ref):
    @pl.when(pl.program_id(0) == 0)
    def _():
        carry_ref[...] = jnp.zeros_like(carry_ref)

    tm = h_ref.shape[0] // SUBLANES
    logits = _dot(_load_token_tiles_bf16(h_ref, tm), w_ref[...]) + b_ref[...]
    lane = lax.broadcasted_iota(jnp.int32, (tm, LANES), 1).astype(F32)
    vals, idxs = [], []
    onehot = jnp.zeros((tm, LANES), F32)
    for _ in range(TOP_K):
        best = jnp.max(logits, axis=-1, keepdims=True)
        idx = jnp.min(jnp.where(logits == best, lane, float(LANES)), axis=-1, keepdims=True)
        hit = lane == idx
        onehot = jnp.where(hit, 1.0, onehot)
        logits = jnp.where(hit, -jnp.inf, logits)
        vals.append(best)
        idxs.append(idx)
    exps = [jnp.exp(v - vals[0]) for v in vals]
    denom = exps[0]
    for e in exps[1:]:
        denom = denom + e

    row = lax.broadcasted_iota(jnp.int32, (tm, tm), 0)
    col = lax.broadcasted_iota(jnp.int32, (tm, tm), 1)
    before = jnp.where(col < row, 1.0, 0.0).astype(BF16)
    prior = _dot(before, onehot.astype(BF16)) + carry_ref[0:1, :]

    idx_out = jnp.zeros((tm, LANES), F32)
    wgt_out = jnp.zeros((tm, LANES), F32)
    rank_out = jnp.zeros((tm, LANES), F32)
    for k in range(TOP_K):
        rank_k = jnp.sum(jnp.where(lane == idxs[k], prior, 0.0), axis=-1, keepdims=True)
        slot = lane == float(k)
        idx_out = jnp.where(slot, idxs[k], idx_out)
        wgt_out = jnp.where(slot, exps[k] / denom, wgt_out)
        rank_out = jnp.where(slot, rank_k, rank_out)
    idx_ref[...] = idx_out.astype(jnp.int32)
    wgt_ref[...] = wgt_out
    rank_ref[...] = rank_out.astype(jnp.int32)
    counts = carry_ref[...] + jnp.sum(onehot, axis=0, keepdims=True)
    carry_ref[...] = counts
    cnt_ref[...] = counts.astype(jnp.int32)


def _route(h, w_router, b_router):
    t = h.shape[0] // SUBLANES
    d = TOKEN_DIM
    e = w_router.shape[1]
    assert e <= LANES
    tm = min(t, 512)
    w_pad = jnp.zeros((d, LANES), BF16).at[:, :e].set(w_router.astype(BF16))
    b_pad = jnp.full((1, LANES), MASKED, F32).at[0, :e].set(b_router)
    tok_spec = pl.BlockSpec((tm, LANES), lambda i: (i, 0))
    return pl.pallas_call(
        _router_kernel,
        grid=(t // tm,),
        in_specs=[pl.BlockSpec((tm * SUBLANES, LANES), lambda i: (i, 0)),
                  pl.BlockSpec((d, LANES), lambda i: (0, 0)),
                  pl.BlockSpec((1, LANES), lambda i: (0, 0))],
        out_specs=[tok_spec, tok_spec, tok_spec, pl.BlockSpec((SUBLANES, LANES), lambda i: (0, 0))],
        out_shape=[jax.ShapeDtypeStruct((t, LANES), jnp.int32), jax.ShapeDtypeStruct((t, LANES), F32),
                   jax.ShapeDtypeStruct((t, LANES), jnp.int32),
                   jax.ShapeDtypeStruct((SUBLANES, LANES), jnp.int32)],
        scratch_shapes=[pltpu.VMEM((SUBLANES, LANES), F32)],
        compiler_params=_params("arbitrary"),
        name="moe_router",
    )(h, w_pad, b_pad)


def _start_rows(n_rows, copy_of_row):
    def body(p, carry):
        copy_of_row(2 * p).start(priority=0)
        copy_of_row(2 * p + 1).start(priority=1)
        return carry

    lax.fori_loop(0, n_rows // 2, body, 0, unroll=ROW_DMA_UNROLL // 2)


def _wait_rows(n_rows, copy_of_row):
    def body(r, carry):
        copy_of_row(r).wait()
        return carry

    lax.fori_loop(0, n_rows, body, 0, unroll=ROW_DMA_UNROLL)


def _moe_dispatch_kernel(padstart_ref, padlen_ref, tail_ref, pos_ref, h_ref, xs_hbm, zeros, sem, *, n_experts, tm):
    i = pl.program_id(0)
    n_tokens = h_ref.shape[0] // SUBLANES

    def zero_block(first_slot, n_slots):
        return pltpu.make_async_copy(
            zeros.at[pl.ds(0, n_slots * SUBLANES)],
            xs_hbm.at[pl.ds(pl.multiple_of(first_slot * SUBLANES, SUBLANES), n_slots * SUBLANES)], sem.at[0])

    def pad_blocks(e, fn):
        first = padstart_ref[e]
        size = tm // 2
        while size >= 1:
            used = (padlen_ref[e] & size) != 0

            @pl.when(used)
            def _(first=first, size=size):
                fn(zero_block(first, size))

            first = first + jnp.where(used, size, 0)
            size //= 2

    def tail_blocks(fn):
        def body(j, carry):
            fn(zero_block(tail_ref[0] + j * tm, tm))
            return carry

        lax.fori_loop(0, tail_ref[1], body, 0)

    @pl.when(i == 0)
    def _():
        zeros[...] = jnp.zeros_like(zeros)

        def start_expert(e, carry):
            pad_blocks(e, lambda copy: copy.start())
            return carry

        def wait_expert(e, carry):
            pad_blocks(e, lambda copy: copy.wait())
            return carry

        lax.fori_loop(0, n_experts, start_expert, 0)
        tail_blocks(lambda copy: copy.start())
        lax.fori_loop(0, n_experts, wait_expert, 0)
        tail_blocks(lambda copy: copy.wait())

    def pair_copy(token, k):
        src = h_ref.at[pl.ds(pl.multiple_of(token * SUBLANES, SUBLANES), SUBLANES)]
        dst = xs_hbm.at[pl.ds(pl.multiple_of(pos_ref[0, token * TOP_K + k], SUBLANES), SUBLANES)]
        return pltpu.make_async_copy(src, dst, sem.at[1])

    def start_token(token, carry):
        for k in range(TOP_K):
            pair_copy(token, k).start(priority=k % 2)
        return carry

    def wait_token(token, carry):
        for k in range(TOP_K):
            pair_copy(token, k).wait()
        return carry

    lax.fori_loop(0, n_tokens, start_token, 0, unroll=ROW_DMA_UNROLL // TOP_K)
    lax.fori_loop(0, n_tokens, wait_token, 0, unroll=ROW_DMA_UNROLL // TOP_K)


def _moe_dispatch(h_packed, pair_row, pad_start, pad_len, tail, n_slots, tm):
    t = h_packed.shape[0] // SUBLANES
    tt = min(t, 512)
    return pl.pallas_call(
        functools.partial(_moe_dispatch_kernel, n_experts=pad_start.shape[0], tm=tm),
        grid_spec=pltpu.PrefetchScalarGridSpec(
            num_scalar_prefetch=3, grid=(t // tt,),
            in_specs=[pl.BlockSpec((None, 1, tt * TOP_K), lambda i, *_: (i, 0, 0), memory_space=pltpu.SMEM),
                      pl.BlockSpec((tt * SUBLANES, LANES), lambda i, *_: (i, 0))],
            out_specs=pl.BlockSpec(memory_space=pl.ANY),
            scratch_shapes=[pltpu.VMEM((tm * SUBLANES, LANES), jnp.uint32), pltpu.SemaphoreType.DMA((2,))]),
        out_shape=jax.ShapeDtypeStruct((n_slots * SUBLANES, LANES), jnp.uint32),
        compiler_params=_params("arbitrary"),
        name="moe_dispatch",
    )(pad_start, pad_len, tail, pair_row.reshape(t // tt, 1, tt * TOP_K), h_packed)


def _moe_up_kernel(te_ref, first_ref, nvalid_ref, x_ref, wg_ref, bg_ref, wu_ref, bu_ref, o_ref, wg_bf, wu_bf):
    i = pl.program_id(0)
    n_valid = nvalid_ref[0]
    tm = x_ref.shape[0] // SUBLANES

    @pl.when(first_ref[i] == 1)
    def _():
        wg_bf[...] = wg_ref[...].astype(BF16)
        wu_bf[...] = wu_ref[...].astype(BF16)

    @pl.when(i < n_valid)
    def _():
        x = _load_token_tiles_bf16(x_ref, tm)
        g = jnp.minimum(_dot(x, wg_bf[...]) + bg_ref[...], SWIGLU_LIMIT)
        u = jnp.clip(_dot(x, wu_bf[...]) + bu_ref[...], -SWIGLU_LIMIT, SWIGLU_LIMIT)
        act = (u + 1.0) * (g / (1.0 + jnp.exp(-SWIGLU_ALPHA * g)))
        o_ref[...] = act.astype(o_ref.dtype)

    @pl.when(i >= n_valid)
    def _():
        o_ref[...] = jnp.zeros_like(o_ref)


def _moe_down_kernel(te_ref, first_ref, nvalid_ref, dst_ref, dst_prev_ref, a_ref, wd_ref, bd_ref, y_hbm,
                     obuf, sem, wd_bf, *, n_tiles, spare_row0):
    i = pl.program_id(0)
    slot = lax.rem(i, 2)
    n_valid = nvalid_ref[0]
    tm = obuf.shape[1] // SUBLANES

    def row_copy(dst_smem, buf, r):
        src = obuf.at[buf, pl.ds(pl.multiple_of(r * SUBLANES, SUBLANES), SUBLANES)]
        dst = y_hbm.at[pl.ds(pl.multiple_of(dst_smem[0, r], SUBLANES), SUBLANES)]
        return pltpu.make_async_copy(src, dst, sem.at[buf])

    @pl.when(i == 0)
    def _():
        obuf[...] = jnp.zeros_like(obuf)
        rows = tm * SUBLANES
        fills = [pltpu.make_async_copy(obuf.at[half], y_hbm.at[pl.ds(spare_row0 + half * rows, rows)], sem.at[half])
                 for half in range(2)]
        for fill in fills:
            fill.start()
        for fill in fills:
            fill.wait()

    @pl.when(first_ref[i] == 1)
    def _():
        wd_bf[...] = wd_ref[...].astype(BF16)

    @pl.when(i < n_valid)
    def _():
        _store_token_tiles(obuf, _dot(a_ref[...], wd_bf[...]) + bd_ref[...], lead=(slot,))
        _start_rows(tm, lambda r: row_copy(dst_ref, slot, r))

    @pl.when(jnp.logical_and(i >= 1, i - 1 < n_valid))
    def _():
        _wait_rows(tm, lambda r: row_copy(dst_prev_ref, 1 - slot, r))

    @pl.when(jnp.logical_and(i == n_tiles - 1, i < n_valid))
    def _():
        _wait_rows(tm, lambda r: row_copy(dst_ref, slot, r))


def _expert_spec(layer, shape):
    return pl.BlockSpec((None, None) + shape, lambda i, te, first, nvalid: (layer, te[i], 0, 0))


def _moe_experts(xs, slot_dst, n_out_rows, tile_expert, tile_first, n_valid, layer,
                 w_gate, b_gate, w_up, b_up, w_down, b_down, tm):
    d = TOKEN_DIM
    buf_shape = (2, tm * SUBLANES, LANES)
    depth, e, _, f = w_gate.shape
    n_tiles = slot_dst.shape[0]
    smem_rows = lambda index: pl.BlockSpec((None, 1, tm), index, memory_space=pltpu.SMEM)
    row_spec = lambda width: pl.BlockSpec((tm, width), lambda i, te, first, nvalid: (i, 0))
    act = pl.pallas_call(
        _moe_up_kernel,
        grid_spec=pltpu.PrefetchScalarGridSpec(
            num_scalar_prefetch=3, grid=(n_tiles,),
            in_specs=[pl.BlockSpec((tm * SUBLANES, LANES), lambda i, te, first, nvalid: (i, 0)),
                      _expert_spec(layer, (d, f)), _expert_spec(layer, (1, f)),
                      _expert_spec(layer, (d, f)), _expert_spec(layer, (1, f))],
            out_specs=row_spec(f),
            scratch_shapes=[pltpu.VMEM((d, f), BF16), pltpu.VMEM((d, f), BF16)]),
        out_shape=jax.ShapeDtypeStruct((n_tiles * tm, f), BF16),
        compiler_params=_params("arbitrary"),
        name="moe_up",
    )(tile_expert, tile_first, n_valid, xs, w_gate, b_gate.reshape(depth, e, 1, f), w_up, b_up.reshape(depth, e, 1, f))
    return pl.pallas_call(
        functools.partial(_moe_down_kernel, n_tiles=n_tiles, spare_row0=(n_out_rows - 2 * tm) * SUBLANES),
        grid_spec=pltpu.PrefetchScalarGridSpec(
            num_scalar_prefetch=3, grid=(n_tiles,),
            in_specs=[smem_rows(lambda i, te, first, nvalid: (i, 0, 0)),
                      smem_rows(lambda i, te, first, nvalid: (jnp.maximum(i - 1, 0), 0, 0)),
                      row_spec(f), _expert_spec(layer, (f, d)), _expert_spec(layer, (1, d))],
            out_specs=pl.BlockSpec(memory_space=pl.ANY),
            scratch_shapes=[pltpu.VMEM(buf_shape, jnp.uint32), pltpu.SemaphoreType.DMA((2,)),
                            pltpu.VMEM((f, d), BF16)]),
        out_shape=jax.ShapeDtypeStruct((n_out_rows * SUBLANES, LANES), jnp.uint32),
        compiler_params=_params("arbitrary"),
        name="moe_down",
    )(tile_expert, tile_first, n_valid, slot_dst, slot_dst, act, w_down, b_down.reshape(depth, e, 1, d))


def _combine_kernel(*refs):
    x_ref, gate_ref, w_ref = refs[0], refs[1], refs[2]
    y_refs = refs[3:3 + TOP_K]
    o_ref = refs[3 + TOP_K]
    ts = x_ref.shape[0]
    w = w_ref[...]
    cols = None
    for k in range(TOP_K):
        chunks = [w[:, k:k + 1] * c for c in _load_token_tiles(y_refs[k], ts)]
        cols = chunks if cols is None else [a + c for a, c in zip(cols, chunks)]
    o_ref[...] = x_ref[...] + gate_ref[...] * jnp.concatenate(cols, axis=1)


def _moe_combine(x, gate, y, weights):
    b, s, d = x.shape
    ts = min(s, 256)
    per_b = s // ts
    tiles_per_k = b * per_b
    y_spec = lambda k: pl.BlockSpec((ts * SUBLANES, LANES), lambda i, j: (k * tiles_per_k + i * per_b + j, 0))
    return pl.pallas_call(
        _combine_kernel,
        grid=(b, per_b),
        in_specs=[pl.BlockSpec((None, ts, d), lambda i, j: (i, j, 0)),
                  pl.BlockSpec((None, 1, d), lambda i, j: (i, 0, 0)),
                  pl.BlockSpec((ts, LANES), lambda i, j: (i * per_b + j, 0))]
                 + [y_spec(k) for k in range(TOP_K)],
        out_specs=pl.BlockSpec((None, ts, d), lambda i, j: (i, j, 0)),
        out_shape=jax.ShapeDtypeStruct((b, s, d), F32),
        compiler_params=_params("arbitrary", "arbitrary"),
        name="moe_combine",
    )(x, gate.reshape(b, 1, d), weights, *([y] * TOP_K))


def _moe_ffn(x, h, gate, layer, w_router, b_router, w_gate, b_gate, w_up, b_up, w_down, b_down):
    b, s, d = x.shape
    t = b * s
    e = w_router.shape[1]
    tm = 512
    h2 = h.reshape(t * SUBLANES, LANES)
    idx_pad, wgt_pad, rank_pad, counts_pad = _route(h2, w_router, b_router)
    idx = idx_pad[:, :TOP_K]
    counts = counts_pad[0, :e]

    padded = ((counts + tm - 1) // tm) * tm
    ends = jnp.cumsum(padded)
    starts = ends - padded
    pos = starts[idx] + rank_pad[:, :TOP_K]
    n_slots = t * TOP_K + e * tm
    n_tiles = n_slots // tm
    tile_start = jnp.arange(n_tiles, dtype=jnp.int32) * tm
    tile_expert = jnp.sum((tile_start[:, None] >= ends[None, :]).astype(jnp.int32), axis=1)
    tile_expert = jnp.minimum(tile_expert, e - 1)
    n_valid = (ends[-1:] // tm).astype(jnp.int32)
    tile_first = jnp.concatenate([jnp.ones((1,), jnp.int32),
                                  (tile_expert[1:] != tile_expert[:-1]).astype(jnp.int32)])

    tail = jnp.stack([ends[-1], n_tiles - ends[-1] // tm]).astype(jnp.int32)
    xs = _moe_dispatch(h2, pos.reshape(-1) * SUBLANES, starts + counts, padded - counts, tail, n_slots, tm)

    n_pairs = t * TOP_K
    slot_pair = jnp.full((n_slots,), -1, jnp.int32).at[pos.reshape(-1)].set(
        jnp.arange(n_pairs, dtype=jnp.int32), unique_indices=True)
    slot_id = jnp.arange(n_slots, dtype=jnp.int32)
    spare = n_pairs + ((slot_id // tm) % 2) * tm + slot_id % tm
    pair_token, pair_k = slot_pair // TOP_K, slot_pair % TOP_K
    slot_dst = (jnp.where(slot_pair >= 0, pair_k * t + pair_token, spare) * SUBLANES).reshape(n_tiles, 1, tm)

    y = _moe_experts(xs, slot_dst, n_pairs + 2 * tm, tile_expert, tile_first, n_valid, layer,
                     w_gate, b_gate, w_up, b_up, w_down, b_down, tm)
    return _moe_combine(x, gate, y, wgt_pad)


def kernel(x, c, mod_w, mod_b, mix_norm_g, ffn_norm_g, ab_w_in, ab_w_out, moba_q_gain, moba_k_gain,
           fox_w_in, fox_b_f, fox_w_out, fox_q_gain, fox_k_gain, router_w, router_b,
           exp_w_gate, exp_b_gate, exp_w_up, exp_b_up, exp_w_down, exp_b_down):
    b, s, d = x.shape
    depth = mod_w.shape[0]
    n_heads = d // HEAD_DIM
    n_moba = n_heads // 2
    n_sb = n_heads - n_moba
    mod = _adaln_mod(c, mod_w, mod_b)

    for layer in range(depth):
        sh1, sc1, g1, sh2, sc2, g2 = [mod[layer, :, i * d:(i + 1) * d] for i in range(N_MOD)]
        j = layer // 2
        h = _norm_mod(x, mix_norm_g[layer], sc1, sh1)
        if layer % 2 == 0:
            wa = n_moba * HEAD_DIM
            wb = n_sb * HEAD_DIM
            col_scale = jnp.concatenate([
                jnp.tile(moba_q_gain[j] * Q_PRESCALE, n_moba), jnp.tile(moba_k_gain[j], n_moba),
                jnp.ones((wa,), F32), jnp.full((wb,), Q_PRESCALE, F32), jnp.ones((2 * wb,), F32)])
            n_cols = col_scale.shape[0]
            proj = _in_proj(h.reshape(b * s, d), ab_w_in[j].astype(BF16), col_scale.reshape(1, n_cols), 2 * wa)
            proj = proj.reshape(b, s, n_cols)
            o_a = _moba_attention(proj, n_moba, 0)
            o_b = _sb_attention(proj, n_sb, 3 * n_moba)
            w_out = ab_w_out[j].astype(BF16)
            x = _out_proj_residual(x, g1, [(o_a, w_out[:wa]), (o_b, w_out[wa:])])
        else:
            w = n_heads * HEAD_DIM
            col_scale = jnp.concatenate([jnp.tile(fox_q_gain[j] * Q_PRESCALE, n_heads),
                                         jnp.tile(fox_k_gain[j], n_heads), jnp.ones((w,), F32)])
            proj = _in_proj(h.reshape(b * s, d), fox_w_in[j, :, :3 * w].astype(BF16),
                            col_scale.reshape(1, 3 * w), 2 * w)
            cum_f = _forget_cumsum(h, fox_w_in[j, :, 3 * w:], fox_b_f[j])
            o = _fox_attention(proj.reshape(b, s, 3 * w), cum_f, n_heads)
            x = _out_proj_residual(x, g1, [(o, fox_w_out[j].astype(BF16))])
        h = _norm_mod(x, ffn_norm_g[layer], sc2, sh2, packed=True)
        x = _moe_ffn(x, h, g2, layer, router_w[layer], router_b[layer], exp_w_gate, exp_b_gate,
                     exp_w_up, exp_b_up, exp_w_down, exp_b_down)
    return x
```

```python
import functools
import math

import jax
import jax.numpy as jnp
from jax import lax
from jax.experimental import pallas as pl
from jax.experimental.pallas import tpu as pltpu

F32 = jnp.float32
BF16 = jnp.bfloat16

HEAD_DIM = 128
MOBA_BLOCK = 256
MOBA_TOPK = 3
TOP_K = 4
SWIGLU_LIMIT = 7.0
SWIGLU_ALPHA = 1.702
RMS_EPS = 1e-5
N_MOD = 6
LOG2E = math.log2(math.e)
Q_PRESCALE = HEAD_DIM ** -0.5 * LOG2E
LANES = 128
SUBLANES = 8
BF16_ROWS = 16
MASKED = -1e30
SB_EXIT_BITS = 160.0
ROW_DMA_UNROLL = 16
VMEM_LIMIT_BYTES = 56 * 1024 * 1024


def _params(*semantics):
    return pltpu.CompilerParams(dimension_semantics=semantics, vmem_limit_bytes=VMEM_LIMIT_BYTES)


def _dot(a, b):
    return jnp.dot(a, b, preferred_element_type=F32)


def _dot_nt(a, b):
    return lax.dot_general(a, b, (((1,), (1,)), ((), ())), preferred_element_type=F32)


def _split_bf16(x):
    hi = x.astype(BF16)
    lo = (x - hi.astype(F32)).astype(BF16)
    return hi, lo


def _kv_block(ref, j, tk):
    return ref[pl.ds(pl.multiple_of(j * tk, tk), tk), :]


def _pack_bf16_pair(lo, hi):
    lo_bits = lax.bitcast_convert_type(lo.astype(BF16).astype(F32), jnp.uint32)
    hi_bits = lax.bitcast_convert_type(hi.astype(BF16).astype(F32), jnp.uint32)
    return lax.shift_right_logical(lo_bits, jnp.uint32(16)) | (hi_bits & jnp.uint32(0xFFFF0000))


def _unpack_bf16_pair(word):
    lo = lax.bitcast_convert_type(lax.shift_left(word, jnp.uint32(16)), F32)
    hi = lax.bitcast_convert_type(word & jnp.uint32(0xFFFF0000), F32)
    return lo, hi


TOKEN_DIM = 2 * SUBLANES * LANES


def _store_token_tiles(ref, y, lead=()):
    n = y.shape[0]
    half = TOKEN_DIM // 2
    for s in range(SUBLANES):
        lo = y[:, s * LANES:(s + 1) * LANES]
        hi = y[:, half + s * LANES:half + (s + 1) * LANES]
        ref[lead + (pl.ds(s, n, stride=SUBLANES), slice(None))] = _pack_bf16_pair(lo, hi)


def _load_token_tiles(ref, n, lead=()):
    los, his = [], []
    for s in range(SUBLANES):
        lo, hi = _unpack_bf16_pair(ref[lead + (pl.ds(s, n, stride=SUBLANES), slice(None))])
        los.append(lo)
        his.append(hi)
    return los + his


def _load_token_tiles_bf16(ref, n, lead=()):
    return jnp.concatenate([c.astype(BF16) for c in _load_token_tiles(ref, n, lead)], axis=1)


def _mod_kernel(c_ref, w_ref, b_ref, o_ref):
    c = c_ref[...]
    c_act = c / (1.0 + jnp.exp(-c))
    o_ref[...] = _dot(c_act, w_ref[...]) + b_ref[...]


def _adaln_mod(c, mod_w, mod_b):
    depth, d, n = mod_w.shape
    b = c.shape[0]
    bp = -(-b // SUBLANES) * SUBLANES
    c_pad = jnp.zeros((bp, d), F32).at[:b].set(c)
    tn = min(n, 1024)
    out = pl.pallas_call(
        _mod_kernel,
        grid=(depth, n // tn),
        in_specs=[pl.BlockSpec((bp, d), lambda l, j: (0, 0)),
                  pl.BlockSpec((None, d, tn), lambda l, j: (l, 0, j)),
                  pl.BlockSpec((None, 1, tn), lambda l, j: (l, 0, j))],
        out_specs=pl.BlockSpec((None, bp, tn), lambda l, j: (l, 0, j)),
        out_shape=jax.ShapeDtypeStruct((depth, bp, n), F32),
        compiler_params=_params("arbitrary", "arbitrary"),
        name="adaln_mod",
    )(c_pad, mod_w, mod_b.reshape(depth, 1, n))
    return out[:, :b]


def _norm_kernel(x_ref, g_ref, sc_ref, sh_ref, o_ref, *, packed):
    x = x_ref[...]
    ms = jnp.mean(x * x, axis=-1, keepdims=True)
    y = x * lax.rsqrt(ms + RMS_EPS)
    y = (y * g_ref[...]) * (1.0 + sc_ref[...]) + sh_ref[...]
    if packed:
        _store_token_tiles(o_ref, y)
    else:
        o_ref[...] = y.astype(o_ref.dtype)


def _norm_mod(x, gain, scale, shift, packed=False):
    b, s, d = x.shape
    ts = min(s, 512)
    if packed:
        assert d == TOKEN_DIM
        out_block, out_shape = (None, ts * SUBLANES, LANES), (b, s * SUBLANES, LANES)
    else:
        out_block, out_shape = (None, ts, d), (b, s, d)
    return pl.pallas_call(
        functools.partial(_norm_kernel, packed=packed),
        grid=(b, s // ts),
        in_specs=[pl.BlockSpec((None, ts, d), lambda i, j: (i, j, 0)),
                  pl.BlockSpec((1, d), lambda i, j: (0, 0)),
                  pl.BlockSpec((None, 1, d), lambda i, j: (i, 0, 0)),
                  pl.BlockSpec((None, 1, d), lambda i, j: (i, 0, 0))],
        out_specs=pl.BlockSpec(out_block, lambda i, j: (i, j, 0)),
        out_shape=jax.ShapeDtypeStruct(out_shape, jnp.uint32 if packed else BF16),
        compiler_params=_params("arbitrary", "arbitrary"),
        name="norm_mod",
    )(x, gain.reshape(1, d), scale.reshape(b, 1, d), shift.reshape(b, 1, d))


def _inproj_kernel(h_ref, w_ref, gain_ref, o_ref, *, n_norm_tiles):
    acc = _dot(h_ref[...], w_ref[...])
    j = pl.program_id(0)

    @pl.when(j < n_norm_tiles)
    def _():
        for g in range(acc.shape[1] // HEAD_DIM):
            sl = slice(g * HEAD_DIM, (g + 1) * HEAD_DIM)
            blk = acc[:, sl]
            ms = jnp.mean(blk * blk, axis=-1, keepdims=True)
            o_ref[:, sl] = (blk * lax.rsqrt(ms + RMS_EPS) * gain_ref[:, sl]).astype(o_ref.dtype)

    @pl.when(j >= n_norm_tiles)
    def _():
        o_ref[...] = (acc * gain_ref[...]).astype(o_ref.dtype)


def _in_proj(h, w, col_scale, n_norm_cols):
    t, d = h.shape
    n = w.shape[1]
    tm = min(t, 1024)
    tn = min(n, 2048)
    assert n_norm_cols % tn == 0 and n % tn == 0 and t % tm == 0
    return pl.pallas_call(
        functools.partial(_inproj_kernel, n_norm_tiles=n_norm_cols // tn),
        grid=(n // tn, t // tm),
        in_specs=[pl.BlockSpec((tm, d), lambda j, i: (i, 0)),
                  pl.BlockSpec((d, tn), lambda j, i: (0, j)),
                  pl.BlockSpec((1, tn), lambda j, i: (0, j))],
        out_specs=pl.BlockSpec((tm, tn), lambda j, i: (i, j)),
        out_shape=jax.ShapeDtypeStruct((t, n), BF16),
        compiler_params=_params("arbitrary", "arbitrary"),
        name="in_proj",
    )(h, w, col_scale)


def _fgate_kernel(h_ref, wf_ref, bf_ref, o_ref, carry_ref):
    @pl.when(pl.program_id(1) == 0)
    def _():
        carry_ref[...] = jnp.zeros_like(carry_ref)

    ts = h_ref.shape[0]
    logit = _dot(h_ref[...], wf_ref[...]) + bf_ref[...]
    log_f = jnp.minimum(logit, 0.0) - jnp.log(1.0 + jnp.exp(-jnp.abs(logit)))
    row = lax.broadcasted_iota(jnp.int32, (ts, ts), 0)
    col = lax.broadcasted_iota(jnp.int32, (ts, ts), 1)
    lower = jnp.where(col <= row, 1.0, 0.0).astype(BF16)
    hi, lo = _split_bf16(log_f)
    cum = _dot(lower, hi) + _dot(lower, lo) + carry_ref[0:1, :]
    o_ref[...] = cum
    carry_ref[...] = jnp.broadcast_to(cum[ts - 1:ts, :], carry_ref.shape)


def _forget_cumsum(h, w_f, b_f):
    b, s, d = h.shape
    nh = w_f.shape[1]
    assert nh <= LANES
    ts = min(s, 512)
    w_pad = jnp.zeros((d, LANES), BF16).at[:, :nh].set(w_f.astype(BF16))
    b_pad = jnp.zeros((1, LANES), F32).at[0, :nh].set(b_f)
    return pl.pallas_call(
        _fgate_kernel,
        grid=(b, s // ts),
        in_specs=[pl.BlockSpec((None, ts, d), lambda i, j: (i, j, 0)),
                  pl.BlockSpec((d, LANES), lambda i, j: (0, 0)),
                  pl.BlockSpec((1, LANES), lambda i, j: (0, 0))],
        out_specs=pl.BlockSpec((None, ts, LANES), lambda i, j: (i, j, 0)),
        out_shape=jax.ShapeDtypeStruct((b, s, LANES), F32),
        scratch_shapes=[pltpu.VMEM((SUBLANES, LANES), F32)],
        compiler_params=_params("arbitrary", "arbitrary"),
        name="forget_cumsum",
    )(h, w_pad, b_pad)


VT_ROWS = HEAD_DIM + BF16_ROWS


def _store_v_transposed(v_ref, vt_ref, c, tile):
    vb = _kv_block(v_ref, c, tile).astype(F32)
    vt_ref[c, 0:HEAD_DIM, :] = vb.T.astype(BF16)
    vt_ref[c, HEAD_DIM:VT_ROWS, :] = jnp.ones((BF16_ROWS, tile), BF16)


def _softmax_accumulate(s, vt_blk, m_ref, acc_ref):
    m_old = m_ref[...]
    m_new = jnp.maximum(m_old, jnp.max(s, axis=0, keepdims=True))
    alpha = jnp.exp2(m_old - m_new)
    p = jnp.exp2((s - m_new).astype(BF16))
    m_ref[...] = m_new
    acc_ref[...] = alpha * acc_ref[...] + _dot(vt_blk, p)


def _softmax_finish(acc_ref, o_ref, qi, tile):
    acc = acc_ref[...]
    out_t = acc[0:HEAD_DIM, :] / acc[HEAD_DIM:HEAD_DIM + 1, :]
    o_ref[pl.ds(pl.multiple_of(qi * tile, tile), tile), :] = out_t.T.astype(o_ref.dtype)


def _flash_sweep(qi, n_past, produce, s0_ref, s1_ref, vt_ref, m_ref, acc_ref, tile):
    m_ref[...] = jnp.full(m_ref.shape, MASKED, F32)
    acc_ref[...] = jnp.zeros_like(acc_ref)
    key = lax.broadcasted_iota(jnp.int32, (tile, tile), 0)
    qry = lax.broadcasted_iota(jnp.int32, (tile, tile), 1)
    s0_ref[...] = jnp.where(key <= qry, produce(qi), MASKED)
    n_pos = n_past + 1
    farthest = qi - n_past

    def pair(i):
        blk = qi - 2 * i
        s1_ref[...] = produce(blk - 1)
        _softmax_accumulate(s0_ref[...], vt_ref[blk], m_ref, acc_ref)
        s0_ref[...] = produce(jnp.maximum(blk - 2, farthest))
        _softmax_accumulate(s1_ref[...], vt_ref[blk - 1], m_ref, acc_ref)

    def two_pairs(i, carry):
        pair(2 * i)
        pair(2 * i + 1)
        return carry

    n_pairs = n_pos // 2
    lax.fori_loop(0, n_pairs // 2, two_pairs, 0)

    @pl.when(lax.rem(n_pairs, 2) == 1)
    def _():
        pair(n_pairs - 1)

    @pl.when(lax.rem(n_pos, 2) == 1)
    def _():
        _softmax_accumulate(s0_ref[...], vt_ref[farthest], m_ref, acc_ref)


SKIP_BITS = 162.0


def _logit_spread_bound(q, kmax_sq_ref):
    q32 = q.astype(F32)
    qmax_sq = jnp.max(jnp.sum(q32 * q32, axis=-1, keepdims=True), axis=0, keepdims=True)
    return 2.0 * jnp.sqrt(qmax_sq * kmax_sq_ref[0:1, 0:1])


def _update_kmax_sq(kmax_sq_ref, kb):
    blk_max = jnp.max(jnp.sum(kb * kb, axis=-1, keepdims=True), axis=0, keepdims=True)
    kmax_sq_ref[...] = jnp.maximum(kmax_sq_ref[...], jnp.broadcast_to(blk_max, kmax_sq_ref.shape))


def _fox_kernel(q_ref, k_ref, v_ref, f_ref, o_ref, vt_ref, frep_ref, fend_ref, kmax_sq_ref, s0_ref, s1_ref, m_ref,
                acc_ref, *, tile, n_kv):
    head = pl.program_id(1)
    lane = lax.broadcasted_iota(jnp.int32, (tile, LANES), 1)
    fend_ref[...] = jnp.zeros_like(fend_ref)
    kmax_sq_ref[...] = jnp.zeros_like(kmax_sq_ref)

    def prep(c, carry):
        _store_v_transposed(v_ref, vt_ref, c, tile)
        _update_kmax_sq(kmax_sq_ref, _kv_block(k_ref, c, tile).astype(F32))
        f_blk = _kv_block(f_ref, c, tile)
        f_col = jnp.sum(jnp.where(lane == head, f_blk, 0.0), axis=-1, keepdims=True)
        f_rep = jnp.broadcast_to(f_col * LOG2E, (tile, LANES))
        frep_ref[pl.ds(pl.multiple_of(c * tile, tile), tile), :] = f_rep
        fend_ref[pl.ds(c, 1), :] = f_rep[tile - 1:tile, :]
        return carry

    lax.fori_loop(0, n_kv, prep, 0)

    def query_tile(qi, carry):
        qi = jnp.asarray(qi, jnp.int32)
        q = _kv_block(q_ref, qi, tile)
        f_first = frep_ref[pl.ds(pl.multiple_of(qi * tile, tile), 1), :]
        gap = fend_ref[...] - f_first
        block_id = lax.broadcasted_iota(jnp.int32, fend_ref.shape, 0)
        needed = jnp.logical_and(block_id < qi, gap <= _logit_spread_bound(q, kmax_sq_ref) + SKIP_BITS)
        n_past = jnp.sum(jnp.where(needed, 1.0, 0.0)[:, 0:1]).astype(jnp.int32)

        def produce(j):
            f_rep = _kv_block(frep_ref, j, tile)
            return _dot_nt(_kv_block(k_ref, j, tile), q) - jnp.concatenate([f_rep] * (tile // LANES), axis=1)

        _flash_sweep(qi, n_past, produce, s0_ref, s1_ref, vt_ref, m_ref, acc_ref, tile)
        _softmax_finish(acc_ref, o_ref, qi, tile)
        return carry

    lax.fori_loop(0, n_kv, query_tile, 0)


def _fox_attention(proj, cum_f, n_heads):
    b, s, _ = proj.shape
    tile = min(s, 512)
    n_kv = s // tile
    return pl.pallas_call(
        functools.partial(_fox_kernel, tile=tile, n_kv=n_kv),
        grid=(b, n_heads),
        in_specs=[pl.BlockSpec((None, s, HEAD_DIM), lambda i, h: (i, 0, h)),
                  pl.BlockSpec((None, s, HEAD_DIM), lambda i, h: (i, 0, n_heads + h)),
                  pl.BlockSpec((None, s, HEAD_DIM), lambda i, h: (i, 0, 2 * n_heads + h)),
                  pl.BlockSpec((None, s, LANES), lambda i, h: (i, 0, 0))],
        out_specs=pl.BlockSpec((None, s, HEAD_DIM), lambda i, h: (i, 0, h)),
        out_shape=jax.ShapeDtypeStruct((b, s, n_heads * HEAD_DIM), BF16),
        scratch_shapes=[pltpu.VMEM((n_kv, VT_ROWS, tile), BF16), pltpu.VMEM((s, LANES), F32),
                        pltpu.VMEM((-(-n_kv // SUBLANES) * SUBLANES, LANES), F32),
                        pltpu.VMEM((SUBLANES, LANES), F32),
                        pltpu.VMEM((tile, tile), F32), pltpu.VMEM((tile, tile), F32),
                        pltpu.VMEM((1, tile), F32),
                        pltpu.VMEM((VT_ROWS, tile), F32)],
        compiler_params=_params("arbitrary", "arbitrary"),
        name="fox_attention",
    )(proj, proj, proj, cum_f)


def _moba_kernel(q_ref, k_ref, v_ref, slope_ref, o_ref, kmean_ref, kmax_sq_ref, vt_ref, sel_ref, alibi_ref,
                 s0_ref, s1_ref, m_ref, acc_ref, *, tile, n_blocks):
    blk = MOBA_BLOCK
    per_tile = tile // blk
    kmean_ref[...] = jnp.zeros_like(kmean_ref)
    kmax_sq_ref[...] = jnp.zeros_like(kmax_sq_ref)

    def block_mean(n, carry):
        kb = _kv_block(k_ref, n, blk).astype(F32)
        kmean_ref[pl.ds(n, 1), :] = jnp.mean(kb, axis=0, keepdims=True)
        _update_kmax_sq(kmax_sq_ref, kb)
        return carry

    def transpose_v(c, carry):
        _store_v_transposed(v_ref, vt_ref, c, tile)
        return carry

    lax.fori_loop(0, n_blocks, block_mean, 0)
    lax.fori_loop(0, n_blocks // per_tile, transpose_v, 0)

    slope = slope_ref[...] * LOG2E
    alibi_ref[...] = slope * lax.broadcasted_iota(jnp.int32, (tile, tile), 0).astype(F32)
    block_id = lax.broadcasted_iota(jnp.int32, (LANES, tile), 0).astype(F32)
    qry_block = jnp.floor(lax.broadcasted_iota(jnp.int32, (LANES, tile), 1).astype(F32) * (1.0 / blk))

    def query_tile(qi, carry):
        qi = jnp.asarray(qi, jnp.int32)
        q = _kv_block(q_ref, qi, tile)
        own = jnp.asarray(qi * per_tile, F32) + qry_block
        gate = _dot_nt(kmean_ref[...].astype(BF16), q)
        gate = jnp.where(block_id < own, gate, -jnp.inf)
        sel = jnp.where(block_id == own, 1.0, 0.0)
        for slot in range(MOBA_TOPK):
            best = jnp.max(gate, axis=0, keepdims=True)
            idx = jnp.min(jnp.where(gate == best, block_id, float(LANES)), axis=0, keepdims=True)
            hit = block_id == idx
            slot_ok = jnp.where(float(slot) < own, 1.0, 0.0)
            sel = jnp.maximum(sel, jnp.where(hit, slot_ok, 0.0))
            gate = jnp.where(hit, -jnp.inf, gate)
        sel_ref[...] = jnp.where(sel > 0.5, 0.0, MASKED)

        def produce(j):
            shift = slope * jnp.asarray((j - qi) * tile, F32)
            per_query = jnp.concatenate(
                [jnp.broadcast_to(sel_ref[pl.ds(j * per_tile + r, 1), :] + shift, (blk, tile))
                 for r in range(per_tile)], axis=0)
            return (_dot_nt(_kv_block(k_ref, j, tile), q) + alibi_ref[...]) + per_query

        reach = (_logit_spread_bound(q, kmax_sq_ref) + SKIP_BITS) / slope
        tiles_back = jnp.floor((reach - 1.0) * (1.0 / tile)) + 2.0
        n_past = jnp.minimum(jnp.minimum(jnp.max(tiles_back), float(n_blocks)).astype(jnp.int32), qi)

        _flash_sweep(qi, n_past, produce, s0_ref, s1_ref, vt_ref, m_ref, acc_ref, tile)
        _softmax_finish(acc_ref, o_ref, qi, tile)
        return carry

    lax.fori_loop(0, n_blocks // per_tile, query_tile, 0)


def _moba_attention(proj, n_heads, col_block):
    b, s, _ = proj.shape
    blk = MOBA_BLOCK
    n_blocks = s // blk
    tile = min(s, 2 * blk)
    assert s % tile == 0 and tile % blk == 0 and n_blocks <= LANES
    slopes = jnp.asarray([2.0 ** (-8.0 * (i + 1) / n_heads) for i in range(n_heads)], F32)
    slope_rows = jnp.broadcast_to(slopes[:, None, None], (n_heads, 1, tile))
    return pl.pallas_call(
        functools.partial(_moba_kernel, tile=tile, n_blocks=n_blocks),
        grid=(b, n_heads),
        in_specs=[pl.BlockSpec((None, s, HEAD_DIM), lambda i, h: (i, 0, col_block + h)),
                  pl.BlockSpec((None, s, HEAD_DIM), lambda i, h: (i, 0, col_block + n_heads + h)),
                  pl.BlockSpec((None, s, HEAD_DIM), lambda i, h: (i, 0, col_block + 2 * n_heads + h)),
                  pl.BlockSpec((None, 1, tile), lambda i, h: (h, 0, 0))],
        out_specs=pl.BlockSpec((None, s, HEAD_DIM), lambda i, h: (i, 0, h)),
        out_shape=jax.ShapeDtypeStruct((b, s, n_heads * HEAD_DIM), BF16),
        scratch_shapes=[pltpu.VMEM((LANES, HEAD_DIM), F32), pltpu.VMEM((SUBLANES, LANES), F32),
                        pltpu.VMEM((s // tile, VT_ROWS, tile), BF16),
                        pltpu.VMEM((LANES, tile), F32), pltpu.VMEM((tile, tile), F32),
                        pltpu.VMEM((tile, tile), F32), pltpu.VMEM((tile, tile), F32),
                        pltpu.VMEM((1, tile), F32), pltpu.VMEM((VT_ROWS, tile), F32)],
        compiler_params=_params("arbitrary", "arbitrary"),
        name="moba_attention",
    )(proj, proj, proj, slope_rows)


def _sb_kernel(q_ref, k_ref, v_ref, o_ref, after_ref, decay_ref, acc_ref, *, tile, n_q):
    row = lax.broadcasted_iota(jnp.int32, (tile, tile), 0)
    col = lax.broadcasted_iota(jnp.int32, (tile, tile), 1)
    after_ref[...] = jnp.where(row > col, 1.0, 0.0).astype(BF16)

    def query_tile(qi, carry):
        qi = jnp.asarray(qi, jnp.int32)
        q = _kv_block(q_ref, qi, tile)
        causal = col < row

        def logits_and_drop(j, masked):
            z = _dot_nt(q, _kv_block(k_ref, j, tile))
            drop = jnp.maximum(z, 0.0) + jnp.log2(1.0 + jnp.exp2(-jnp.abs(z)))
            if masked:
                drop = jnp.where(causal, drop, 0.0)
            return z, drop

        def weighted_values(j, z, drop, decay_right, masked):
            hi, lo = _split_bf16(drop)
            after = after_ref[...]
            decay = _dot(hi, after) + _dot(lo, after) + decay_right
            a = jnp.exp2(z - drop - decay)
            if masked:
                a = jnp.where(causal, a, 0.0)
            return _dot(a.astype(BF16), _kv_block(v_ref, j, tile))

        @pl.when(qi == 0)
        def _():
            z, drop = logits_and_drop(qi, True)
            acc_ref[...] = weighted_values(qi, z, drop, 0.0, True)
            decay_ref[...] = jnp.sum(drop, axis=-1, keepdims=True)

        @pl.when(qi > 0)
        def _():
            z_d, drop_d = logits_and_drop(qi, True)
            z_p, drop_p = logits_and_drop(qi - 1, False)
            decay_d = jnp.sum(drop_d, axis=-1, keepdims=True)
            acc_ref[...] = (weighted_values(qi, z_d, drop_d, 0.0, True)
                            + weighted_values(qi - 1, z_p, drop_p, decay_d, False))
            decay_ref[...] = decay_d + jnp.sum(drop_p, axis=-1, keepdims=True)

        def cond(state):
            jj, least = state
            return jnp.logical_and(jj < qi, least < SB_EXIT_BITS)

        def body(state):
            jj, _ = state
            j = qi - 1 - jj
            z, drop = logits_and_drop(j, False)
            acc_ref[...] += weighted_values(j, z, drop, decay_ref[...], False)
            decay_ref[...] += jnp.sum(drop, axis=-1, keepdims=True)
            return jj + 1, jnp.min(decay_ref[...])

        lax.while_loop(cond, body, (jnp.int32(1), jnp.min(decay_ref[...])))
        o_ref[pl.ds(pl.multiple_of(qi * tile, tile), tile), :] = acc_ref[...].astype(o_ref.dtype)
        return carry

    lax.fori_loop(0, n_q, query_tile, 0)


def _sb_attention(proj, n_heads, col_block):
    b, s, _ = proj.shape
    tile = min(s, 256)
    return pl.pallas_call(
        functools.partial(_sb_kernel, tile=tile, n_q=s // tile),
        grid=(b, n_heads),
        in_specs=[pl.BlockSpec((None, s, HEAD_DIM), lambda i, h: (i, 0, col_block + h)),
                  pl.BlockSpec((None, s, HEAD_DIM), lambda i, h: (i, 0, col_block + n_heads + h)),
                  pl.BlockSpec((None, s, HEAD_DIM), lambda i, h: (i, 0, col_block + 2 * n_heads + h))],
        out_specs=pl.BlockSpec((None, s, HEAD_DIM), lambda i, h: (i, 0, h)),
        out_shape=jax.ShapeDtypeStruct((b, s, n_heads * HEAD_DIM), BF16),
        scratch_shapes=[pltpu.VMEM((tile, tile), BF16), pltpu.VMEM((tile, 1), F32),
                        pltpu.VMEM((tile, HEAD_DIM), F32)],
        compiler_params=_params("arbitrary", "arbitrary"),
        name="sb_attention",
    )(proj, proj, proj)


def _outproj_kernel(*refs, n_parts):
    x_ref, gate_ref = refs[0], refs[1]
    parts = refs[2:2 + 2 * n_parts]
    o_ref = refs[2 + 2 * n_parts]
    y = _dot(parts[0][...], parts[1][...])
    for p in range(1, n_parts):
        y += _dot(parts[2 * p][...], parts[2 * p + 1][...])
    o_ref[...] = x_ref[...] + gate_ref[...] * y


def _out_proj_residual(x, gate, parts):
    b, s, d = x.shape
    ts = min(s, 512)
    in_specs = [pl.BlockSpec((None, ts, d), lambda i, j: (i, j, 0)),
                pl.BlockSpec((None, 1, d), lambda i, j: (i, 0, 0))]
    args = [x, gate.reshape(b, 1, d)]
    for a, w in parts:
        kp = a.shape[-1]
        in_specs += [pl.BlockSpec((None, ts, kp), lambda i, j: (i, j, 0)),
                     pl.BlockSpec((kp, d), lambda i, j: (0, 0))]
        args += [a, w]
    return pl.pallas_call(
        functools.partial(_outproj_kernel, n_parts=len(parts)),
        grid=(b, s // ts),
        in_specs=in_specs,
        out_specs=pl.BlockSpec((None, ts, d), lambda i, j: (i, j, 0)),
        out_shape=jax.ShapeDtypeStruct((b, s, d), F32),
        compiler_params=_params("arbitrary", "arbitrary"),
        name="out_proj",
    )(*args)


def _router_kernel(h_ref, w_ref, b_ref, idx_ref, wgt_ref, rank_ref, cnt_ref, carry_ref):
    @pl.when(pl.program_id(0) == 0)
    def _():
        carry_ref[...] = jnp.zeros_like(carry_ref)

    tm = h_ref.shape[0] // SUBLANES
    logits = _dot(_load_token_tiles_bf16(h_ref, tm), w_ref[...]) + b_ref[...]
    lane = lax.broadcasted_iota(jnp.int32, (tm, LANES), 1).astype(F32)
    vals, idxs = [], []
    onehot = jnp.zeros((tm, LANES), F32)
    for _ in range(TOP_K):
        best = jnp.max(logits, axis=-1, keepdims=True)
        idx = jnp.min(jnp.where(logits == best, lane, float(LANES)), axis=-1, keepdims=True)
        hit = lane == idx
        onehot = jnp.where(hit, 1.0, onehot)
        logits = jnp.where(hit, -jnp.inf, logits)
        vals.append(best)
        idxs.append(idx)
    exps = [jnp.exp(v - vals[0]) for v in vals]
    denom = exps[0]
    for e in exps[1:]:
        denom = denom + e

    row = lax.broadcasted_iota(jnp.int32, (tm, tm), 0)
    col = lax.broadcasted_iota(jnp.int32, (tm, tm), 1)
    before = jnp.where(col < row, 1.0, 0.0).astype(BF16)
    prior = _dot(before, onehot.astype(BF16)) + carry_ref[0:1, :]

    idx_out = jnp.zeros((tm, LANES), F32)
    wgt_out = jnp.zeros((tm, LANES), F32)
    rank_out = jnp.zeros((tm, LANES), F32)
    for k in range(TOP_K):
        rank_k = jnp.sum(jnp.where(lane == idxs[k], prior, 0.0), axis=-1, keepdims=True)
        slot = lane == float(k)
        idx_out = jnp.where(slot, idxs[k], idx_out)
        wgt_out = jnp.where(slot, exps[k] / denom, wgt_out)
        rank_out = jnp.where(slot, rank_k, rank_out)
    idx_ref[...] = idx_out.astype(jnp.int32)
    wgt_ref[...] = wgt_out
    rank_ref[...] = rank_out.astype(jnp.int32)
    counts = carry_ref[...] + jnp.sum(onehot, axis=0, keepdims=True)
    carry_ref[...] = counts
    cnt_ref[...] = counts.astype(jnp.int32)


def _route(h, w_router, b_router):
    t = h.shape[0] // SUBLANES
    d = TOKEN_DIM
    e = w_router.shape[1]
    assert e <= LANES
    tm = min(t, 512)
    w_pad = jnp.zeros((d, LANES), BF16).at[:, :e].set(w_router.astype(BF16))
    b_pad = jnp.full((1, LANES), MASKED, F32).at[0, :e].set(b_router)
    tok_spec = pl.BlockSpec((tm, LANES), lambda i: (i, 0))
    return pl.pallas_call(
        _router_kernel,
        grid=(t // tm,),
        in_specs=[pl.BlockSpec((tm * SUBLANES, LANES), lambda i: (i, 0)),
                  pl.BlockSpec((d, LANES), lambda i: (0, 0)),
                  pl.BlockSpec((1, LANES), lambda i: (0, 0))],
        out_specs=[tok_spec, tok_spec, tok_spec, pl.BlockSpec((SUBLANES, LANES), lambda i: (0, 0))],
        out_shape=[jax.ShapeDtypeStruct((t, LANES), jnp.int32), jax.ShapeDtypeStruct((t, LANES), F32),
                   jax.ShapeDtypeStruct((t, LANES), jnp.int32),
                   jax.ShapeDtypeStruct((SUBLANES, LANES), jnp.int32)],
        scratch_shapes=[pltpu.VMEM((SUBLANES, LANES), F32)],
        compiler_params=_params("arbitrary"),
        name="moe_router",
    )(h, w_pad, b_pad)


def _start_rows(n_rows, copy_of_row):
    def body(p, carry):
        copy_of_row(2 * p).start(priority=0)
        copy_of_row(2 * p + 1).start(priority=1)
        return carry

    lax.fori_loop(0, n_rows // 2, body, 0, unroll=ROW_DMA_UNROLL // 2)


def _wait_rows(n_rows, copy_of_row):
    def body(r, carry):
        copy_of_row(r).wait()
        return carry

    lax.fori_loop(0, n_rows, body, 0, unroll=ROW_DMA_UNROLL)


def _moe_dispatch_kernel(padstart_ref, padlen_ref, tail_ref, pos_ref, h_ref, xs_hbm, zeros, sem, *, n_experts, tm):
    i = pl.program_id(0)
    n_tokens = h_ref.shape[0] // SUBLANES

    def zero_block(first_slot, n_slots):
        return pltpu.make_async_copy(
            zeros.at[pl.ds(0, n_slots * SUBLANES)],
            xs_hbm.at[pl.ds(pl.multiple_of(first_slot * SUBLANES, SUBLANES), n_slots * SUBLANES)], sem.at[0])

    def pad_blocks(e, fn):
        first = padstart_ref[e]
        size = tm // 2
        while size >= 1:
            used = (padlen_ref[e] & size) != 0

            @pl.when(used)
            def _(first=first, size=size):
                fn(zero_block(first, size))

            first = first + jnp.where(used, size, 0)
            size //= 2

    def tail_blocks(fn):
        def body(j, carry):
            fn(zero_block(tail_ref[0] + j * tm, tm))
            return carry

        lax.fori_loop(0, tail_ref[1], body, 0)

    @pl.when(i == 0)
    def _():
        zeros[...] = jnp.zeros_like(zeros)

        def start_expert(e, carry):
            pad_blocks(e, lambda copy: copy.start())
            return carry

        def wait_expert(e, carry):
            pad_blocks(e, lambda copy: copy.wait())
            return carry

        lax.fori_loop(0, n_experts, start_expert, 0)
        tail_blocks(lambda copy: copy.start())
        lax.fori_loop(0, n_experts, wait_expert, 0)
        tail_blocks(lambda copy: copy.wait())

    def pair_copy(token, k):
        src = h_ref.at[pl.ds(pl.multiple_of(token * SUBLANES, SUBLANES), SUBLANES)]
        dst = xs_hbm.at[pl.ds(pl.multiple_of(pos_ref[0, token * TOP_K + k], SUBLANES), SUBLANES)]
        return pltpu.make_async_copy(src, dst, sem.at[1])

    def start_token(token, carry):
        for k in range(TOP_K):
            pair_copy(token, k).start(priority=k % 2)
        return carry

    def wait_token(token, carry):
        for k in range(TOP_K):
            pair_copy(token, k).wait()
        return carry

    lax.fori_loop(0, n_tokens, start_token, 0, unroll=ROW_DMA_UNROLL // TOP_K)
    lax.fori_loop(0, n_tokens, wait_token, 0, unroll=ROW_DMA_UNROLL // TOP_K)


def _moe_dispatch(h_packed, pair_row, pad_start, pad_len, tail, n_slots, tm):
    t = h_packed.shape[0] // SUBLANES
    tt = min(t, 512)
    return pl.pallas_call(
        functools.partial(_moe_dispatch_kernel, n_experts=pad_start.shape[0], tm=tm),
        grid_spec=pltpu.PrefetchScalarGridSpec(
            num_scalar_prefetch=3, grid=(t // tt,),
            in_specs=[pl.BlockSpec((None, 1, tt * TOP_K), lambda i, *_: (i, 0, 0), memory_space=pltpu.SMEM),
                      pl.BlockSpec((tt * SUBLANES, LANES), lambda i, *_: (i, 0))],
            out_specs=pl.BlockSpec(memory_space=pl.ANY),
            scratch_shapes=[pltpu.VMEM((tm * SUBLANES, LANES), jnp.uint32), pltpu.SemaphoreType.DMA((2,))]),
        out_shape=jax.ShapeDtypeStruct((n_slots * SUBLANES, LANES), jnp.uint32),
        compiler_params=_params("arbitrary"),
        name="moe_dispatch",
    )(pad_start, pad_len, tail, pair_row.reshape(t // tt, 1, tt * TOP_K), h_packed)


def _moe_up_kernel(te_ref, first_ref, nvalid_ref, x_ref, wg_ref, bg_ref, wu_ref, bu_ref, o_ref, wg_bf, wu_bf):
    i = pl.program_id(0)
    n_valid = nvalid_ref[0]
    tm = x_ref.shape[0] // SUBLANES

    @pl.when(first_ref[i] == 1)
    def _():
        wg_bf[...] = wg_ref[...].astype(BF16)
        wu_bf[...] = wu_ref[...].astype(BF16)

    @pl.when(i < n_valid)
    def _():
        x = _load_token_tiles_bf16(x_ref, tm)
        g = jnp.minimum(_dot(x, wg_bf[...]) + bg_ref[...], SWIGLU_LIMIT)
        u = jnp.clip(_dot(x, wu_bf[...]) + bu_ref[...], -SWIGLU_LIMIT, SWIGLU_LIMIT)
        act = (u + 1.0) * (g / (1.0 + jnp.exp(-SWIGLU_ALPHA * g)))
        o_ref[...] = act.astype(o_ref.dtype)

    @pl.when(i >= n_valid)
    def _():
        o_ref[...] = jnp.zeros_like(o_ref)


def _moe_down_kernel(te_ref, first_ref, nvalid_ref, dst_ref, dst_prev_ref, a_ref, wd_ref, bd_ref, y_hbm,
                     obuf, sem, wd_bf, *, n_tiles, spare_row0):
    i = pl.program_id(0)
    slot = lax.rem(i, 2)
    n_valid = nvalid_ref[0]
    tm = obuf.shape[1] // SUBLANES

    def row_copy(dst_smem, buf, r):
        src = obuf.at[buf, pl.ds(pl.multiple_of(r * SUBLANES, SUBLANES), SUBLANES)]
        dst = y_hbm.at[pl.ds(pl.multiple_of(dst_smem[0, r], SUBLANES), SUBLANES)]
        return pltpu.make_async_copy(src, dst, sem.at[buf])

    @pl.when(i == 0)
    def _():
        obuf[...] = jnp.zeros_like(obuf)
        rows = tm * SUBLANES
        fills = [pltpu.make_async_copy(obuf.at[half], y_hbm.at[pl.ds(spare_row0 + half * rows, rows)], sem.at[half])
                 for half in range(2)]
        for fill in fills:
            fill.start()
        for fill in fills:
            fill.wait()

    @pl.when(first_ref[i] == 1)
    def _():
        wd_bf[...] = wd_ref[...].astype(BF16)

    @pl.when(i < n_valid)
    def _():
        _store_token_tiles(obuf, _dot(a_ref[...], wd_bf[...]) + bd_ref[...], lead=(slot,))
        _start_rows(tm, lambda r: row_copy(dst_ref, slot, r))

    @pl.when(jnp.logical_and(i >= 1, i - 1 < n_valid))
    def _():
        _wait_rows(tm, lambda r: row_copy(dst_prev_ref, 1 - slot, r))

    @pl.when(jnp.logical_and(i == n_tiles - 1, i < n_valid))
    def _():
        _wait_rows(tm, lambda r: row_copy(dst_ref, slot, r))


def _expert_spec(layer, shape):
    return pl.BlockSpec((None, None) + shape, lambda i, te, first, nvalid: (layer, te[i], 0, 0))


def _moe_experts(xs, slot_dst, n_out_rows, tile_expert, tile_first, n_valid, layer,
                 w_gate, b_gate, w_up, b_up, w_down, b_down, tm):
    d = TOKEN_DIM
    buf_shape = (2, tm * SUBLANES, LANES)
    depth, e, _, f = w_gate.shape
    n_tiles = slot_dst.shape[0]
    smem_rows = lambda index: pl.BlockSpec((None, 1, tm), index, memory_space=pltpu.SMEM)
    row_spec = lambda width: pl.BlockSpec((tm, width), lambda i, te, first, nvalid: (i, 0))
    act = pl.pallas_call(
        _moe_up_kernel,
        grid_spec=pltpu.PrefetchScalarGridSpec(
            num_scalar_prefetch=3, grid=(n_tiles,),
            in_specs=[pl.BlockSpec((tm * SUBLANES, LANES), lambda i, te, first, nvalid: (i, 0)),
                      _expert_spec(layer, (d, f)), _expert_spec(layer, (1, f)),
                      _expert_spec(layer, (d, f)), _expert_spec(layer, (1, f))],
            out_specs=row_spec(f),
            scratch_shapes=[pltpu.VMEM((d, f), BF16), pltpu.VMEM((d, f), BF16)]),
        out_shape=jax.ShapeDtypeStruct((n_tiles * tm, f), BF16),
        compiler_params=_params("arbitrary"),
        name="moe_up",
    )(tile_expert, tile_first, n_valid, xs, w_gate, b_gate.reshape(depth, e, 1, f), w_up, b_up.reshape(depth, e, 1, f))
    return pl.pallas_call(
        functools.partial(_moe_down_kernel, n_tiles=n_tiles, spare_row0=(n_out_rows - 2 * tm) * SUBLANES),
        grid_spec=pltpu.PrefetchScalarGridSpec(
            num_scalar_prefetch=3, grid=(n_tiles,),
            in_specs=[smem_rows(lambda i, te, first, nvalid: (i, 0, 0)),
                      smem_rows(lambda i, te, first, nvalid: (jnp.maximum(i - 1, 0), 0, 0)),
                      row_spec(f), _expert_spec(layer, (f, d)), _expert_spec(layer, (1, d))],
            out_specs=pl.BlockSpec(memory_space=pl.ANY),
            scratch_shapes=[pltpu.VMEM(buf_shape, jnp.uint32), pltpu.SemaphoreType.DMA((2,)),
                            pltpu.VMEM((f, d), BF16)]),
        out_shape=jax.ShapeDtypeStruct((n_out_rows * SUBLANES, LANES), jnp.uint32),
        compiler_params=_params("arbitrary"),
        name="moe_down",
    )(tile_expert, tile_first, n_valid, slot_dst, slot_dst, act, w_down, b_down.reshape(depth, e, 1, d))


def _combine_kernel(*refs):
    x_ref, gate_ref, w_ref = refs[0], refs[1], refs[2]
    y_refs = refs[3:3 + TOP_K]
    o_ref = refs[3 + TOP_K]
    ts = x_ref.shape[0]
    w = w_ref[...]
    cols = None
    for k in range(TOP_K):
        chunks = [w[:, k:k + 1] * c for c in _load_token_tiles(y_refs[k], ts)]
        cols = chunks if cols is None else [a + c for a, c in zip(cols, chunks)]
    o_ref[...] = x_ref[...] + gate_ref[...] * jnp.concatenate(cols, axis=1)


def _moe_combine(x, gate, y, weights):
    b, s, d = x.shape
    ts = min(s, 512)
    per_b = s // ts
    tiles_per_k = b * per_b
    y_spec = lambda k: pl.BlockSpec((ts * SUBLANES, LANES), lambda i, j: (k * tiles_per_k + i * per_b + j, 0))
    return pl.pallas_call(
        _combine_kernel,
        grid=(b, per_b),
        in_specs=[pl.BlockSpec((None, ts, d), lambda i, j: (i, j, 0)),
                  pl.BlockSpec((None, 1, d), lambda i, j: (i, 0, 0)),
                  pl.BlockSpec((ts, LANES), lambda i, j: (i * per_b + j, 0))]
                 + [y_spec(k) for k in range(TOP_K)],
        out_specs=pl.BlockSpec((None, ts, d), lambda i, j: (i, j, 0)),
        out_shape=jax.ShapeDtypeStruct((b, s, d), F32),
        compiler_params=_params("arbitrary", "arbitrary"),
        name="moe_combine",
    )(x, gate.reshape(b, 1, d), weights, *([y] * TOP_K))


def _moe_ffn(x, h, gate, layer, w_router, b_router, w_gate, b_gate, w_up, b_up, w_down, b_down):
    b, s, d = x.shape
    t = b * s
    e = w_router.shape[1]
    tm = 512
    h2 = h.reshape(t * SUBLANES, LANES)
    idx_pad, wgt_pad, rank_pad, counts_pad = _route(h2, w_router, b_router)
    idx = idx_pad[:, :TOP_K]
    counts = counts_pad[0, :e]

    padded = ((counts + tm - 1) // tm) * tm
    ends = jnp.cumsum(padded)
    starts = ends - padded
    pos = starts[idx] + rank_pad[:, :TOP_K]
    n_slots = t * TOP_K + e * tm
    n_tiles = n_slots // tm
    tile_start = jnp.arange(n_tiles, dtype=jnp.int32) * tm
    tile_expert = jnp.sum((tile_start[:, None] >= ends[None, :]).astype(jnp.int32), axis=1)
    tile_expert = jnp.minimum(tile_expert, e - 1)
    n_valid = (ends[-1:] // tm).astype(jnp.int32)
    tile_first = jnp.concatenate([jnp.ones((1,), jnp.int32),
                                  (tile_expert[1:] != tile_expert[:-1]).astype(jnp.int32)])

    tail = jnp.stack([ends[-1], n_tiles - ends[-1] // tm]).astype(jnp.int32)
    xs = _moe_dispatch(h2, pos.reshape(-1) * SUBLANES, starts + counts, padded - counts, tail, n_slots, tm)

    n_pairs = t * TOP_K
    slot_pair = jnp.full((n_slots,), -1, jnp.int32).at[pos.reshape(-1)].set(
        jnp.arange(n_pairs, dtype=jnp.int32), unique_indices=True)
    slot_id = jnp.arange(n_slots, dtype=jnp.int32)
    spare = n_pairs + ((slot_id // tm) % 2) * tm + slot_id % tm
    pair_token, pair_k = slot_pair // TOP_K, slot_pair % TOP_K
    slot_dst = (jnp.where(slot_pair >= 0, pair_k * t + pair_token, spare) * SUBLANES).reshape(n_tiles, 1, tm)

    y = _moe_experts(xs, slot_dst, n_pairs + 2 * tm, tile_expert, tile_first, n_valid, layer,
                     w_gate, b_gate, w_up, b_up, w_down, b_down, tm)
    return _moe_combine(x, gate, y, wgt_pad)


def kernel(x, c, mod_w, mod_b, mix_norm_g, ffn_norm_g, ab_w_in, ab_w_out, moba_q_gain, moba_k_gain,
           fox_w_in, fox_b_f, fox_w_out, fox_q_gain, fox_k_gain, router_w, router_b,
           exp_w_gate, exp_b_gate, exp_w_up, exp_b_up, exp_w_down, exp_b_down):
    b, s, d = x.shape
    depth = mod_w.shape[0]
    n_heads = d // HEAD_DIM
    n_moba = n_heads // 2
    n_sb = n_heads - n_moba
    mod = _adaln_mod(c, mod_w, mod_b)

    for layer in range(depth):
        sh1, sc1, g1, sh2, sc2, g2 = [mod[layer, :, i * d:(i + 1) * d] for i in range(N_MOD)]
        j = layer // 2
        h = _norm_mod(x, mix_norm_g[layer], sc1, sh1)
        if layer % 2 == 0:
            wa = n_moba * HEAD_DIM
            wb = n_sb * HEAD_DIM
            col_scale = jnp.concatenate([
                jnp.tile(moba_q_gain[j] * Q_PRESCALE, n_moba), jnp.tile(moba_k_gain[j], n_moba),
                jnp.ones((wa,), F32), jnp.full((wb,), Q_PRESCALE, F32), jnp.ones((2 * wb,), F32)])
            n_cols = col_scale.shape[0]
            proj = _in_proj(h.reshape(b * s, d), ab_w_in[j].astype(BF16), col_scale.reshape(1, n_cols), 2 * wa)
            proj = proj.reshape(b, s, n_cols)
            o_a = _moba_attention(proj, n_moba, 0)
            o_b = _sb_attention(proj, n_sb, 3 * n_moba)
            w_out = ab_w_out[j].astype(BF16)
            x = _out_proj_residual(x, g1, [(o_a, w_out[:wa]), (o_b, w_out[wa:])])
        else:
            w = n_heads * HEAD_DIM
            col_scale = jnp.concatenate([jnp.tile(fox_q_gain[j] * Q_PRESCALE, n_heads),
                                         jnp.tile(fox_k_gain[j], n_heads), jnp.ones((w,), F32)])
            proj = _in_proj(h.reshape(b * s, d), fox_w_in[j, :, :3 * w].astype(BF16),
                            col_scale.reshape(1, 3 * w), 2 * w)
            cum_f = _forget_cumsum(h, fox_w_in[j, :, 3 * w:], fox_b_f[j])
            o = _fox_attention(proj.reshape(b, s, 3 * w), cum_f, n_heads)
            x = _out_proj_residual(x, g1, [(o, fox_w_out[j].astype(BF16))])
        h = _norm_mod(x, ffn_norm_g[layer], sc2, sh2, packed=True)
        x = _moe_ffn(x, h, g2, layer, router_w[layer], router_b[layer], exp_w_gate, exp_b_gate,
                     exp_w_up, exp_b_up, exp_w_down, exp_b_down)
    return x
```

```python
import functools
import math

import jax
import jax.numpy as jnp
from jax import lax
from jax.experimental import pallas as pl
from jax.experimental.pallas import tpu as pltpu

F32 = jnp.float32
BF16 = jnp.bfloat16

HEAD_DIM = 128
MOBA_BLOCK = 256
MOBA_TOPK = 3
TOP_K = 4
SWIGLU_LIMIT = 7.0
SWIGLU_ALPHA = 1.702
RMS_EPS = 1e-5
N_MOD = 6
LOG2E = math.log2(math.e)
Q_PRESCALE = HEAD_DIM ** -0.5 * LOG2E
LANES = 128
SUBLANES = 8
BF16_ROWS = 16
MASKED = -1e30
SB_EXIT_BITS = 160.0
ROW_DMA_UNROLL = 16
VMEM_LIMIT_BYTES = 56 * 1024 * 1024


def _params(*semantics):
    return pltpu.CompilerParams(dimension_semantics=semantics, vmem_limit_bytes=VMEM_LIMIT_BYTES)


def _dot(a, b):
    return jnp.dot(a, b, preferred_element_type=F32)


def _dot_nt(a, b):
    return lax.dot_general(a, b, (((1,), (1,)), ((), ())), preferred_element_type=F32)


def _split_bf16(x):
    hi = x.astype(BF16)
    lo = (x - hi.astype(F32)).astype(BF16)
    return hi, lo


def _kv_block(ref, j, tk):
    return ref[pl.ds(pl.multiple_of(j * tk, tk), tk), :]


def _pack_bf16_pair(lo, hi):
    lo_bits = lax.bitcast_convert_type(lo.astype(BF16).astype(F32), jnp.uint32)
    hi_bits = lax.bitcast_convert_type(hi.astype(BF16).astype(F32), jnp.uint32)
    return lax.shift_right_logical(lo_bits, jnp.uint32(16)) | (hi_bits & jnp.uint32(0xFFFF0000))


def _unpack_bf16_pair(word):
    lo = lax.bitcast_convert_type(lax.shift_left(word, jnp.uint32(16)), F32)
    hi = lax.bitcast_convert_type(word & jnp.uint32(0xFFFF0000), F32)
    return lo, hi


TOKEN_DIM = 2 * SUBLANES * LANES


def _store_token_tiles(ref, y, lead=()):
    n = y.shape[0]
    half = TOKEN_DIM // 2
    for s in range(SUBLANES):
        lo = y[:, s * LANES:(s + 1) * LANES]
        hi = y[:, half + s * LANES:half + (s + 1) * LANES]
        ref[lead + (pl.ds(s, n, stride=SUBLANES), slice(None))] = _pack_bf16_pair(lo, hi)


def _load_token_tiles(ref, n, lead=()):
    los, his = [], []
    for s in range(SUBLANES):
        lo, hi = _unpack_bf16_pair(ref[lead + (pl.ds(s, n, stride=SUBLANES), slice(None))])
        los.append(lo)
        his.append(hi)
    return los + his


def _load_token_tiles_bf16(ref, n, lead=()):
    return jnp.concatenate([c.astype(BF16) for c in _load_token_tiles(ref, n, lead)], axis=1)


def _mod_kernel(c_ref, w_ref, b_ref, o_ref):
    c = c_ref[...]
    c_act = c / (1.0 + jnp.exp(-c))
    o_ref[...] = _dot(c_act, w_ref[...]) + b_ref[...]


def _adaln_mod(c, mod_w, mod_b):
    depth, d, n = mod_w.shape
    b = c.shape[0]
    bp = -(-b // SUBLANES) * SUBLANES
    c_pad = jnp.zeros((bp, d), F32).at[:b].set(c)
    tn = min(n, 1024)
    out = pl.pallas_call(
        _mod_kernel,
        grid=(depth, n // tn),
        in_specs=[pl.BlockSpec((bp, d), lambda l, j: (0, 0)),
                  pl.BlockSpec((None, d, tn), lambda l, j: (l, 0, j)),
                  pl.BlockSpec((None, 1, tn), lambda l, j: (l, 0, j))],
        out_specs=pl.BlockSpec((None, bp, tn), lambda l, j: (l, 0, j)),
        out_shape=jax.ShapeDtypeStruct((depth, bp, n), F32),
        compiler_params=_params("arbitrary", "arbitrary"),
        name="adaln_mod",
    )(c_pad, mod_w, mod_b.reshape(depth, 1, n))
    return out[:, :b]


def _norm_kernel(x_ref, g_ref, sc_ref, sh_ref, o_ref, *, packed):
    x = x_ref[...]
    ms = jnp.mean(x * x, axis=-1, keepdims=True)
    y = x * lax.rsqrt(ms + RMS_EPS)
    y = (y * g_ref[...]) * (1.0 + sc_ref[...]) + sh_ref[...]
    if packed:
        _store_token_tiles(o_ref, y)
    else:
        o_ref[...] = y.astype(o_ref.dtype)


def _norm_mod(x, gain, scale, shift, packed=False):
    b, s, d = x.shape
    ts = min(s, 512)
    if packed:
        assert d == TOKEN_DIM
        out_block, out_shape = (None, ts * SUBLANES, LANES), (b, s * SUBLANES, LANES)
    else:
        out_block, out_shape = (None, ts, d), (b, s, d)
    return pl.pallas_call(
        functools.partial(_norm_kernel, packed=packed),
        grid=(b, s // ts),
        in_specs=[pl.BlockSpec((None, ts, d), lambda i, j: (i, j, 0)),
                  pl.BlockSpec((1, d), lambda i, j: (0, 0)),
                  pl.BlockSpec((None, 1, d), lambda i, j: (i, 0, 0)),
                  pl.BlockSpec((None, 1, d), lambda i, j: (i, 0, 0))],
        out_specs=pl.BlockSpec(out_block, lambda i, j: (i, j, 0)),
        out_shape=jax.ShapeDtypeStruct(out_shape, jnp.uint32 if packed else BF16),
        compiler_params=_params("arbitrary", "arbitrary"),
        name="norm_mod",
    )(x, gain.reshape(1, d), scale.reshape(b, 1, d), shift.reshape(b, 1, d))


def _inproj_kernel(h_ref, w_ref, gain_ref, o_ref, *, n_norm_tiles):
    acc = _dot(h_ref[...], w_ref[...])
    j = pl.program_id(0)

    @pl.when(j < n_norm_tiles)
    def _():
        for g in range(acc.shape[1] // HEAD_DIM):
            sl = slice(g * HEAD_DIM, (g + 1) * HEAD_DIM)
            blk = acc[:, sl]
            ms = jnp.mean(blk * blk, axis=-1, keepdims=True)
            o_ref[:, sl] = (blk * lax.rsqrt(ms + RMS_EPS) * gain_ref[:, sl]).astype(o_ref.dtype)

    @pl.when(j >= n_norm_tiles)
    def _():
        o_ref[...] = (acc * gain_ref[...]).astype(o_ref.dtype)


def _in_proj(h, w, col_scale, n_norm_cols):
    t, d = h.shape
    n = w.shape[1]
    tm = min(t, 1024)
    tn = min(n, 2048)
    assert n_norm_cols % tn == 0 and n % tn == 0 and t % tm == 0
    return pl.pallas_call(
        functools.partial(_inproj_kernel, n_norm_tiles=n_norm_cols // tn),
        grid=(n // tn, t // tm),
        in_specs=[pl.BlockSpec((tm, d), lambda j, i: (i, 0)),
                  pl.BlockSpec((d, tn), lambda j, i: (0, j)),
                  pl.BlockSpec((1, tn), lambda j, i: (0, j))],
        out_specs=pl.BlockSpec((tm, tn), lambda j, i: (i, j)),
        out_shape=jax.ShapeDtypeStruct((t, n), BF16),
        compiler_params=_params("arbitrary", "arbitrary"),
        name="in_proj",
    )(h, w, col_scale)


def _fgate_kernel(h_ref, wf_ref, bf_ref, o_ref, carry_ref):
    @pl.when(pl.program_id(1) == 0)
    def _():
        carry_ref[...] = jnp.zeros_like(carry_ref)

    ts = h_ref.shape[0]
    logit = _dot(h_ref[...], wf_ref[...]) + bf_ref[...]
    log_f = jnp.minimum(logit, 0.0) - jnp.log(1.0 + jnp.exp(-jnp.abs(logit)))
    row = lax.broadcasted_iota(jnp.int32, (ts, ts), 0)
    col = lax.broadcasted_iota(jnp.int32, (ts, ts), 1)
    lower = jnp.where(col <= row, 1.0, 0.0).astype(BF16)
    hi, lo = _split_bf16(log_f)
    cum = _dot(lower, hi) + _dot(lower, lo) + carry_ref[0:1, :]
    o_ref[...] = cum
    carry_ref[...] = jnp.broadcast_to(cum[ts - 1:ts, :], carry_ref.shape)


def _forget_cumsum(h, w_f, b_f):
    b, s, d = h.shape
    nh = w_f.shape[1]
    assert nh <= LANES
    ts = min(s, 512)
    w_pad = jnp.zeros((d, LANES), BF16).at[:, :nh].set(w_f.astype(BF16))
    b_pad = jnp.zeros((1, LANES), F32).at[0, :nh].set(b_f)
    return pl.pallas_call(
        _fgate_kernel,
        grid=(b, s // ts),
        in_specs=[pl.BlockSpec((None, ts, d), lambda i, j: (i, j, 0)),
                  pl.BlockSpec((d, LANES), lambda i, j: (0, 0)),
                  pl.BlockSpec((1, LANES), lambda i, j: (0, 0))],
        out_specs=pl.BlockSpec((None, ts, LANES), lambda i, j: (i, j, 0)),
        out_shape=jax.ShapeDtypeStruct((b, s, LANES), F32),
        scratch_shapes=[pltpu.VMEM((SUBLANES, LANES), F32)],
        compiler_params=_params("arbitrary", "arbitrary"),
        name="forget_cumsum",
    )(h, w_pad, b_pad)


VT_ROWS = HEAD_DIM + BF16_ROWS


def _store_v_transposed(v_ref, vt_ref, c, tile):
    vb = _kv_block(v_ref, c, tile).astype(F32)
    vt_ref[c, 0:HEAD_DIM, :] = vb.T.astype(BF16)
    vt_ref[c, HEAD_DIM:VT_ROWS, :] = jnp.ones((BF16_ROWS, tile), BF16)


def _softmax_accumulate(s, vt_blk, m_ref, acc_ref):
    m_old = m_ref[...]
    m_new = jnp.maximum(m_old, jnp.max(s, axis=0, keepdims=True))
    alpha = jnp.exp2(m_old - m_new)
    p = jnp.exp2((s - m_new).astype(BF16))
    m_ref[...] = m_new
    acc_ref[...] = alpha * acc_ref[...] + _dot(vt_blk, p)


def _softmax_finish(acc_ref, o_ref, qi, tile):
    acc = acc_ref[...]
    out_t = acc[0:HEAD_DIM, :] / acc[HEAD_DIM:HEAD_DIM + 1, :]
    o_ref[pl.ds(pl.multiple_of(qi * tile, tile), tile), :] = out_t.T.astype(o_ref.dtype)


def _flash_sweep(qi, n_past, produce, s0_ref, s1_ref, vt_ref, m_ref, acc_ref, tile):
    m_ref[...] = jnp.full(m_ref.shape, MASKED, F32)
    acc_ref[...] = jnp.zeros_like(acc_ref)
    key = lax.broadcasted_iota(jnp.int32, (tile, tile), 0)
    qry = lax.broadcasted_iota(jnp.int32, (tile, tile), 1)
    s0_ref[...] = jnp.where(key <= qry, produce(qi), MASKED)
    n_pos = n_past + 1
    farthest = qi - n_past

    def pair(i):
        blk = qi - 2 * i
        s1_ref[...] = produce(blk - 1)
        _softmax_accumulate(s0_ref[...], vt_ref[blk], m_ref, acc_ref)
        s0_ref[...] = produce(jnp.maximum(blk - 2, farthest))
        _softmax_accumulate(s1_ref[...], vt_ref[blk - 1], m_ref, acc_ref)

    def two_pairs(i, carry):
        pair(2 * i)
        pair(2 * i + 1)
        return carry

    n_pairs = n_pos // 2
    lax.fori_loop(0, n_pairs // 2, two_pairs, 0)

    @pl.when(lax.rem(n_pairs, 2) == 1)
    def _():
        pair(n_pairs - 1)

    @pl.when(lax.rem(n_pos, 2) == 1)
    def _():
        _softmax_accumulate(s0_ref[...], vt_ref[farthest], m_ref, acc_ref)


SKIP_BITS = 162.0


def _logit_spread_bound(q, kmax_sq_ref):
    q32 = q.astype(F32)
    qmax_sq = jnp.max(jnp.sum(q32 * q32, axis=-1, keepdims=True), axis=0, keepdims=True)
    return 2.0 * jnp.sqrt(qmax_sq * kmax_sq_ref[0:1, 0:1])


def _update_kmax_sq(kmax_sq_ref, kb):
    blk_max = jnp.max(jnp.sum(kb * kb, axis=-1, keepdims=True), axis=0, keepdims=True)
    kmax_sq_ref[...] = jnp.maximum(kmax_sq_ref[...], jnp.broadcast_to(blk_max, kmax_sq_ref.shape))


def _fox_kernel(q_ref, k_ref, v_ref, f_ref, o_ref, vt_ref, frep_ref, fend_ref, kmax_sq_ref, s0_ref, s1_ref, m_ref,
                acc_ref, *, tile, n_kv):
    head = pl.program_id(1)
    lane = lax.broadcasted_iota(jnp.int32, (tile, LANES), 1)
    fend_ref[...] = jnp.zeros_like(fend_ref)
    kmax_sq_ref[...] = jnp.zeros_like(kmax_sq_ref)

    def prep(c, carry):
        _store_v_transposed(v_ref, vt_ref, c, tile)
        _update_kmax_sq(kmax_sq_ref, _kv_block(k_ref, c, tile).astype(F32))
        f_blk = _kv_block(f_ref, c, tile)
        f_col = jnp.sum(jnp.where(lane == head, f_blk, 0.0), axis=-1, keepdims=True)
        f_rep = jnp.broadcast_to(f_col * LOG2E, (tile, LANES))
        frep_ref[pl.ds(pl.multiple_of(c * tile, tile), tile), :] = f_rep
        fend_ref[pl.ds(c, 1), :] = f_rep[tile - 1:tile, :]
        return carry

    lax.fori_loop(0, n_kv, prep, 0)

    def query_tile(qi, carry):
        qi = jnp.asarray(qi, jnp.int32)
        q = _kv_block(q_ref, qi, tile)
        f_first = frep_ref[pl.ds(pl.multiple_of(qi * tile, tile), 1), :]
        gap = fend_ref[...] - f_first
        block_id = lax.broadcasted_iota(jnp.int32, fend_ref.shape, 0)
        needed = jnp.logical_and(block_id < qi, gap <= _logit_spread_bound(q, kmax_sq_ref) + SKIP_BITS)
        n_past = jnp.sum(jnp.where(needed, 1.0, 0.0)[:, 0:1]).astype(jnp.int32)

        def produce(j):
            f_rep = _kv_block(frep_ref, j, tile)
            return _dot_nt(_kv_block(k_ref, j, tile), q) - jnp.concatenate([f_rep] * (tile // LANES), axis=1)

        _flash_sweep(qi, n_past, produce, s0_ref, s1_ref, vt_ref, m_ref, acc_ref, tile)
        _softmax_finish(acc_ref, o_ref, qi, tile)
        return carry

    lax.fori_loop(0, n_kv, query_tile, 0)


def _fox_attention(proj, cum_f, n_heads):
    b, s, _ = proj.shape
    tile = min(s, 512)
    n_kv = s // tile
    return pl.pallas_call(
        functools.partial(_fox_kernel, tile=tile, n_kv=n_kv),
        grid=(b, n_heads),
        in_specs=[pl.BlockSpec((None, s, HEAD_DIM), lambda i, h: (i, 0, h)),
                  pl.BlockSpec((None, s, HEAD_DIM), lambda i, h: (i, 0, n_heads + h)),
                  pl.BlockSpec((None, s, HEAD_DIM), lambda i, h: (i, 0, 2 * n_heads + h)),
                  pl.BlockSpec((None, s, LANES), lambda i, h: (i, 0, 0))],
        out_specs=pl.BlockSpec((None, s, HEAD_DIM), lambda i, h: (i, 0, h)),
        out_shape=jax.ShapeDtypeStruct((b, s, n_heads * HEAD_DIM), BF16),
        scratch_shapes=[pltpu.VMEM((n_kv, VT_ROWS, tile), BF16), pltpu.VMEM((s, LANES), F32),
                        pltpu.VMEM((-(-n_kv // SUBLANES) * SUBLANES, LANES), F32),
                        pltpu.VMEM((SUBLANES, LANES), F32),
                        pltpu.VMEM((tile, tile), F32), pltpu.VMEM((tile, tile), F32),
                        pltpu.VMEM((1, tile), F32),
                        pltpu.VMEM((VT_ROWS, tile), F32)],
        compiler_params=_params("arbitrary", "arbitrary"),
        name="fox_attention",
    )(proj, proj, proj, cum_f)


def _moba_kernel(q_ref, k_ref, v_ref, slope_ref, o_ref, kmean_ref, kmax_sq_ref, vt_ref, sel_ref, alibi_ref,
                 s0_ref, s1_ref, m_ref, acc_ref, *, tile, n_blocks):
    blk = MOBA_BLOCK
    per_tile = tile // blk
    kmean_ref[...] = jnp.zeros_like(kmean_ref)
    kmax_sq_ref[...] = jnp.zeros_like(kmax_sq_ref)

    def block_mean(n, carry):
        kb = _kv_block(k_ref, n, blk).astype(F32)
        kmean_ref[pl.ds(n, 1), :] = jnp.mean(kb, axis=0, keepdims=True)
        _update_kmax_sq(kmax_sq_ref, kb)
        return carry

    def transpose_v(c, carry):
        _store_v_transposed(v_ref, vt_ref, c, tile)
        return carry

    lax.fori_loop(0, n_blocks, block_mean, 0)
    lax.fori_loop(0, n_blocks // per_tile, transpose_v, 0)

    slope = slope_ref[...] * LOG2E
    alibi_ref[...] = slope * lax.broadcasted_iota(jnp.int32, (tile, tile), 0).astype(F32)
    block_id = lax.broadcasted_iota(jnp.int32, (LANES, tile), 0).astype(F32)
    qry_block = jnp.floor(lax.broadcasted_iota(jnp.int32, (LANES, tile), 1).astype(F32) * (1.0 / blk))

    def query_tile(qi, carry):
        qi = jnp.asarray(qi, jnp.int32)
        q = _kv_block(q_ref, qi, tile)
        own = jnp.asarray(qi * per_tile, F32) + qry_block
        gate = _dot_nt(kmean_ref[...].astype(BF16), q)
        gate = jnp.where(block_id < own, gate, -jnp.inf)
        sel = jnp.where(block_id == own, 1.0, 0.0)
        for slot in range(MOBA_TOPK):
            best = jnp.max(gate, axis=0, keepdims=True)
            idx = jnp.min(jnp.where(gate == best, block_id, float(LANES)), axis=0, keepdims=True)
            hit = block_id == idx
            slot_ok = jnp.where(float(slot) < own, 1.0, 0.0)
            sel = jnp.maximum(sel, jnp.where(hit, slot_ok, 0.0))
            gate = jnp.where(hit, -jnp.inf, gate)
        sel_ref[...] = jnp.where(sel > 0.5, 0.0, MASKED)

        def produce(j):
            shift = slope * jnp.asarray((j - qi) * tile, F32)
            per_query = jnp.concatenate(
                [jnp.broadcast_to(sel_ref[pl.ds(j * per_tile + r, 1), :] + shift, (blk, tile))
                 for r in range(per_tile)], axis=0)
            return (_dot_nt(_kv_block(k_ref, j, tile), q) + alibi_ref[...]) + per_query

        reach = (_logit_spread_bound(q, kmax_sq_ref) + SKIP_BITS) / slope
        tiles_back = jnp.floor((reach - 1.0) * (1.0 / tile)) + 2.0
        n_past = jnp.minimum(jnp.minimum(jnp.max(tiles_back), float(n_blocks)).astype(jnp.int32), qi)

        _flash_sweep(qi, n_past, produce, s0_ref, s1_ref, vt_ref, m_ref, acc_ref, tile)
        _softmax_finish(acc_ref, o_ref, qi, tile)
        return carry

    lax.fori_loop(0, n_blocks // per_tile, query_tile, 0)


def _moba_attention(proj, n_heads, col_block):
    b, s, _ = proj.shape
    blk = MOBA_BLOCK
    n_blocks = s // blk
    tile = min(s, 2 * blk)
    assert s % tile == 0 and tile % blk == 0 and n_blocks <= LANES
    slopes = jnp.asarray([2.0 ** (-8.0 * (i + 1) / n_heads) for i in range(n_heads)], F32)
    slope_rows = jnp.broadcast_to(slopes[:, None, None], (n_heads, 1, tile))
    return pl.pallas_call(
        functools.partial(_moba_kernel, tile=tile, n_blocks=n_blocks),
        grid=(b, n_heads),
        in_specs=[pl.BlockSpec((None, s, HEAD_DIM), lambda i, h: (i, 0, col_block + h)),
                  pl.BlockSpec((None, s, HEAD_DIM), lambda i, h: (i, 0, col_block + n_heads + h)),
                  pl.BlockSpec((None, s, HEAD_DIM), lambda i, h: (i, 0, col_block + 2 * n_heads + h)),
                  pl.BlockSpec((None, 1, tile), lambda i, h: (h, 0, 0))],
        out_specs=pl.BlockSpec((None, s, HEAD_DIM), lambda i, h: (i, 0, h)),
        out_shape=jax.ShapeDtypeStruct((b, s, n_heads * HEAD_DIM), BF16),
        scratch_shapes=[pltpu.VMEM((LANES, HEAD_DIM), F32), pltpu.VMEM((SUBLANES, LANES), F32),
                        pltpu.VMEM((s // tile, VT_ROWS, tile), BF16),
                        pltpu.VMEM((LANES, tile), F32), pltpu.VMEM((tile, tile), F32),
                        pltpu.VMEM((tile, tile), F32), pltpu.VMEM((tile, tile), F32),
                        pltpu.VMEM((1, tile), F32), pltpu.VMEM((VT_ROWS, tile), F32)],
        compiler_params=_params("arbitrary", "arbitrary"),
        name="moba_attention",
    )(proj, proj, proj, slope_rows)


def _sb_kernel(q_ref, k_ref, v_ref, o_ref, after_ref, decay_ref, acc_ref, *, tile, n_q):
    row = lax.broadcasted_iota(jnp.int32, (tile, tile), 0)
    col = lax.broadcasted_iota(jnp.int32, (tile, tile), 1)
    after_ref[...] = jnp.where(row > col, 1.0, 0.0).astype(BF16)

    def query_tile(qi, carry):
        qi = jnp.asarray(qi, jnp.int32)
        q = _kv_block(q_ref, qi, tile)
        causal = col < row

        def logits_and_drop(j, masked):
            z = _dot_nt(q, _kv_block(k_ref, j, tile))
            drop = jnp.maximum(z, 0.0) + jnp.log2(1.0 + jnp.exp2(-jnp.abs(z)))
            if masked:
                drop = jnp.where(causal, drop, 0.0)
            return z, drop

        def weighted_values(j, z, drop, decay_right, masked):
            hi, lo = _split_bf16(drop)
            after = after_ref[...]
            decay = _dot(hi, after) + _dot(lo, after) + decay_right
            a = jnp.exp2(z - drop - decay)
            if masked:
                a = jnp.where(causal, a, 0.0)
            return _dot(a.astype(BF16), _kv_block(v_ref, j, tile))

        @pl.when(qi == 0)
        def _():
            z, drop = logits_and_drop(qi, True)
            acc_ref[...] = weighted_values(qi, z, drop, 0.0, True)
            decay_ref[...] = jnp.sum(drop, axis=-1, keepdims=True)

        @pl.when(qi > 0)
        def _():
            z_d, drop_d = logits_and_drop(qi, True)
            z_p, drop_p = logits_and_drop(qi - 1, False)
            decay_d = jnp.sum(drop_d, axis=-1, keepdims=True)
            acc_ref[...] = (weighted_values(qi, z_d, drop_d, 0.0, True)
                            + weighted_values(qi - 1, z_p, drop_p, decay_d, False))
            decay_ref[...] = decay_d + jnp.sum(drop_p, axis=-1, keepdims=True)

        def cond(state):
            jj, least = state
            return jnp.logical_and(jj < qi, least < SB_EXIT_BITS)

        def body(state):
            jj, _ = state
            j = qi - 1 - jj
            z, drop = logits_and_drop(j, False)
            acc_ref[...] += weighted_values(j, z, drop, decay_ref[...], False)
            decay_ref[...] += jnp.sum(drop, axis=-1, keepdims=True)
            return jj + 1, jnp.min(decay_ref[...])

        lax.while_loop(cond, body, (jnp.int32(1), jnp.min(decay_ref[...])))
        o_ref[pl.ds(pl.multiple_of(qi * tile, tile), tile), :] = acc_ref[...].astype(o_ref.dtype)
        return carry

    lax.fori_loop(0, n_q, query_tile, 0)


def _sb_attention(proj, n_heads, col_block):
    b, s, _ = proj.shape
    tile = min(s, 256)
    return pl.pallas_call(
        functools.partial(_sb_kernel, tile=tile, n_q=s // tile),
        grid=(b, n_heads),
        in_specs=[pl.BlockSpec((None, s, HEAD_DIM), lambda i, h: (i, 0, col_block + h)),
                  pl.BlockSpec((None, s, HEAD_DIM), lambda i, h: (i, 0, col_block + n_heads + h)),
                  pl.BlockSpec((None, s, HEAD_DIM), lambda i, h: (i, 0, col_block + 2 * n_heads + h))],
        out_specs=pl.BlockSpec((None, s, HEAD_DIM), lambda i, h: (i, 0, h)),
        out_shape=jax.ShapeDtypeStruct((b, s, n_heads * HEAD_DIM), BF16),
        scratch_shapes=[pltpu.VMEM((tile, tile), BF16), pltpu.VMEM((tile, 1), F32),
                        pltpu.VMEM((tile, HEAD_DIM), F32)],
        compiler_params=_params("arbitrary", "arbitrary"),
        name="sb_attention",
    )(proj, proj, proj)


def _outproj_kernel(*refs, n_parts):
    x_ref, gate_ref = refs[0], refs[1]
    parts = refs[2:2 + 2 * n_parts]
    o_ref = refs[2 + 2 * n_parts]
    y = _dot(parts[0][...], parts[1][...])
    for p in range(1, n_parts):
        y += _dot(parts[2 * p][...], parts[2 * p + 1][...])
    o_ref[...] = x_ref[...] + gate_ref[...] * y


def _out_proj_residual(x, gate, parts):
    b, s, d = x.shape
    ts = min(s, 512)
    in_specs = [pl.BlockSpec((None, ts, d), lambda i, j: (i, j, 0)),
                pl.BlockSpec((None, 1, d), lambda i, j: (i, 0, 0))]
    args = [x, gate.reshape(b, 1, d)]
    for a, w in parts:
        kp = a.shape[-1]
        in_specs += [pl.BlockSpec((None, ts, kp), lambda i, j: (i, j, 0)),
                     pl.BlockSpec((kp, d), lambda i, j: (0, 0))]
        args += [a, w]
    return pl.pallas_call(
        functools.partial(_outproj_kernel, n_parts=len(parts)),
        grid=(b, s // ts),
        in_specs=in_specs,
        out_specs=pl.BlockSpec((None, ts, d), lambda i, j: (i, j, 0)),
        out_shape=jax.ShapeDtypeStruct((b, s, d), F32),
        compiler_params=_params("arbitrary", "arbitrary"),
        name="out_proj",
    )(*args)


def _router_kernel(h_ref, w_ref, b_ref, idx_ref, wgt_ref, rank_ref, cnt_ref, carry_ref):
    @pl.when(pl.program_id(0) == 0)
    def _():
        carry_ref[...] = jnp.zeros_like(carry_ref)

    tm = h_ref.shape[0] // SUBLANES
    logits = _dot(_load_token_tiles_bf16(h_ref, tm), w_ref[...]) + b_ref[...]
    lane = lax.broadcasted_iota(jnp.int32, (tm, LANES), 1).astype(F32)
    vals, idxs = [], []
    onehot = jnp.zeros((tm, LANES), F32)
    for _ in range(TOP_K):
        best = jnp.max(logits, axis=-1, keepdims=True)
        idx = jnp.min(jnp.where(logits == best, lane, float(LANES)), axis=-1, keepdims=True)
        hit = lane == idx
        onehot = jnp.where(hit, 1.0, onehot)
        logits = jnp.where(hit, -jnp.inf, logits)
        vals.append(best)
        idxs.append(idx)
    exps = [jnp.exp(v - vals[0]) for v in vals]
    denom = exps[0]
    for e in exps[1:]:
        denom = denom + e

    row = lax.broadcasted_iota(jnp.int32, (tm, tm), 0)
    col = lax.broadcasted_iota(jnp.int32, (tm, tm), 1)
    before = jnp.where(col < row, 1.0, 0.0).astype(BF16)
    prior = _dot(before, onehot.astype(BF16)) + carry_ref[0:1, :]

    idx_out = jnp.zeros((tm, LANES), F32)
    wgt_out = jnp.zeros((tm, LANES), F32)
    rank_out = jnp.zeros((tm, LANES), F32)
    for k in range(TOP_K):
        rank_k = jnp.sum(jnp.where(lane == idxs[k], prior, 0.0), axis=-1, keepdims=True)
        slot = lane == float(k)
        idx_out = jnp.where(slot, idxs[k], idx_out)
        wgt_out = jnp.where(slot, exps[k] / denom, wgt_out)
        rank_out = jnp.where(slot, rank_k, rank_out)
    idx_ref[...] = idx_out.astype(jnp.int32)
    wgt_ref[...] = wgt_out
    rank_ref[...] = rank_out.astype(jnp.int32)
    counts = carry_ref[...] + jnp.sum(onehot, axis=0, keepdims=True)
    carry_ref[...] = counts
    cnt_ref[...] = counts.astype(jnp.int32)


def _route(h, w_router, b_router):
    t = h.shape[0] // SUBLANES
    d = TOKEN_DIM
    e = w_router.shape[1]
    assert e <= LANES
    tm = min(t, 512)
    w_pad = jnp.zeros((d, LANES), BF16).at[:, :e].set(w_router.astype(BF16))
    b_pad = jnp.full((1, LANES), MASKED, F32).at[0, :e].set(b_router)
    tok_spec = pl.BlockSpec((tm, LANES), lambda i: (i, 0))
    return pl.pallas_call(
        _router_kernel,
        grid=(t // tm,),
        in_specs=[pl.BlockSpec((tm * SUBLANES, LANES), lambda i: (i, 0)),
                  pl.BlockSpec((d, LANES), lambda i: (0, 0)),
                  pl.BlockSpec((1, LANES), lambda i: (0, 0))],
        out_specs=[tok_spec, tok_spec, tok_spec, pl.BlockSpec((SUBLANES, LANES), lambda i: (0, 0))],
        out_shape=[jax.ShapeDtypeStruct((t, LANES), jnp.int32), jax.ShapeDtypeStruct((t, LANES), F32),
                   jax.ShapeDtypeStruct((t, LANES), jnp.int32),
                   jax.ShapeDtypeStruct((SUBLANES, LANES), jnp.int32)],
        scratch_shapes=[pltpu.VMEM((SUBLANES, LANES), F32)],
        compiler_params=_params("arbitrary"),
        name="moe_router",
    )(h, w_pad, b_pad)


def _start_rows(n_rows, copy_of_row):
    def body(p, carry):
        copy_of_row(2 * p).start(priority=0)
        copy_of_row(2 * p + 1).start(priority=1)
        return carry

    lax.fori_loop(0, n_rows // 2, body, 0, unroll=ROW_DMA_UNROLL // 2)


def _wait_rows(n_rows, copy_of_row):
    def body(r, carry):
        copy_of_row(r).wait()
        return carry

    lax.fori_loop(0, n_rows, body, 0, unroll=ROW_DMA_UNROLL)


def _moe_dispatch_kernel(padstart_ref, padlen_ref, tail_ref, pos_ref, h_ref, xs_hbm, zeros, sem, *, n_experts, tm):
    i = pl.program_id(0)
    n_tokens = h_ref.shape[0] // SUBLANES

    def zero_block(first_slot, n_slots):
        return pltpu.make_async_copy(
            zeros.at[pl.ds(0, n_slots * SUBLANES)],
            xs_hbm.at[pl.ds(pl.multiple_of(first_slot * SUBLANES, SUBLANES), n_slots * SUBLANES)], sem.at[0])

    def pad_blocks(e, fn):
        first = padstart_ref[e]
        size = tm // 2
        while size >= 1:
            used = (padlen_ref[e] & size) != 0

            @pl.when(used)
            def _(first=first, size=size):
                fn(zero_block(first, size))

            first = first + jnp.where(used, size, 0)
            size //= 2

    def tail_blocks(fn):
        def body(j, carry):
            fn(zero_block(tail_ref[0] + j * tm, tm))
            return carry

        lax.fori_loop(0, tail_ref[1], body, 0)

    @pl.when(i == 0)
    def _():
        zeros[...] = jnp.zeros_like(zeros)

        def start_expert(e, carry):
            pad_blocks(e, lambda copy: copy.start())
            return carry

        def wait_expert(e, carry):
            pad_blocks(e, lambda copy: copy.wait())
            return carry

        lax.fori_loop(0, n_experts, start_expert, 0)
        tail_blocks(lambda copy: copy.start())
        lax.fori_loop(0, n_experts, wait_expert, 0)
        tail_blocks(lambda copy: copy.wait())

    def pair_copy(token, k):
        src = h_ref.at[pl.ds(pl.multiple_of(token * SUBLANES, SUBLANES), SUBLANES)]
        dst = xs_hbm.at[pl.ds(pl.multiple_of(pos_ref[0, token * TOP_K + k], SUBLANES), SUBLANES)]
        return pltpu.make_async_copy(src, dst, sem.at[1])

    def start_token(token, carry):
        for k in range(TOP_K):
            pair_copy(token, k).start(priority=k % 2)
        return carry

    def wait_token(token, carry):
        for k in range(TOP_K):
            pair_copy(token, k).wait()
        return carry

    lax.fori_loop(0, n_tokens, start_token, 0, unroll=ROW_DMA_UNROLL // TOP_K)
    lax.fori_loop(0, n_tokens, wait_token, 0, unroll=ROW_DMA_UNROLL // TOP_K)


def _moe_dispatch(h_packed, pair_row, pad_start, pad_len, tail, n_slots, tm):
    t = h_packed.shape[0] // SUBLANES
    tt = min(t, 512)
    return pl.pallas_call(
        functools.partial(_moe_dispatch_kernel, n_experts=pad_start.shape[0], tm=tm),
        grid_spec=pltpu.PrefetchScalarGridSpec(
            num_scalar_prefetch=3, grid=(t // tt,),
            in_specs=[pl.BlockSpec((None, 1, tt * TOP_K), lambda i, *_: (i, 0, 0), memory_space=pltpu.SMEM),
                      pl.BlockSpec((tt * SUBLANES, LANES), lambda i, *_: (i, 0))],
            out_specs=pl.BlockSpec(memory_space=pl.ANY),
            scratch_shapes=[pltpu.VMEM((tm * SUBLANES, LANES), jnp.uint32), pltpu.SemaphoreType.DMA((2,))]),
        out_shape=jax.ShapeDtypeStruct((n_slots * SUBLANES, LANES), jnp.uint32),
        compiler_params=_params("arbitrary"),
        name="moe_dispatch",
    )(pad_start, pad_len, tail, pair_row.reshape(t // tt, 1, tt * TOP_K), h_packed)


def _moe_up_kernel(te_ref, first_ref, nvalid_ref, x_ref, wg_ref, bg_ref, wu_ref, bu_ref, o_ref, wg_bf, wu_bf):
    i = pl.program_id(0)
    n_valid = nvalid_ref[0]
    tm = x_ref.shape[0] // SUBLANES

    @pl.when(first_ref[i] == 1)
    def _():
        wg_bf[...] = wg_ref[...].astype(BF16)
        wu_bf[...] = wu_ref[...].astype(BF16)

    @pl.when(i < n_valid)
    def _():
        x = _load_token_tiles_bf16(x_ref, tm)
        g = jnp.minimum(_dot(x, wg_bf[...]) + bg_ref[...], SWIGLU_LIMIT)
        u = jnp.clip(_dot(x, wu_bf[...]) + bu_ref[...], -SWIGLU_LIMIT, SWIGLU_LIMIT)
        act = (u + 1.0) * (g / (1.0 + jnp.exp(-SWIGLU_ALPHA * g)))
        o_ref[...] = act.astype(o_ref.dtype)

    @pl.when(i >= n_valid)
    def _():
        o_ref[...] = jnp.zeros_like(o_ref)


def _moe_down_kernel(te_ref, first_ref, nvalid_ref, dst_ref, dst_prev_ref, a_ref, wd_ref, bd_ref, y_hbm,
                     obuf, sem, wd_bf, *, n_tiles, spare_row0):
    i = pl.program_id(0)
    slot = lax.rem(i, 2)
    n_valid = nvalid_ref[0]
    tm = obuf.shape[1] // SUBLANES

    def row_copy(dst_smem, buf, r):
        src = obuf.at[buf, pl.ds(pl.multiple_of(r * SUBLANES, SUBLANES), SUBLANES)]
        dst = y_hbm.at[pl.ds(pl.multiple_of(dst_smem[0, r], SUBLANES), SUBLANES)]
        return pltpu.make_async_copy(src, dst, sem.at[buf])

    @pl.when(i == 0)
    def _():
        obuf[...] = jnp.zeros_like(obuf)
        rows = tm * SUBLANES
        fills = [pltpu.make_async_copy(obuf.at[half], y_hbm.at[pl.ds(spare_row0 + half * rows, rows)], sem.at[half])
                 for half in range(2)]
        for fill in fills:
            fill.start()
        for fill in fills:
            fill.wait()

    @pl.when(first_ref[i] == 1)
    def _():
        wd_bf[...] = wd_ref[...].astype(BF16)

    @pl.when(i < n_valid)
    def _():
        _store_token_tiles(obuf, _dot(a_ref[...], wd_bf[...]) + bd_ref[...], lead=(slot,))
        _start_rows(tm, lambda r: row_copy(dst_ref, slot, r))

    @pl.when(jnp.logical_and(i >= 1, i - 1 < n_valid))
    def _():
        _wait_rows(tm, lambda r: row_copy(dst_prev_ref, 1 - slot, r))

    @pl.when(jnp.logical_and(i == n_tiles - 1, i < n_valid))
    def _():
        _wait_rows(tm, lambda r: row_copy(dst_ref, slot, r))


def _expert_spec(layer, shape):
    return pl.BlockSpec((None, None) + shape, lambda i, te, first, nvalid: (layer, te[i], 0, 0))


def _moe_experts(xs, slot_dst, n_out_rows, tile_expert, tile_first, n_valid, layer,
                 w_gate, b_gate, w_up, b_up, w_down, b_down, tm):
    d = TOKEN_DIM
    buf_shape = (2, tm * SUBLANES, LANES)
    depth, e, _, f = w_gate.shape
    n_tiles = slot_dst.shape[0]
    smem_rows = lambda index: pl.BlockSpec((None, 1, tm), index, memory_space=pltpu.SMEM)
    row_spec = lambda width: pl.BlockSpec((tm, width), lambda i, te, first, nvalid: (i, 0))
    act = pl.pallas_call(
        _moe_up_kernel,
        grid_spec=pltpu.PrefetchScalarGridSpec(
            num_scalar_prefetch=3, grid=(n_tiles,),
            in_specs=[pl.BlockSpec((tm * SUBLANES, LANES), lambda i, te, first, nvalid: (i, 0)),
                      _expert_spec(layer, (d, f)), _expert_spec(layer, (1, f)),
                      _expert_spec(layer, (d, f)), _expert_spec(layer, (1, f))],
            out_specs=row_spec(f),
            scratch_shapes=[pltpu.VMEM((d, f), BF16), pltpu.VMEM((d, f), BF16)]),
        out_shape=jax.ShapeDtypeStruct((n_tiles * tm, f), BF16),
        compiler_params=_params("arbitrary"),
        name="moe_up",
    )(tile_expert, tile_first, n_valid, xs, w_gate, b_gate.reshape(depth, e, 1, f), w_up, b_up.reshape(depth, e, 1, f))
    return pl.pallas_call(
        functools.partial(_moe_down_kernel, n_tiles=n_tiles, spare_row0=(n_out_rows - 2 * tm) * SUBLANES),
        grid_spec=pltpu.PrefetchScalarGridSpec(
            num_scalar_prefetch=3, grid=(n_tiles,),
            in_specs=[smem_rows(lambda i, te, first, nvalid: (i, 0, 0)),
                      smem_rows(lambda i, te, first, nvalid: (jnp.maximum(i - 1, 0), 0, 0)),
                      row_spec(f), _expert_spec(layer, (f, d)), _expert_spec(layer, (1, d))],
            out_specs=pl.BlockSpec(memory_space=pl.ANY),
            scratch_shapes=[pltpu.VMEM(buf_shape, jnp.uint32), pltpu.SemaphoreType.DMA((2,)),
                            pltpu.VMEM((f, d), BF16)]),
        out_shape=jax.ShapeDtypeStruct((n_out_rows * SUBLANES, LANES), jnp.uint32),
        compiler_params=_params("arbitrary"),
        name="moe_down",
    )(tile_expert, tile_first, n_valid, slot_dst, slot_dst, act, w_down, b_down.reshape(depth, e, 1, d))


def _combine_kernel(*refs):
    x_ref, gate_ref, w_ref = refs[0], refs[1], refs[2]
    y_refs = refs[3:3 + TOP_K]
    o_ref = refs[3 + TOP_K]
    ts = x_ref.shape[0]
    w = w_ref[...]
    cols = None
    for k in range(TOP_K):
        chunks = [w[:, k:k + 1] * c for c in _load_token_tiles(y_refs[k], ts)]
        cols = chunks if cols is None else [a + c for a, c in zip(cols, chunks)]
    o_ref[...] = x_ref[...] + gate_ref[...] * jnp.concatenate(cols, axis=1)


def _moe_combine(x, gate, y, weights):
    b, s, d = x.shape
    ts = min(s, 512)
    per_b = s // ts
    tiles_per_k = b * per_b
    y_spec = lambda k: pl.BlockSpec((ts * SUBLANES, LANES), lambda i, j: (k * tiles_per_k + i * per_b + j, 0))
    return pl.pallas_call(
        _combine_kernel,
        grid=(b, per_b),
        in_specs=[pl.BlockSpec((None, ts, d), lambda i, j: (i, j, 0)),
                  pl.BlockSpec((None, 1, d), lambda i, j: (i, 0, 0)),
                  pl.BlockSpec((ts, LANES), lambda i, j: (i * per_b + j, 0))]
                 + [y_spec(k) for k in range(TOP_K)],
        out_specs=pl.BlockSpec((None, ts, d), lambda i, j: (i, j, 0)),
        out_shape=jax.ShapeDtypeStruct((b, s, d), F32),
        compiler_params=_params("arbitrary", "arbitrary"),
        name="moe_combine",
    )(x, gate.reshape(b, 1, d), weights, *([y] * TOP_K))


def _moe_ffn(x, h, gate, layer, w_router, b_router, w_gate, b_gate, w_up, b_up, w_down, b_down):
    b, s, d = x.shape
    t = b * s
    e = w_router.shape[1]
    tm = 512
    h2 = h.reshape(t * SUBLANES, LANES)
    idx_pad, wgt_pad, rank_pad, counts_pad = _route(h2, w_router, b_router)
    idx = idx_pad[:, :TOP_K]
    counts = counts_pad[0, :e]

    padded = ((counts + tm - 1) // tm) * tm
    ends = jnp.cumsum(padded)
    starts = ends - padded
    pos = starts[idx] + rank_pad[:, :TOP_K]
    n_slots = t * TOP_K + e * tm
    n_tiles = n_slots // tm
    tile_start = jnp.arange(n_tiles, dtype=jnp.int32) * tm
    tile_expert = jnp.sum((tile_start[:, None] >= ends[None, :]).astype(jnp.int32), axis=1)
    tile_expert = jnp.minimum(tile_expert, e - 1)
    n_valid = (ends[-1:] // tm).astype(jnp.int32)
    tile_first = jnp.concatenate([jnp.ones((1,), jnp.int32),
                                  (tile_expert[1:] != tile_expert[:-1]).astype(jnp.int32)])

    tail = jnp.stack([ends[-1], n_tiles - ends[-1] // tm]).astype(jnp.int32)
    xs = _moe_dispatch(h2, pos.reshape(-1) * SUBLANES, starts + counts, padded - counts, tail, n_slots, tm)

    n_pairs = t * TOP_K
    slot_id = jnp.arange(n_slots, dtype=jnp.int32)
    by_slot = jnp.argsort(pos.reshape(-1)).astype(jnp.int32)
    slot_expert = jnp.repeat(tile_expert, tm)
    rank_in_expert = slot_id - starts[slot_expert]
    first_pair = jnp.cumsum(counts) - counts
    pair_at = jnp.clip(first_pair[slot_expert] + rank_in_expert, 0, n_pairs - 1)
    slot_pair = jnp.where(rank_in_expert < counts[slot_expert], by_slot[pair_at], -1)
    spare = n_pairs + ((slot_id // tm) % 2) * tm + slot_id % tm
    pair_token, pair_k = slot_pair // TOP_K, slot_pair % TOP_K
    slot_dst = (jnp.where(slot_pair >= 0, pair_k * t + pair_token, spare) * SUBLANES).reshape(n_tiles, 1, tm)

    y = _moe_experts(xs, slot_dst, n_pairs + 2 * tm, tile_expert, tile_first, n_valid, layer,
                     w_gate, b_gate, w_up, b_up, w_down, b_down, tm)
    return _moe_combine(x, gate, y, wgt_pad)


def kernel(x, c, mod_w, mod_b, mix_norm_g, ffn_norm_g, ab_w_in, ab_w_out, moba_q_gain, moba_k_gain,
           fox_w_in, fox_b_f, fox_w_out, fox_q_gain, fox_k_gain, router_w, router_b,
           exp_w_gate, exp_b_gate, exp_w_up, exp_b_up, exp_w_down, exp_b_down):
    b, s, d = x.shape
    depth = mod_w.shape[0]
    n_heads = d // HEAD_DIM
    n_moba = n_heads // 2
    n_sb = n_heads - n_moba
    mod = _adaln_mod(c, mod_w, mod_b)

    for layer in range(depth):
        sh1, sc1, g1, sh2, sc2, g2 = [mod[layer, :, i * d:(i + 1) * d] for i in range(N_MOD)]
        j = layer // 2
        h = _norm_mod(x, mix_norm_g[layer], sc1, sh1)
        if layer % 2 == 0:
            wa = n_moba * HEAD_DIM
            wb = n_sb * HEAD_DIM
            col_scale = jnp.concatenate([
                jnp.tile(moba_q_gain[j] * Q_PRESCALE, n_moba), jnp.tile(moba_k_gain[j], n_moba),
                jnp.ones((wa,), F32), jnp.full((wb,), Q_PRESCALE, F32), jnp.ones((2 * wb,), F32)])
            n_cols = col_scale.shape[0]
            proj = _in_proj(h.reshape(b * s, d), ab_w_in[j].astype(BF16), col_scale.reshape(1, n_cols), 2 * wa)
            proj = proj.reshape(b, s, n_cols)
            o_a = _moba_attention(proj, n_moba, 0)
            o_b = _sb_attention(proj, n_sb, 3 * n_moba)
            w_out = ab_w_out[j].astype(BF16)
            x = _out_proj_residual(x, g1, [(o_a, w_out[:wa]), (o_b, w_out[wa:])])
        else:
            w = n_heads * HEAD_DIM
            col_scale = jnp.concatenate([jnp.tile(fox_q_gain[j] * Q_PRESCALE, n_heads),
                                         jnp.tile(fox_k_gain[j], n_heads), jnp.ones((w,), F32)])
            proj = _in_proj(h.reshape(b * s, d), fox_w_in[j, :, :3 * w].astype(BF16),
                            col_scale.reshape(1, 3 * w), 2 * w)
            cum_f = _forget_cumsum(h, fox_w_in[j, :, 3 * w:], fox_b_f[j])
            o = _fox_attention(proj.reshape(b, s, 3 * w), cum_f, n_heads)
            x = _out_proj_residual(x, g1, [(o, fox_w_out[j].astype(BF16))])
        h = _norm_mod(x, ffn_norm_g[layer], sc2, sh2, packed=True)
        x = _moe_ffn(x, h, g2, layer, router_w[layer], router_b[layer], exp_w_gate, exp_b_gate,
                     exp_w_up, exp_b_up, exp_w_down, exp_b_down)
    return x
```
